```python
import math
import functools
import jax
import jax.numpy as jnp
from jax import lax
import numpy as np

D_MODEL = 1024
BATCH = 8
SEQ = 8192
DEPTH = 1
DEC_BATCH = 128
DEC_SEQ = 1
PAST_LEN = 8192
PAGE_SIZE = 128

SSM_D_INNER = 2 * D_MODEL
SSM_HEAD_DIM = 64
SSM_HEADS = SSM_D_INNER // SSM_HEAD_DIM
SSM_GROUPS = 4
SSM_D_STATE = 128
SSM_CONV_W = 4
SSM_CONV_DIM = SSM_D_INNER + 2 * SSM_GROUPS * SSM_D_STATE
SSM_CHUNK = 128
DT_MIN = 0.001
DT_MAX = 0.1
NSA_HEADS = 16
NSA_KV_HEADS = 4
NSA_HEAD_DIM = 64
NSA_GQA = NSA_HEADS // NSA_KV_HEADS
KV_COLS = 2 * NSA_KV_HEADS * NSA_HEAD_DIM
CMP_LEN = 32
CMP_STRIDE = 16
SEL_BLOCK = 64
SEL_TOP_N = 16
WINDOW = 512
Q_BLOCK = 128
ROPE_DIM = NSA_HEAD_DIM // 4
ROPE_THETA = 500000.0
FFN_HIDDEN = -(-(8 * D_MODEL) // (3 * 256)) * 256
IN_COLS = (SSM_D_INNER + SSM_CONV_DIM + SSM_HEADS + NSA_HEADS * NSA_HEAD_DIM
           + 3 * KV_COLS + 3 * NSA_HEADS + 2 * D_MODEL)
NORM_EPS = 1e-5
BIG = 1e30
DEEPNORM_ALPHA = (2 * DEPTH) ** 0.25
DEEPNORM_BETA = (8 * DEPTH) ** -0.25

kernel_name = 'hybrid_ssd_nsa_decoder_step'


def _in_split_points():
    sizes = (SSM_D_INNER, SSM_CONV_DIM, SSM_HEADS, NSA_HEADS * NSA_HEAD_DIM,
             KV_COLS, KV_COLS, KV_COLS, 3 * NSA_HEADS)
    points, acc = [], 0
    for s in sizes:
        acc += s
        points.append(acc)
    return points


def layer_norm(x, g, b):
    xf = x.astype(jnp.float32)
    mu = jnp.mean(xf, axis=-1, keepdims=True)
    var = jnp.mean(jnp.square(xf - mu), axis=-1, keepdims=True)
    return ((xf - mu) * lax.rsqrt(var + NORM_EPS) * g + b).astype(x.dtype)


def gated_rmsnorm(y, z, w):
    v = y.astype(jnp.float32) * jax.nn.silu(z.astype(jnp.float32))
    shp = v.shape
    v = v.reshape(shp[:-1] + (SSM_GROUPS, SSM_D_INNER // SSM_GROUPS))
    v = v * lax.rsqrt(jnp.mean(jnp.square(v), axis=-1, keepdims=True) + NORM_EPS)
    return (v.reshape(shp) * w).astype(z.dtype)


def rope_partial(x, pos):
    half = ROPE_DIM // 2
    inv_freq = jnp.power(ROPE_THETA, -jnp.arange(half, dtype=jnp.float32) * 2.0 / ROPE_DIM)
    ang = pos.astype(jnp.float32)[:, None] * inv_freq[None, :]
    cos = jnp.cos(ang)[:, None, :]
    sin = jnp.sin(ang)[:, None, :]
    xf = x.astype(jnp.float32)
    x1 = xf[..., :half]
    x2 = xf[..., half:ROPE_DIM]
    out = jnp.concatenate([x1 * cos - x2 * sin, x2 * cos + x1 * sin, xf[..., ROPE_DIM:]], axis=-1)
    return out.astype(x.dtype)


def masked_softmax(s, mask):
    s = jnp.where(mask, s.astype(jnp.float32), -jnp.inf)
    m = jnp.max(s, axis=-1, keepdims=True)
    m = jnp.where(jnp.isfinite(m), m, 0.0)
    e = jnp.exp(s - m)
    return e / jnp.maximum(jnp.sum(e, axis=-1, keepdims=True), 1e-30)


def causal_conv(xbc, buf, w, b):
    t = xbc.shape[1]
    xp = jnp.concatenate([buf.astype(xbc.dtype), xbc], axis=1)
    y = b + xp[:, 0:t] * w[0]
    for k in range(1, SSM_CONV_W):
        y = y + xp[:, k:k + t] * w[k]
    return jax.nn.silu(y), xp[:, t:]


def ssd_chunked(x, dt, a, bm, cm, state0):
    bsz, t, nh, p = x.shape
    g, n = bm.shape[2], bm.shape[3]
    r = nh // g
    nc = t // SSM_CHUNK
    xdt = (x.astype(jnp.float32) * dt[..., None]).reshape(bsz, nc, SSM_CHUNK, g, r, p)
    da = (dt * a).reshape(bsz, nc, SSM_CHUNK, g, r)
    bc = bm.astype(jnp.float32).reshape(bsz, nc, SSM_CHUNK, g, n)
    cc = cm.astype(jnp.float32).reshape(bsz, nc, SSM_CHUNK, g, n)
    causal = jnp.tril(jnp.ones((SSM_CHUNK, SSM_CHUNK), dtype=bool))[None, :, :, None, None]

    def chunk_step(state, inp):
        xdt_c, da_c, b_c, c_c = inp
        cum = jnp.cumsum(da_c, axis=1)
        seg = cum[:, :, None] - cum[:, None, :]
        decay = jnp.exp(jnp.where(causal, seg, -jnp.inf))
        cb = jnp.einsum('btgn,bsgn->btsg', c_c, b_c)
        y = jnp.einsum('btsg,btsgr,bsgrp->btgrp', cb, decay, xdt_c)
        y = y + jnp.einsum('btgn,bgrpn,btgr->btgrp', c_c, state, jnp.exp(cum))
        last = cum[:, -1]
        w_in = jnp.exp(last[:, None] - cum)
        state = (state * jnp.exp(last)[..., None, None]
                 + jnp.einsum('bsgn,bsgr,bsgrp->bgrpn', b_c, w_in, xdt_c))
        return state, y

    xs = tuple(jnp.moveaxis(v, 1, 0) for v in (xdt, da, bc, cc))
    state, ys = lax.scan(chunk_step, state0.reshape(bsz, g, r, p, n), xs)
    y = jnp.moveaxis(ys, 0, 1).reshape(bsz, t, nh, p)
    return y, state.reshape(bsz, nh, p, n)


def ssd_recurrent(x, dt, a, bm, cm, state0):
    nh = x.shape[2]
    r = nh // bm.shape[2]

    def step(state, inp):
        x_t, dt_t, b_t, c_t = inp
        b_h = jnp.repeat(b_t, r, axis=1)
        c_h = jnp.repeat(c_t, r, axis=1)
        state = (state * jnp.exp(dt_t * a)[..., None, None]
                 + jnp.einsum('bhp,bhn->bhpn', x_t * dt_t[..., None], b_h))
        return state, jnp.einsum('bhpn,bhn->bhp', state, c_h)

    xs = (jnp.moveaxis(x.astype(jnp.float32), 1, 0), jnp.moveaxis(dt, 1, 0),
          jnp.moveaxis(bm.astype(jnp.float32), 1, 0), jnp.moveaxis(cm.astype(jnp.float32), 1, 0))
    state, ys = lax.scan(step, state0, xs)
    return jnp.moveaxis(ys, 0, 1), state


def compress_kv(kv, cmp_pe, cmp_w1, cmp_w2):
    r = CMP_LEN // CMP_STRIDE
    sub = kv.reshape((kv.shape[0] // CMP_STRIDE, CMP_STRIDE) + kv.shape[1:])
    n_cmp = sub.shape[0] - r + 1
    blocks = jnp.concatenate([sub[i:i + n_cmp] for i in range(r)], axis=1)
    blocks = blocks + jnp.transpose(cmp_pe, (1, 0, 2))[None, :, :, None, :]
    hid = jax.nn.gelu(jnp.einsum('clvhd,vlde->cvhe', blocks, cmp_w1))
    return jnp.einsum('cvhe,vef->cvhf', hid, cmp_w2)


def selection_map(n_cmp, n_sel):
    i = jnp.arange(n_cmp)[:, None]
    j = jnp.arange(n_sel)[None, :]
    ov = (i * CMP_STRIDE < (j + 1) * SEL_BLOCK) & (i * CMP_STRIDE + CMP_LEN > j * SEL_BLOCK)
    return ov.astype(jnp.float32)


def compressed_ends(n_cmp):
    return jnp.arange(n_cmp, dtype=jnp.int32) * CMP_STRIDE + (CMP_LEN - 1)


def nsa_core(q, qr, qpos, kvc, c_end, smap, gather_sel, kvw, wpos, gates):
    nq = q.shape[0]
    scale = NSA_HEAD_DIM ** -0.5
    qg = q.reshape(nq, NSA_KV_HEADS, NSA_GQA, NSA_HEAD_DIM)
    qrg = qr.reshape(nq, NSA_KV_HEADS, NSA_GQA, NSA_HEAD_DIM)
    s_c = jnp.einsum('qhgd,chd->hgqc', qg, kvc[:, 0]) * scale
    p_c = masked_softmax(s_c, (c_end[None, :] <= qpos[:, None])[None, None])
    o_c = jnp.einsum('hgqc,chd->qhgd', p_c.astype(kvc.dtype), kvc[:, 1])
    imp = jnp.einsum('hgqc,cj->hqj', p_c, smap)
    n_sel = smap.shape[1]
    j = jnp.arange(n_sel)[None, :]
    cur = (qpos // SEL_BLOCK)[:, None]
    forced = (j == 0) | (j == cur) | (j == cur - 1)
    score = jnp.where(j > cur, -BIG, jnp.where(forced, BIG, imp))
    _, idx = lax.top_k(score, min(SEL_TOP_N, n_sel))
    k_s, v_s = gather_sel(idx)
    pos_s = idx[..., None] * SEL_BLOCK + jnp.arange(SEL_BLOCK)
    n_keys = idx.shape[-1] * SEL_BLOCK
    k_s = k_s.reshape(NSA_KV_HEADS, nq, n_keys, NSA_HEAD_DIM)
    v_s = v_s.reshape(NSA_KV_HEADS, nq, n_keys, NSA_HEAD_DIM)
    pos_s = pos_s.reshape(NSA_KV_HEADS, nq, n_keys)
    s_s = jnp.einsum('qhgd,hqkd->hgqk', qrg, k_s) * scale
    p_s = masked_softmax(s_s, (pos_s <= qpos[None, :, None])[:, None])
    o_s = jnp.einsum('hgqk,hqkd->qhgd', p_s.astype(v_s.dtype), v_s)
    s_w = jnp.einsum('qhgd,khd->hgqk', qrg, kvw[:, 0]) * scale
    dp = qpos[:, None] - wpos[None, :]
    m_w = (dp >= 0) & (dp <= WINDOW) & (wpos >= 0)[None, :]
    p_w = masked_softmax(s_w, m_w[None, None])
    o_w = jnp.einsum('hgqk,khd->qhgd', p_w.astype(kvw.dtype), kvw[:, 1])
    g = gates.reshape(nq, NSA_KV_HEADS, NSA_GQA, 3)
    o = g[..., 0:1] * o_c + g[..., 1:2] * o_s + g[..., 2:3] * o_w
    return o.reshape(nq, NSA_HEADS, NSA_HEAD_DIM)


def nsa_prompt(q, qr, kv_c, kv_s, kv_w, gates, cmp_pe, cmp_w1, cmp_w2):
    t = q.shape[1]
    n_sel = t // SEL_BLOCK
    hh = jnp.arange(NSA_KV_HEADS)[:, None, None]

    def one_seq(args):
        q1, qr1, c1, s1, w1, g1 = args
        kvc = compress_kv(c1, cmp_pe, cmp_w1, cmp_w2)
        n_cmp = kvc.shape[0]
        c_end = compressed_ends(n_cmp)
        smap = selection_map(n_cmp, n_sel)
        sel_blocks = s1.reshape((n_sel, SEL_BLOCK) + s1.shape[1:])

        def gather_sel(idx):
            gsel = sel_blocks[idx, :, :, hh]
            return gsel[..., 0, :], gsel[..., 1, :]

        w_pad = jnp.concatenate([jnp.zeros((WINDOW,) + w1.shape[1:], w1.dtype), w1], axis=0)

        def one_block(qb):
            start = qb * Q_BLOCK
            qpos = start + jnp.arange(Q_BLOCK, dtype=jnp.int32)
            qb1 = lax.dynamic_slice_in_dim(q1, start, Q_BLOCK, axis=0)
            qrb1 = lax.dynamic_slice_in_dim(qr1, start, Q_BLOCK, axis=0)
            gb1 = lax.dynamic_slice_in_dim(g1, start, Q_BLOCK, axis=0)
            kvw = lax.dynamic_slice_in_dim(w_pad, start, WINDOW + Q_BLOCK, axis=0)
            wpos = start - WINDOW + jnp.arange(WINDOW + Q_BLOCK, dtype=jnp.int32)
            return nsa_core(qb1, qrb1, qpos, kvc, c_end, smap, gather_sel, kvw, wpos, gb1)

        o = lax.map(one_block, jnp.arange(t // Q_BLOCK, dtype=jnp.int32))
        return o.reshape(t, NSA_HEADS, NSA_HEAD_DIM)

    o = lax.map(one_seq, (q, qr, kv_c, kv_s, kv_w, gates))
    return o, kv_w[:, t - min(WINDOW, t):]


def nsa_sample(cache_cmp, cache_sel, cache_win, page_table,
               q, qr, kv_c, kv_s, kv_w, gates, cmp_pe, cmp_w1, cmp_w2):
    s = q.shape[1]
    t_tot = PAST_LEN + s
    t_cmp = (t_tot // CMP_STRIDE) * CMP_STRIDE
    n_cmp = t_cmp // CMP_STRIDE - CMP_LEN // CMP_STRIDE + 1
    n_past_blk = PAST_LEN // SEL_BLOCK
    n_new_blk = -(-s // SEL_BLOCK)
    n_sel = n_past_blk + n_new_blk
    c_end = compressed_ends(n_cmp)
    smap = selection_map(n_cmp, n_sel)
    blk_per_page = PAGE_SIZE // SEL_BLOCK
    pool_blk = cache_sel.reshape((cache_sel.shape[0], blk_per_page, SEL_BLOCK) + cache_sel.shape[2:])
    hh = jnp.arange(NSA_KV_HEADS)[:, None, None]
    qpos = PAST_LEN + jnp.arange(s, dtype=jnp.int32)
    wbuf = cache_win.shape[1]
    wpos = PAST_LEN - wbuf + jnp.arange(wbuf + s, dtype=jnp.int32)
    w_keep = min(WINDOW, t_tot)

    def one_seq(args):
        q1, qr1, c1, s1, w1, g1, pt_row, wb = args
        past_c = cache_cmp[pt_row].reshape((PAST_LEN,) + c1.shape[1:])
        full_c = jnp.concatenate([past_c, c1.astype(past_c.dtype)], axis=0)[:t_cmp]
        kvc = compress_kv(full_c, cmp_pe, cmp_w1, cmp_w2)
        new_blocks = jnp.pad(s1, ((0, n_new_blk * SEL_BLOCK - s), (0, 0), (0, 0), (0, 0)))
        new_blocks = new_blocks.reshape((n_new_blk, SEL_BLOCK) + s1.shape[1:])

        def gather_sel(idx):
            jp = jnp.minimum(idx, n_past_blk - 1)
            phys = pt_row[jp // blk_per_page]
            gp = pool_blk[phys, jp % blk_per_page, :, :, hh]
            jn = jnp.clip(idx - n_past_blk, 0, n_new_blk - 1)
            gn = new_blocks[jn, :, :, hh]
            gsel = jnp.where((idx >= n_past_blk)[..., None, None, None], gn, gp)
            return gsel[..., 0, :], gsel[..., 1, :]

        kvw = jnp.concatenate([wb, w1.astype(wb.dtype)], axis=0)
        o = nsa_core(q1, qr1, qpos, kvc, c_end, smap, gather_sel, kvw, wpos, g1)
        return o, kvw[kvw.shape[0] - w_keep:]

    o, new_win = lax.map(one_seq, (q, qr, kv_c, kv_s, kv_w, gates, page_table, cache_win))
    return o, new_win


def decoder_layer(x, pos, conv_buf, ssm_state, ssd_fn, nsa_fn, lw):
    (w_in, conv_w, conv_b, dt_bias, a_log, d_skip, ssm_norm_w, w_ssm_out,
     cmp_pe, cmp_w1, cmp_w2, w_nsa_out, w_o, ln1_g, ln1_b,
     w_gate, w_up, w_down, ln2_g, ln2_b) = lw
    bsz, t, _ = x.shape
    proj = jnp.einsum('btd,dc->btc', x, w_in)
    z, xbc, dt_raw, q, kv_c, kv_s, kv_w, nsa_g, merge_g = jnp.split(proj, _in_split_points(), axis=-1)
    xbc, new_conv = causal_conv(xbc, conv_buf, conv_w, conv_b)
    xs, bm, cm = jnp.split(xbc, [SSM_D_INNER, SSM_D_INNER + SSM_GROUPS * SSM_D_STATE], axis=-1)
    xs = xs.reshape(bsz, t, SSM_HEADS, SSM_HEAD_DIM)
    bm = bm.reshape(bsz, t, SSM_GROUPS, SSM_D_STATE)
    cm = cm.reshape(bsz, t, SSM_GROUPS, SSM_D_STATE)
    dt = jax.nn.softplus(dt_raw.astype(jnp.float32) + dt_bias.astype(jnp.float32))
    a = -jnp.exp(a_log.astype(jnp.float32))
    y_ssm, new_ssm = ssd_fn(xs, dt, a, bm, cm, ssm_state.astype(jnp.float32))
    y_ssm = y_ssm + d_skip.astype(jnp.float32)[:, None] * xs.astype(jnp.float32)
    y_ssm = gated_rmsnorm(y_ssm.reshape(bsz, t, SSM_D_INNER), z, ssm_norm_w)
    branch_ssm = y_ssm @ w_ssm_out
    q = q.reshape(bsz, t, NSA_HEADS, NSA_HEAD_DIM)
    qr = rope_partial(q, pos)
    kv_c = kv_c.reshape(bsz, t, 2, NSA_KV_HEADS, NSA_HEAD_DIM)
    kv_s = kv_s.reshape(bsz, t, 2, NSA_KV_HEADS, NSA_HEAD_DIM)
    kv_s = jnp.stack([rope_partial(kv_s[:, :, 0], pos), kv_s[:, :, 1]], axis=2)
    kv_w = kv_w.reshape(bsz, t, 2, NSA_KV_HEADS, NSA_HEAD_DIM)
    kv_w = jnp.stack([rope_partial(kv_w[:, :, 0], pos), kv_w[:, :, 1]], axis=2)
    gates = jax.nn.sigmoid(nsa_g.reshape(bsz, t, NSA_HEADS, 3))
    o_nsa, new_win = nsa_fn(q, qr, kv_c, kv_s, kv_w, gates, cmp_pe, cmp_w1, cmp_w2)
    branch_nsa = o_nsa.reshape(bsz, t, NSA_HEADS * NSA_HEAD_DIM) @ w_nsa_out
    g_ssm, g_nsa = jnp.split(jax.nn.sigmoid(merge_g), 2, axis=-1)
    mixed = (g_ssm * branch_ssm + g_nsa * branch_nsa) @ w_o
    h = layer_norm(DEEPNORM_ALPHA * x + mixed, ln1_g, ln1_b)
    ffn = (jax.nn.silu(h @ w_gate) * (h @ w_up)) @ w_down
    y = layer_norm(DEEPNORM_ALPHA * h + ffn, ln2_g, ln2_b)
    return y, kv_c, kv_s, new_win, new_ssm.astype(x.dtype), new_conv


def setup_inputs(seed: int = 0) -> dict:
    key = jax.random.key(seed)
    ks = jax.random.split(key, 32)
    f32 = jnp.float32
    n_pages = PAST_LEN // PAGE_SIZE
    n_pool = (DEC_BATCH * n_pages * 5) // 4
    wbuf = min(WINDOW, PAST_LEN)

    def nrm(k, shape, scale):
        return jax.random.normal(k, shape, f32) * scale

    kvshape = (DEPTH, n_pool, PAGE_SIZE, 2, NSA_KV_HEADS, NSA_HEAD_DIM)
    page_table = jax.random.permutation(ks[7], n_pool)[:DEC_BATCH * n_pages]
    page_table = page_table.reshape(DEC_BATCH, n_pages).astype(jnp.int32)
    dt = jnp.exp(jax.random.uniform(ks[11], (DEPTH, SSM_HEADS), f32, math.log(DT_MIN), math.log(DT_MAX)))
    return {
        'x_prompt': nrm(ks[0], (BATCH, SEQ, D_MODEL), 1.0),
        'x_sample': nrm(ks[1], (DEC_BATCH, DEC_SEQ, D_MODEL), 1.0),
        'cache_cmp_kv': nrm(ks[2], kvshape, 1.0),
        'cache_sel_kv': nrm(ks[3], kvshape, 1.0),
        'cache_win_kv': nrm(ks[4], (DEPTH, DEC_BATCH, wbuf, 2, NSA_KV_HEADS, NSA_HEAD_DIM), 1.0),
        'state_ssm': nrm(ks[5], (DEPTH, DEC_BATCH, SSM_HEADS, SSM_HEAD_DIM, SSM_D_STATE), 0.5),
        'state_conv': nrm(ks[6], (DEPTH, DEC_BATCH, SSM_CONV_W - 1, SSM_CONV_DIM), 1.0),
        'page_table': page_table,
        'w_in': nrm(ks[8], (DEPTH, D_MODEL, IN_COLS), D_MODEL ** -0.5),
        'conv_w': nrm(ks[9], (DEPTH, SSM_CONV_W, SSM_CONV_DIM), SSM_CONV_W ** -0.5),
        'conv_b': nrm(ks[10], (DEPTH, SSM_CONV_DIM), 0.02),
        'dt_bias': dt + jnp.log(-jnp.expm1(-dt)),
        'a_log': jnp.log(jax.random.uniform(ks[12], (DEPTH, SSM_HEADS), f32, 1.0, 16.0)),
        'd_skip': 1.0 + nrm(ks[13], (DEPTH, SSM_HEADS), 0.1),
        'ssm_norm_w': 1.0 + nrm(ks[14], (DEPTH, SSM_D_INNER), 0.02),
        'w_ssm_out': nrm(ks[15], (DEPTH, SSM_D_INNER, D_MODEL), SSM_D_INNER ** -0.5),
        'cmp_pe': nrm(ks[16], (DEPTH, 2, CMP_LEN, NSA_HEAD_DIM), 0.1),
        'cmp_w1': nrm(ks[17], (DEPTH, 2, CMP_LEN, NSA_HEAD_DIM, NSA_HEAD_DIM), (CMP_LEN * NSA_HEAD_DIM) ** -0.5),
        'cmp_w2': nrm(ks[18], (DEPTH, 2, NSA_HEAD_DIM, NSA_HEAD_DIM), NSA_HEAD_DIM ** -0.5),
        'w_nsa_out': nrm(ks[19], (DEPTH, NSA_HEADS * NSA_HEAD_DIM, D_MODEL), (NSA_HEADS * NSA_HEAD_DIM) ** -0.5),
        'w_o': nrm(ks[20], (DEPTH, D_MODEL, D_MODEL), D_MODEL ** -0.5 * DEEPNORM_BETA),
        'ln1_g': 1.0 + nrm(ks[21], (DEPTH, D_MODEL), 0.02),
        'ln1_b': nrm(ks[22], (DEPTH, D_MODEL), 0.02),
        'w_gate': nrm(ks[23], (DEPTH, D_MODEL, FFN_HIDDEN), D_MODEL ** -0.5),
        'w_up': nrm(ks[24], (DEPTH, D_MODEL, FFN_HIDDEN), D_MODEL ** -0.5),
        'w_down': nrm(ks[25], (DEPTH, FFN_HIDDEN, D_MODEL), FFN_HIDDEN ** -0.5 * DEEPNORM_BETA),
        'ln2_g': 1.0 + nrm(ks[26], (DEPTH, D_MODEL), 0.02),
        'ln2_b': nrm(ks[27], (DEPTH, D_MODEL), 0.02),
    }


def reference(x_prompt, x_sample, cache_cmp_kv, cache_sel_kv, cache_win_kv, state_ssm, state_conv,
              page_table, w_in, conv_w, conv_b, dt_bias, a_log, d_skip, ssm_norm_w, w_ssm_out,
              cmp_pe, cmp_w1, cmp_w2, w_nsa_out, w_o, ln1_g, ln1_b, w_gate, w_up, w_down,
              ln2_g, ln2_b):
    bsz, t = x_prompt.shape[0], x_prompt.shape[1]
    pos_p = jnp.arange(t, dtype=jnp.int32)
    pos_s = PAST_LEN + jnp.arange(x_sample.shape[1], dtype=jnp.int32)
    y_p, y_s = x_prompt, x_sample
    p_cmp, p_sel, p_win, p_ssm, p_conv = [], [], [], [], []
    s_cmp, s_sel, s_win, s_ssm, s_conv = [], [], [], [], []
    for l in range(DEPTH):
        lw = (w_in[l], conv_w[l], conv_b[l], dt_bias[l], a_log[l], d_skip[l], ssm_norm_w[l],
              w_ssm_out[l], cmp_pe[l], cmp_w1[l], cmp_w2[l], w_nsa_out[l], w_o[l], ln1_g[l],
              ln1_b[l], w_gate[l], w_up[l], w_down[l], ln2_g[l], ln2_b[l])
        conv0 = jnp.zeros((bsz, SSM_CONV_W - 1, SSM_CONV_DIM), x_prompt.dtype)
        ssm0 = jnp.zeros((bsz, SSM_HEADS, SSM_HEAD_DIM, SSM_D_STATE), jnp.float32)
        y_p, kc, ksel, kw, st, cv = decoder_layer(y_p, pos_p, conv0, ssm0, ssd_chunked, nsa_prompt, lw)
        p_cmp.append(kc)
        p_sel.append(ksel)
        p_win.append(kw)
        p_ssm.append(st)
        p_conv.append(cv)
        nsa_s = functools.partial(nsa_sample, cache_cmp_kv[l], cache_sel_kv[l], cache_win_kv[l], page_table)
        y_s, kc, ksel, kw, st, cv = decoder_layer(y_s, pos_s, state_conv[l], state_ssm[l], ssd_recurrent, nsa_s, lw)
        s_cmp.append(kc)
        s_sel.append(ksel)
        s_win.append(kw)
        s_ssm.append(st)
        s_conv.append(cv)
    return (y_p, y_s, jnp.stack(p_cmp), jnp.stack(p_sel), jnp.stack(p_win), jnp.stack(p_ssm),
            jnp.stack(p_conv), jnp.stack(s_cmp), jnp.stack(s_sel), jnp.stack(s_win),
            jnp.stack(s_ssm), jnp.stack(s_conv))
```

```python
import functools
import math

import numpy as np
import jax
import jax.numpy as jnp
from jax import lax
from jax.experimental import pallas as pl
from jax.experimental.pallas import tpu as pltpu

F32 = jnp.float32
BF16 = jnp.bfloat16

D_MODEL = 1024
SSM_D_INNER = 2048
SSM_HEAD_DIM = 64
SSM_HEADS = 32
SSM_GROUPS = 4
SSM_D_STATE = 128
SSM_CONV_W = 4
SSM_CONV_DIM = 3072
SSM_CHUNK = 128
NSA_HEADS = 16
NSA_KV_HEADS = 4
NSA_HEAD_DIM = 64
NSA_GQA = 4
KV_COLS = 512
KV_HALF = 256
CMP_LEN = 32
CMP_STRIDE = 16
SEL_BLOCK = 64
SEL_TOP_N = 16
WINDOW = 512
Q_BLOCK = 128
ROPE_DIM = 16
ROPE_THETA = 500000.0
FFN_HIDDEN = 2816
NORM_EPS = 1e-5
BIG = 1e30
NEG = -1e30
DEPTH = 1
DEEPNORM_ALPHA = (2 * DEPTH) ** 0.25
PAGE_SIZE = 128

LANES = 128
VMEM_LIMIT_BYTES = 56 * 1024 * 1024

COL_XBC = 0
COL_Q = 3072
COL_Z = 4096
COL_MG = 6144
COL_KVC = 8192
COL_KVS = 8704
COL_KVW = 9216
COL_SMALL = 9728
PROJ_COLS = 9856
PROJ_TN = 896

KEY_SLAB = 512
CMP_CHUNK = 2048
FFN_CHUNK = 256


def _dot_dims(a, b, dims):
    return lax.dot_general(a, b, (dims, ((), ())), preferred_element_type=F32)


def _dot(a, b):
    return _dot_dims(a, b, ((1,), (0,)))


def _dot_nt(a, b):
    return _dot_dims(a, b, ((1,), (1,)))


def _dot_tn(a, b):
    return _dot_dims(a, b, ((0,), (0,)))


def _bf(x):
    return x.astype(BF16)


def _split3(x):
    hi = _bf(x)
    r1 = x - hi.astype(F32)
    mid = _bf(r1)
    lo = _bf(r1 - mid.astype(F32))
    return hi, mid, lo


def _dot_exact_lhs(x, w_bf16):
    hi, mid, lo = _split3(x)
    return _dot(hi, w_bf16) + _dot(mid, w_bf16) + _dot(lo, w_bf16)


def _dot_exact_rhs(w_bf16, x):
    hi, mid, lo = _split3(x)
    return _dot(w_bf16, hi) + _dot(w_bf16, mid) + _dot(w_bf16, lo)


def _silu(x):
    return x * jax.nn.sigmoid(x)


def _softplus(x):
    return jnp.maximum(x, 0.0) + jnp.log1p(jnp.exp(-jnp.abs(x)))


def _gelu_tanh(x):
    return 0.5 * x * (1.0 + jnp.tanh(0.7978845608028654 * (x + 0.044715 * (x * x * x))))


def _layer_norm(x, g, b):
    mu = jnp.mean(x, axis=-1, keepdims=True)
    xc = x - mu
    var = jnp.mean(xc * xc, axis=-1, keepdims=True)
    return xc * lax.rsqrt(var + NORM_EPS) * g + b


def _cparams(sem):
    return pltpu.CompilerParams(dimension_semantics=sem, vmem_limit_bytes=VMEM_LIMIT_BYTES)


def _mm_kernel(x_ref, w_ref, o_ref):
    o_ref[...] = _dot(_bf(x_ref[...]), w_ref[...])


def _matmul(x, w, tm, tn, name):
    m, k = x.shape
    n = w.shape[1]
    return pl.pallas_call(
        _mm_kernel,
        grid=(m // tm, n // tn),
        in_specs=[pl.BlockSpec((tm, k), lambda i, j: (i, 0)),
                  pl.BlockSpec((k, tn), lambda i, j: (0, j))],
        out_specs=pl.BlockSpec((tm, tn), lambda i, j: (i, j)),
        out_shape=jax.ShapeDtypeStruct((m, n), F32),
        compiler_params=_cparams(("parallel", "arbitrary")),
        name=name,
    )(x, w)


def _ssd_kernel(xbc_ref, z_ref, sm_ref, cw_ref, cb_ref, dtb_ref, alog_ref, drow_ref, nw_ref, r3_ref,
                y_ref, st_ref, cv_ref, xp_s, stT_s):
    j = pl.program_id(1)
    nj = pl.num_programs(1)
    q = SSM_CHUNK

    @pl.when(j == 0)
    def _():
        xp_s[0:8, :] = jnp.zeros((8, SSM_CONV_DIM), F32)
        stT_s[...] = jnp.zeros_like(stT_s)

    xp_s[8:8 + q, :] = xbc_ref[...]
    acc = cb_ref[...]
    for k in range(SSM_CONV_W):
        acc = acc + cw_ref[k:k + 1, :] * xp_s[5 + k:5 + k + q, :]
    act = _silu(acc)
    tail = xp_s[q + 5:q + 8, :]
    xp_s[5:8, :] = tail

    @pl.when(j == nj - 1)
    def _():
        cv_ref[0] = tail

    lane = lax.broadcasted_iota(jnp.int32, (1, LANES), 1)
    a_full = jnp.where(lane < SSM_HEADS, -jnp.exp(alog_ref[...]), 0.0)
    dt = _softplus(sm_ref[...] + dtb_ref[...])
    da = dt * a_full
    row_i = lax.broadcasted_iota(jnp.int32, (q, q), 0)
    col_i = lax.broadcasted_iota(jnp.int32, (q, q), 1)
    causal = col_i <= row_i
    tril = jnp.where(causal, 1.0, 0.0).astype(BF16)
    cum = _dot_exact_rhs(tril, da)
    cumT = cum.T
    dtT = dt.T
    hi, mid, lo = _split3(cum)
    packed = _bf(hi.astype(F32) + pltpu.roll(mid.astype(F32), 32, 1) + pltpu.roll(lo.astype(F32), 64, 1))
    lane_lo = lax.broadcasted_iota(jnp.int32, (1, LANES), 1) < SSM_HEAD_DIM

    heads_per_group = SSM_HEADS // SSM_GROUPS
    y_parts = []
    for g in range(SSM_GROUPS):
        bm_g = act[:, SSM_D_INNER + g * SSM_D_STATE:SSM_D_INNER + (g + 1) * SSM_D_STATE]
        cm_g = act[:, SSM_D_INNER + SSM_GROUPS * SSM_D_STATE + g * SSM_D_STATE:
                   SSM_D_INNER + SSM_GROUPS * SSM_D_STATE + (g + 1) * SSM_D_STATE]
        cmb = _bf(cm_g)
        cb = _dot_nt(cmb, _bf(bm_g))
        bT = bm_g.T
        for pp in range(heads_per_group // 2):
            pair = g * (heads_per_group // 2) + pp
            h0 = 2 * pair
            xs_pair = act[:, pair * LANES:(pair + 1) * LANES]
            xs_a = _bf(jnp.where(lane_lo, xs_pair, 0.0))
            xs_b = _bf(jnp.where(lane_lo, 0.0, xs_pair))
            cols2 = _dot(packed, r3_ref[:, pair * 2 * LANES:(pair + 1) * 2 * LANES])
            y_pair = None
            ds_pair = None
            lasts = []
            cols = []
            for hh, xs_m in ((0, xs_a), (1, xs_b)):
                h = h0 + hh
                col = cols2[:, hh * LANES:(hh + 1) * LANES]
                row = cumT[h:h + 1, :]
                dtrow = dtT[h:h + 1, :]
                dec = jnp.exp(jnp.where(causal, col - row, NEG))
                m_h = _bf(cb * dec * dtrow)
                y_h = _dot(m_h, xs_m)
                last = col[q - 1:q, :]
                wrow = jnp.exp(last - row) * dtrow
                ds_h = _dot(_bf(bT * wrow), xs_m)
                y_pair = y_h if y_pair is None else y_pair + y_h
                ds_pair = ds_h if ds_pair is None else ds_pair + ds_h
                lasts.append(last)
                cols.append(col)
            st_pair = stT_s[:, pair * LANES:(pair + 1) * LANES]
            scale_t = jnp.exp(jnp.where(lane_lo, cols[0], cols[1]))
            y_pair = y_pair + _dot(cmb, _bf(st_pair)) * scale_t
            stT_s[:, pair * LANES:(pair + 1) * LANES] = (
                st_pair * jnp.exp(jnp.where(lane_lo, lasts[0], lasts[1])) + ds_pair)
            y_pair = y_pair + drow_ref[:, pair * LANES:(pair + 1) * LANES] * xs_pair
            y_parts.append(y_pair)
    y = jnp.concatenate(y_parts, axis=1)
    v = y * _silu(z_ref[...])
    gw = SSM_D_INNER // SSM_GROUPS
    outs = []
    for g in range(SSM_GROUPS):
        vg = v[:, g * gw:(g + 1) * gw]
        ms = jnp.sum(vg * vg, axis=-1, keepdims=True) * (1.0 / gw)
        outs.append(vg * lax.rsqrt(ms + NORM_EPS) * nw_ref[:, g * gw:(g + 1) * gw])
    y_ref[...] = _bf(jnp.concatenate(outs, axis=1))

    @pl.when(j == nj - 1)
    def _():
        st_ref[0] = stT_s[...].T.reshape(SSM_HEADS, SSM_HEAD_DIM, SSM_D_STATE)


def _r3_table():
    k = np.arange(LANES)[:, None]
    c = np.arange(SSM_HEADS * LANES)[None, :]
    return jnp.asarray(((k % SSM_HEADS) == (c // LANES)) & (k < 3 * SSM_HEADS), dtype=BF16)


def _ssd_prompt(proj, bsz, t, cw, cb, dtb_pad, alog_pad, drow, nw):
    nch = t // SSM_CHUNK
    q = SSM_CHUNK
    row = lambda b, j: b * nch + j
    const = lambda shape: pl.BlockSpec(shape, lambda b, j: (0, 0))
    return pl.pallas_call(
        _ssd_kernel,
        grid=(bsz, nch),
        in_specs=[
            pl.BlockSpec((q, SSM_CONV_DIM), lambda b, j: (row(b, j), COL_XBC // SSM_CONV_DIM)),
            pl.BlockSpec((q, SSM_D_INNER), lambda b, j: (row(b, j), COL_Z // SSM_D_INNER)),
            pl.BlockSpec((q, LANES), lambda b, j: (row(b, j), COL_SMALL // LANES)),
            const((SSM_CONV_W, SSM_CONV_DIM)), const((1, SSM_CONV_DIM)),
            const((1, LANES)), const((1, LANES)), const((1, SSM_D_INNER)), const((1, SSM_D_INNER)),
            const((LANES, SSM_HEADS * LANES)),
        ],
        out_specs=[
            pl.BlockSpec((q, SSM_D_INNER), lambda b, j: (row(b, j), 0)),
            pl.BlockSpec((1, SSM_HEADS, SSM_HEAD_DIM, SSM_D_STATE), lambda b, j: (b, 0, 0, 0)),
            pl.BlockSpec((1, SSM_CONV_W - 1, SSM_CONV_DIM), lambda b, j: (b, 0, 0)),
        ],
        out_shape=[
            jax.ShapeDtypeStruct((bsz * t, SSM_D_INNER), BF16),
            jax.ShapeDtypeStruct((bsz, SSM_HEADS, SSM_HEAD_DIM, SSM_D_STATE), F32),
            jax.ShapeDtypeStruct((bsz, SSM_CONV_W - 1, SSM_CONV_DIM), F32),
        ],
        scratch_shapes=[pltpu.VMEM((q + 8, SSM_CONV_DIM), F32),
                        pltpu.VMEM((SSM_D_STATE, SSM_D_INNER), F32)],
        compiler_params=_cparams(("parallel", "arbitrary")),
        name="ssd_prompt",
    )(proj, proj, proj, cw, cb, dtb_pad, alog_pad, drow, nw, _r3_table())


def _rope_t(x_t, nh, c, s):
    n = x_t.shape[1]
    x3 = x_t.reshape(nh, NSA_HEAD_DIM, n)
    half = ROPE_DIM // 2
    x1 = x3[:, 0:half, :]
    x2 = x3[:, half:ROPE_DIM, :]
    r1 = x1 * c - x2 * s
    r2 = x2 * c + x1 * s
    return jnp.concatenate([r1, r2, x3[:, ROPE_DIM:, :]], axis=1).reshape(nh * NSA_HEAD_DIM, n)


def _prep_kernel(q_ref, kvc_ref, kvs_ref, kvw_ref, sm_ref, cos_ref, sin_ref,
                 qT_ref, qrT_ref, ks_ref, vsT_ref, kw_ref, vwT_ref, gT_ref, ncmp_ref, nsel_ref, nwin_ref):
    c = cos_ref[...]
    s = sin_ref[...]
    scale = NSA_HEAD_DIM ** -0.5
    q_t = q_ref[...].T
    qT_ref[0] = _bf(q_t * scale)
    qrT_ref[0] = _bf(_rope_t(q_t, NSA_HEADS, c, s) * scale)
    ncmp_ref[...] = kvc_ref[...]
    for src, k_out, vt_out, full_out in ((kvs_ref, ks_ref, vsT_ref, nsel_ref), (kvw_ref, kw_ref, vwT_ref, nwin_ref)):
        kv = src[...]
        k_rot = _rope_t(kv[:, :KV_HALF].T, NSA_KV_HEADS, c, s).T
        v = kv[:, KV_HALF:]
        full_out[:, :KV_HALF] = k_rot
        full_out[:, KV_HALF:] = v
        k_out[0, 0] = _bf(k_rot)
        vt_out[0, 0] = _bf(v.T)
    g_t = jax.nn.sigmoid(sm_ref[...]).T
    gT_ref[0] = g_t[SSM_HEADS:SSM_HEADS + 3 * NSA_HEADS, :]


def _attn_prep(proj, bsz, t, tt, cos_t, sin_t):
    nt = t // tt
    row = lambda b, j: b * nt + j
    n_gate = 3 * NSA_HEADS
    return pl.pallas_call(
        _prep_kernel,
        grid=(bsz, nt),
        in_specs=[
            pl.BlockSpec((tt, D_MODEL), lambda b, j: (row(b, j), COL_Q // D_MODEL)),
            pl.BlockSpec((tt, KV_COLS), lambda b, j: (row(b, j), COL_KVC // KV_COLS)),
            pl.BlockSpec((tt, KV_COLS), lambda b, j: (row(b, j), COL_KVS // KV_COLS)),
            pl.BlockSpec((tt, KV_COLS), lambda b, j: (row(b, j), COL_KVW // KV_COLS)),
            pl.BlockSpec((tt, LANES), lambda b, j: (row(b, j), COL_SMALL // LANES)),
            pl.BlockSpec((ROPE_DIM // 2, tt), lambda b, j: (0, j)),
            pl.BlockSpec((ROPE_DIM // 2, tt), lambda b, j: (0, j)),
        ],
        out_specs=[
            pl.BlockSpec((1, D_MODEL, tt), lambda b, j: (b, 0, j)),
            pl.BlockSpec((1, D_MODEL, tt), lambda b, j: (b, 0, j)),
            pl.BlockSpec((1, 1, tt, KV_HALF), lambda b, j: (b, j, 0, 0)),
            pl.BlockSpec((1, 1, KV_HALF, tt), lambda b, j: (b, j, 0, 0)),
            pl.BlockSpec((1, 1, tt, KV_HALF), lambda b, j: (b, j, 0, 0)),
            pl.BlockSpec((1, 1, KV_HALF, tt), lambda b, j: (b, j, 0, 0)),
            pl.BlockSpec((1, n_gate, tt), lambda b, j: (b, 0, j)),
            pl.BlockSpec((tt, KV_COLS), lambda b, j: (row(b, j), 0)),
            pl.BlockSpec((tt, KV_COLS), lambda b, j: (row(b, j), 0)),
            pl.BlockSpec((tt, KV_COLS), lambda b, j: (row(b, j), 0)),
        ],
        out_shape=[
            jax.ShapeDtypeStruct((bsz, D_MODEL, t), BF16),
            jax.ShapeDtypeStruct((bsz, D_MODEL, t), BF16),
            jax.ShapeDtypeStruct((bsz, nt, tt, KV_HALF), BF16),
            jax.ShapeDtypeStruct((bsz, nt, KV_HALF, tt), BF16),
            jax.ShapeDtypeStruct((bsz, nt, tt, KV_HALF), BF16),
            jax.ShapeDtypeStruct((bsz, nt, KV_HALF, tt), BF16),
            jax.ShapeDtypeStruct((bsz, n_gate, t), F32),
            jax.ShapeDtypeStruct((bsz * t, KV_COLS), F32),
            jax.ShapeDtypeStruct((bsz * t, KV_COLS), F32),
            jax.ShapeDtypeStruct((bsz * t, KV_COLS), F32),
        ],
        compiler_params=_cparams(("parallel", "parallel")),
        name="attn_prep",
    )(proj, proj, proj, proj, proj, cos_t, sin_t)


def _compress_partial(data_refs, wcat_ref, xcat_s):
    ns = data_refs[0].shape[0] // CMP_STRIDE
    tiles_per_half = KV_HALF // LANES
    outs = []
    for lp in range(CMP_STRIDE):
        for c, ref in enumerate(data_refs):
            rows = ref[pl.ds(lp, ns, stride=CMP_STRIDE), :]
            v, cc = divmod(c, tiles_per_half)
            xcat_s[v, :, lp * KV_HALF + cc * LANES:lp * KV_HALF + (cc + 1) * LANES] = _bf(rows)
    for v in range(2):
        outs.append(_dot(xcat_s[v], wcat_ref[v]))
    return outs


def _compress_finish(pab_s, bias, w2bd_ref):
    n_sub = pab_s.shape[1]
    res = []
    for v in range(2):
        pab = pab_s[v]
        pre = pab[:, :KV_HALF] + pltpu.roll(pab[:, KV_HALF:], n_sub - 1, 0) + bias[v:v + 1, :]
        res.append(_dot(_bf(_gelu_tanh(pre)), w2bd_ref[v]))
    return res


def _pe_bias(pe_ref, w1f_ref):
    rows = []
    for v in range(2):
        rows.append(jnp.dot(pe_ref[v], w1f_ref[v], preferred_element_type=F32,
                            precision=lax.Precision.HIGHEST)[0:1, :])
    return jnp.concatenate(rows + [jnp.zeros((6, KV_HALF), F32)], axis=0)


def _compress_prompt_kernel(d0_ref, d1_ref, d2_ref, d3_ref, wcat_ref, w2bd_ref, pe_ref, w1f_ref,
                            kc_ref, vcT_ref, xcat_s, pab_s):
    j = pl.program_id(1)
    nj = pl.num_programs(1)
    ns = CMP_CHUNK // CMP_STRIDE
    parts = _compress_partial((d0_ref, d1_ref, d2_ref, d3_ref), wcat_ref, xcat_s)
    for v in range(2):
        pab_s[v, pl.ds(pl.multiple_of(j * ns, ns), ns), :] = parts[v]

    @pl.when(j == nj - 1)
    def _():
        kc, vc = _compress_finish(pab_s, _pe_bias(pe_ref, w1f_ref), w2bd_ref)
        kc_ref[0] = _bf(kc)
        vcT_ref[0] = _bf(vc.T)


def _compress_prompt(proj, bsz, t, wcat, w2bd, pe8, w1f4):
    nj = t // CMP_CHUNK
    n_sub = t // CMP_STRIDE
    ns = CMP_CHUNK // CMP_STRIDE
    c3 = lambda shape: pl.BlockSpec(shape, lambda b, j: (0, 0, 0))
    return pl.pallas_call(
        _compress_prompt_kernel,
        grid=(bsz, nj),
        in_specs=[pl.BlockSpec((CMP_CHUNK, LANES),
                               functools.partial(lambda b, j, c: (b * nj + j, COL_KVC // LANES + c), c=c))
                  for c in range(KV_COLS // LANES)]
        + [c3(wcat.shape), c3(w2bd.shape), c3(pe8.shape), c3(w1f4.shape)],
        out_specs=[pl.BlockSpec((1, n_sub, KV_HALF), lambda b, j: (b, 0, 0)),
                   pl.BlockSpec((1, KV_HALF, n_sub), lambda b, j: (b, 0, 0))],
        out_shape=[jax.ShapeDtypeStruct((bsz, n_sub, KV_HALF), BF16),
                   jax.ShapeDtypeStruct((bsz, KV_HALF, n_sub), BF16)],
        scratch_shapes=[pltpu.VMEM((2, ns, CMP_STRIDE * KV_HALF), BF16),
                        pltpu.VMEM((2, n_sub, KV_COLS), F32)],
        compiler_params=_cparams(("parallel", "arbitrary")),
        name="compress_prompt",
    )(proj, proj, proj, proj, wcat, w2bd, pe8, w1f4)


def _topk_mask(score, jrow, n_pick):
    sel = jnp.zeros_like(score)
    for _ in range(n_pick):
        mx = jnp.max(score, axis=0, keepdims=True)
        idx = jnp.min(jnp.where(score == mx, jrow, 1e9), axis=0, keepdims=True)
        chosen = jrow == idx
        sel = jnp.where(chosen, 1.0, sel)
        score = jnp.where(chosen, -jnp.inf, score)
    return sel


def _online_step(st, vt_aug, m, acc):
    mn = jnp.maximum(m, jnp.max(st, axis=0, keepdims=True))
    p = jnp.exp(st - mn)
    acc = acc * jnp.exp(m - mn) + _dot(vt_aug, _bf(p))
    return mn, acc


def _expand_block_rows(rows8, n):
    return jnp.concatenate([jnp.broadcast_to(rows8[r:r + 1, :], (SEL_BLOCK, n)) for r in range(8)], axis=0)


def _nsa_prompt_kernel(qT_ref, qrT_ref, gT_ref, kc_ref, vcT_ref, ks_ref, vsT_ref, kw_ref, vwT_ref, smapT_ref, wn_ref,
                       o_ref, sel_s):
    qb = pl.program_id(1)
    nq = NSA_GQA * Q_BLOCK
    n_cmp_rows = kc_ref.shape[1]
    n_selblk = smapT_ref.shape[0]
    lane_q = lax.broadcasted_iota(jnp.int32, (1, nq), 1) % Q_BLOCK
    qpos = qb * Q_BLOCK + lane_q
    ones16 = jnp.ones((16, KEY_SLAB), BF16)
    zero_q = jnp.zeros((NSA_HEAD_DIM, nq), BF16)
    n_full = (qb * Q_BLOCK) // KEY_SLAB
    key_row = lax.broadcasted_iota(jnp.int32, (KEY_SLAB, 1), 0)

    jrow = lax.broadcasted_iota(jnp.int32, (n_selblk, Q_BLOCK), 0).astype(F32)
    cur = (qb * (Q_BLOCK // SEL_BLOCK)
           + lax.broadcasted_iota(jnp.int32, (1, Q_BLOCK), 1) // SEL_BLOCK).astype(F32)
    future = jrow > cur
    forced = jnp.where(jrow == 0.0, 1.0, 0.0) + jnp.where(jrow == cur, 1.0, 0.0) + jnp.where(jrow == cur - 1.0, 1.0, 0.0)
    crow = lax.broadcasted_iota(jnp.int32, (n_cmp_rows, 1), 0)
    cmp_valid = (crow * CMP_STRIDE + (CMP_LEN - 1)) <= qpos

    ot_parts = []
    for h in range(NSA_KV_HEADS):
        def q_cat(ref):
            x = jnp.concatenate([ref[0, (h * NSA_GQA + g) * NSA_HEAD_DIM:(h * NSA_GQA + g + 1) * NSA_HEAD_DIM, :]
                                 for g in range(NSA_GQA)], axis=1)
            return jnp.concatenate([x if hh == h else zero_q for hh in range(NSA_KV_HEADS)], axis=0)

        qp = q_cat(qT_ref)
        qrp = q_cat(qrT_ref)

        sc = jnp.where(cmp_valid, _dot(kc_ref[0], qp), NEG)
        m_c = jnp.max(sc, axis=0, keepdims=True)
        e_c = jnp.where(cmp_valid, jnp.exp(sc - m_c), 0.0)
        p_c = e_c / jnp.maximum(jnp.sum(e_c, axis=0, keepdims=True), 1e-30)
        oc_t = _dot(vcT_ref[0, h * NSA_HEAD_DIM:(h + 1) * NSA_HEAD_DIM, :], _bf(p_c))
        psum = (p_c[:, 0:Q_BLOCK] + p_c[:, Q_BLOCK:2 * Q_BLOCK]
                + p_c[:, 2 * Q_BLOCK:3 * Q_BLOCK] + p_c[:, 3 * Q_BLOCK:4 * Q_BLOCK])
        p_hi = _bf(psum)
        p_lo = _bf(psum - p_hi.astype(F32))
        imp_t = _dot(smapT_ref[...], p_hi) + _dot(smapT_ref[...], p_lo)

        score = jnp.where(future, -BIG, jnp.where(forced > 0.5, BIG, imp_t))
        sel = _topk_mask(score, jrow, min(SEL_TOP_N, n_selblk))
        sel_s[...] = jnp.concatenate([sel] * NSA_GQA, axis=1)

        def vt_aug(ref, s):
            return jnp.concatenate([ref[0, s, h * NSA_HEAD_DIM:(h + 1) * NSA_HEAD_DIM, :], ones16], axis=0)

        def sel_mask(s):
            rows8 = sel_s[pl.ds(pl.multiple_of(s * 8, 8), 8), :]
            return _expand_block_rows(rows8, nq) > 0.5

        def sel_step(s, carry):
            m, acc = carry
            st = jnp.where(sel_mask(s), _dot(ks_ref[0, s], qrp), NEG)
            return _online_step(st, vt_aug(vsT_ref, s), m, acc)

        m0 = jnp.full((1, nq), NEG, F32)
        acc0 = jnp.zeros((NSA_HEAD_DIM + 16, nq), F32)
        m_s, acc_s = lax.fori_loop(0, n_full, sel_step, (m0, acc0))
        kpos = n_full * KEY_SLAB + key_row
        causal = kpos <= qpos
        st = jnp.where(causal, jnp.where(sel_mask(n_full), _dot(ks_ref[0, n_full], qrp), NEG), NEG)
        m_s, acc_s = _online_step(st, vt_aug(vsT_ref, n_full), m_s, acc_s)
        os_t = acc_s[0:NSA_HEAD_DIM, :] / acc_s[NSA_HEAD_DIM:NSA_HEAD_DIM + 1, :]

        m_w, acc_w = m0, acc0
        for off in (1, 0):
            s_true = n_full - off
            s_ld = jnp.maximum(s_true, 0)
            wpos = s_true * KEY_SLAB + key_row
            dp = qpos - wpos
            ok = jnp.where(dp >= 0, 1.0, 0.0) * jnp.where(dp <= WINDOW, 1.0, 0.0) * jnp.where(wpos >= 0, 1.0, 0.0)
            st = jnp.where(ok > 0.5, _dot(kw_ref[0, s_ld], qrp), NEG)
            m_w, acc_w = _online_step(st, vt_aug(vwT_ref, s_ld), m_w, acc_w)
        ow_t = acc_w[0:NSA_HEAD_DIM, :] / acc_w[NSA_HEAD_DIM:NSA_HEAD_DIM + 1, :]

        def gate_row(br):
            return jnp.concatenate([gT_ref[0, (h * NSA_GQA + g) * 3 + br:(h * NSA_GQA + g) * 3 + br + 1, :]
                                    for g in range(NSA_GQA)], axis=1)

        o_h = gate_row(0) * oc_t + gate_row(1) * os_t + gate_row(2) * ow_t
        for g in range(NSA_GQA):
            ot_parts.append(o_h[:, g * Q_BLOCK:(g + 1) * Q_BLOCK])
    o_t = _bf(jnp.concatenate(ot_parts, axis=0))
    o_ref[...] = _dot_tn(o_t, wn_ref[...])


def _selection_map_t(n_sel, n_cmp_rows):
    i = np.arange(n_cmp_rows)[None, :]
    j = np.arange(n_sel)[:, None]
    ov = (i * CMP_STRIDE < (j + 1) * SEL_BLOCK) & (i * CMP_STRIDE + CMP_LEN > j * SEL_BLOCK)
    return ov


def _nsa_prompt(qT, qrT, gT, kc, vcT, ks, vsT, kw, vwT, wn, bsz, t):
    nqb = t // Q_BLOCK
    n_sub = t // CMP_STRIDE
    n_cmp = n_sub - CMP_LEN // CMP_STRIDE + 1
    n_sel = t // SEL_BLOCK
    smap = _selection_map_t(n_sel, n_sub) & (np.arange(n_sub)[None, :] < n_cmp)
    smap_t = jnp.asarray(smap, dtype=BF16)
    nsl = t // KEY_SLAB
    per_b3 = lambda shape: pl.BlockSpec(shape, lambda b, j: (b, 0, 0))
    per_b4 = lambda shape: pl.BlockSpec(shape, lambda b, j: (b, 0, 0, 0))
    return pl.pallas_call(
        _nsa_prompt_kernel,
        grid=(bsz, nqb),
        in_specs=[
            pl.BlockSpec((1, D_MODEL, Q_BLOCK), lambda b, j: (b, 0, j)),
            pl.BlockSpec((1, D_MODEL, Q_BLOCK), lambda b, j: (b, 0, j)),
            pl.BlockSpec((1, 3 * NSA_HEADS, Q_BLOCK), lambda b, j: (b, 0, j)),
            per_b3((1, n_sub, KV_HALF)), per_b3((1, KV_HALF, n_sub)),
            per_b4((1, nsl, KEY_SLAB, KV_HALF)), per_b4((1, nsl, KV_HALF, KEY_SLAB)),
            per_b4((1, nsl, KEY_SLAB, KV_HALF)), per_b4((1, nsl, KV_HALF, KEY_SLAB)),
            pl.BlockSpec((n_sel, n_sub), lambda b, j: (0, 0)),
            pl.BlockSpec((D_MODEL, D_MODEL), lambda b, j: (0, 0)),
        ],
        out_specs=pl.BlockSpec((Q_BLOCK, D_MODEL), lambda b, j: (b * nqb + j, 0)),
        out_shape=jax.ShapeDtypeStruct((bsz * t, D_MODEL), F32),
        scratch_shapes=[pltpu.VMEM((n_sel, NSA_GQA * Q_BLOCK), F32)],
        compiler_params=_cparams(("parallel", "arbitrary")),
        name="nsa_prompt",
    )(qT, qrT, gT, kc, vcT, ks, vsT, kw, vwT, smap_t, wn)


def _merge_kernel(x_ref, mg_ref, bs_ref, bn_ref, wo_ref, g_ref, b_ref, h_ref):
    mg = mg_ref[...]
    mix = jax.nn.sigmoid(mg[:, :D_MODEL]) * bs_ref[...] + jax.nn.sigmoid(mg[:, D_MODEL:]) * bn_ref[...]
    pre = DEEPNORM_ALPHA * x_ref[...] + _dot(_bf(mix), wo_ref[...])
    h_ref[...] = _layer_norm(pre, g_ref[...], b_ref[...])


def _merge(x, proj, b_ssm, b_nsa, wo, g, b, tm):
    m = x.shape[0]
    rowblk = lambda shape, c=0: pl.BlockSpec(shape, lambda i, c=c: (i, c))
    const = lambda shape: pl.BlockSpec(shape, lambda i: (0, 0))
    return pl.pallas_call(
        _merge_kernel,
        grid=(m // tm,),
        in_specs=[rowblk((tm, D_MODEL)), rowblk((tm, 2 * D_MODEL), COL_MG // (2 * D_MODEL)),
                  rowblk((tm, D_MODEL)), rowblk((tm, D_MODEL)),
                  const((D_MODEL, D_MODEL)), const((1, D_MODEL)), const((1, D_MODEL))],
        out_specs=rowblk((tm, D_MODEL)),
        out_shape=jax.ShapeDtypeStruct((m, D_MODEL), F32),
        compiler_params=_cparams(("parallel",)),
        name="merge_ln1",
    )(x, proj, b_ssm, b_nsa, wo, g, b)


def _ffn_kernel(h_ref, wg_ref, wu_ref, wd_ref, g_ref, b_ref, y_ref, acc_s):
    h = h_ref[...]
    hb = _bf(h)
    acc_s[...] = jnp.zeros_like(acc_s)

    def body(c, carry):
        gate = _dot(hb, wg_ref[c])
        up = _dot(hb, wu_ref[c])
        acc_s[...] += _dot(_bf(_silu(gate) * up), wd_ref[c])
        return carry

    lax.fori_loop(0, wg_ref.shape[0], body, 0)
    y_ref[...] = _layer_norm(DEEPNORM_ALPHA * h + acc_s[...], g_ref[...], b_ref[...])


def _ffn(h, wg3, wu3, wd3, g, b, tm):
    m = h.shape[0]
    nc = wg3.shape[0]
    const2 = lambda shape: pl.BlockSpec(shape, lambda i: (0, 0))
    const3 = lambda shape: pl.BlockSpec(shape, lambda i: (0, 0, 0))
    return pl.pallas_call(
        _ffn_kernel,
        grid=(m // tm,),
        in_specs=[pl.BlockSpec((tm, D_MODEL), lambda i: (i, 0)),
                  const3((nc, D_MODEL, FFN_CHUNK)), const3((nc, D_MODEL, FFN_CHUNK)), const3((nc, FFN_CHUNK, D_MODEL)),
                  const2((1, D_MODEL)), const2((1, D_MODEL))],
        out_specs=pl.BlockSpec((tm, D_MODEL), lambda i: (i, 0)),
        out_shape=jax.ShapeDtypeStruct((m, D_MODEL), F32),
        scratch_shapes=[pltpu.VMEM((tm, D_MODEL), F32)],
        compiler_params=_cparams(("parallel",)),
        name="ffn_ln2",
    )(h, wg3, wu3, wd3, g, b)


def _ssd_s_pre_kernel(xbc_ref, cst_ref, sm_ref, cw_ref, cb_ref, dtb_ref, alog_ref, e_ref,
                      xs_ref, xdt_ref, bm_ref, cm_ref, dec_ref, ncv_ref):
    cd = SSM_CONV_DIM
    xbc = xbc_ref[...]
    acc = cb_ref[...] + cw_ref[SSM_CONV_W - 1:SSM_CONV_W, :] * xbc
    for k in range(SSM_CONV_W - 1):
        acc = acc + cw_ref[k:k + 1, :] * cst_ref[:, k * cd:(k + 1) * cd]
    act = _silu(acc)
    for k in range(SSM_CONV_W - 2):
        ncv_ref[:, k * cd:(k + 1) * cd] = cst_ref[:, (k + 1) * cd:(k + 2) * cd]
    ncv_ref[:, (SSM_CONV_W - 2) * cd:] = xbc
    lane = lax.broadcasted_iota(jnp.int32, (1, LANES), 1)
    a_full = jnp.where(lane < SSM_HEADS, -jnp.exp(alog_ref[...]), 0.0)
    dt = _softplus(sm_ref[...] + dtb_ref[...])
    dec_ref[...] = jnp.exp(dt * a_full)
    xs = act[:, :SSM_D_INNER]
    xs_ref[...] = xs
    xdt_ref[...] = xs * _dot_exact_lhs(dt, e_ref[...])
    bm_ref[...] = act[:, SSM_D_INNER:SSM_D_INNER + SSM_GROUPS * SSM_D_STATE]
    cm_ref[...] = act[:, SSM_D_INNER + SSM_GROUPS * SSM_D_STATE:]


def _ssd_s_pre(proj_s, cst, cw, cb, dtb_pad, alog_pad):
    n = proj_s.shape[0]
    e = np.zeros((LANES, SSM_D_INNER), np.float32)
    for h in range(SSM_HEADS):
        e[h, h * SSM_HEAD_DIM:(h + 1) * SSM_HEAD_DIM] = 1.0
    gw = SSM_GROUPS * SSM_D_STATE
    full = lambda shape: pl.BlockSpec(shape, lambda i: (0, 0))
    return pl.pallas_call(
        _ssd_s_pre_kernel,
        grid=(1,),
        in_specs=[pl.BlockSpec((n, SSM_CONV_DIM), lambda i: (0, COL_XBC // SSM_CONV_DIM)),
                  full(cst.shape),
                  pl.BlockSpec((n, LANES), lambda i: (0, COL_SMALL // LANES)),
                  full(cw.shape), full(cb.shape), full((1, LANES)), full((1, LANES)), full(e.shape)],
        out_specs=[full((n, SSM_D_INNER)), full((n, SSM_D_INNER)), full((n, gw)), full((n, gw)),
                   full((n, LANES)), full(cst.shape)],
        out_shape=[jax.ShapeDtypeStruct((n, SSM_D_INNER), F32), jax.ShapeDtypeStruct((n, SSM_D_INNER), F32),
                   jax.ShapeDtypeStruct((n, gw), F32), jax.ShapeDtypeStruct((n, gw), F32),
                   jax.ShapeDtypeStruct((n, LANES), F32), jax.ShapeDtypeStruct(cst.shape, F32)],
        compiler_params=_cparams(("arbitrary",)),
        name="ssd_sample_pre",
    )(proj_s, cst, proj_s, cw, cb, dtb_pad, alog_pad, jnp.asarray(e, dtype=BF16))


def _dyn_row(ref, b, cols=slice(None)):
    tile = ref[pl.ds(pl.multiple_of((b >> 3) << 3, 8), 8), cols]
    r = lax.broadcasted_iota(jnp.int32, (8, 1), 0)
    return jnp.sum(jnp.where(r == (b & 7), tile, 0.0), axis=0, keepdims=True)


def _onehot_cols(b, n):
    return jnp.where(lax.broadcasted_iota(jnp.int32, (n, n), 0) == b, 1.0, 0.0).astype(BF16)


def _ssd_s_state_kernel(st_ref, xdt_ref, bm_ref, cm_ref, dec_ref, nst_ref, yT_ref, xT_s, dT_s, cT_s):
    b = pl.program_id(0)
    n = xdt_ref.shape[0]

    @pl.when(b == 0)
    def _():
        for i, part in enumerate(_split3(xdt_ref[...].T)):
            xT_s[i] = part
        for i, part in enumerate(_split3(dec_ref[...].T)):
            dT_s[i] = part
        cT_s[...] = _bf(cm_ref[...].T)
        yT_ref[...] = jnp.zeros_like(yT_ref)

    hb = _onehot_cols(b, n)
    r = _dot(xT_s[0], hb) + _dot(xT_s[1], hb) + _dot(xT_s[2], hb)
    dec_r = _dot(dT_s[0], hb) + _dot(dT_s[1], hb) + _dot(dT_s[2], hb)
    c_r = _bf(_dot(cT_s[...], hb))
    lane_is_b = lax.broadcasted_iota(jnp.int32, (1, n), 1) == b
    hpg = SSM_HEADS // SSM_GROUPS
    brow_all = _dyn_row(bm_ref, b)
    for h in range(SSM_HEADS):
        g = h // hpg
        brow = brow_all[:, g * SSM_D_STATE:(g + 1) * SSM_D_STATE]
        rows = slice(h * SSM_HEAD_DIM, (h + 1) * SSM_HEAD_DIM)
        new = st_ref[0, h] * dec_r[h:h + 1, :] + r[rows, :] * brow
        nst_ref[0, h] = new
        y_h = _dot(_bf(new), c_r[g * SSM_D_STATE:(g + 1) * SSM_D_STATE, :])
        yT_ref[rows, :] = jnp.where(lane_is_b, y_h, yT_ref[rows, :])


def _ssd_s_state(state, xdt, bm, cm, dec):
    n = xdt.shape[0]
    assert n == LANES and SSM_D_STATE == LANES
    full = lambda a: pl.BlockSpec(a.shape, lambda b: (0, 0))
    blk = pl.BlockSpec((1, SSM_HEADS, SSM_HEAD_DIM, SSM_D_STATE), lambda b: (b, 0, 0, 0))
    return pl.pallas_call(
        _ssd_s_state_kernel,
        grid=(n,),
        in_specs=[blk, full(xdt), full(bm), full(cm), full(dec)],
        out_specs=[blk, pl.BlockSpec((SSM_D_INNER, n), lambda b: (0, 0))],
        out_shape=[jax.ShapeDtypeStruct(state.shape, F32), jax.ShapeDtypeStruct((SSM_D_INNER, n), F32)],
        scratch_shapes=[pltpu.VMEM((3, SSM_D_INNER, n), BF16), pltpu.VMEM((3, LANES, n), BF16),
                        pltpu.VMEM((SSM_GROUPS * SSM_D_STATE, n), BF16)],
        compiler_params=_cparams(("arbitrary",)),
        name="ssd_sample_state",
    )(state, xdt, bm, cm, dec)


def _ssd_s_post_kernel(yT_ref, xs_ref, z_ref, drow_ref, nw_ref, o_ref):
    y = yT_ref[...].T + drow_ref[...] * xs_ref[...]
    v = y * _silu(z_ref[...])
    gw = SSM_D_INNER // SSM_GROUPS
    outs = []
    for g in range(SSM_GROUPS):
        vg = v[:, g * gw:(g + 1) * gw]
        ms = jnp.sum(vg * vg, axis=-1, keepdims=True) * (1.0 / gw)
        outs.append(vg * lax.rsqrt(ms + NORM_EPS) * nw_ref[:, g * gw:(g + 1) * gw])
    o_ref[...] = _bf(jnp.concatenate(outs, axis=1))


def _ssd_s_post(yT, xs, proj_s, drow, nw):
    n = xs.shape[0]
    full = lambda shape: pl.BlockSpec(shape, lambda i: (0, 0))
    return pl.pallas_call(
        _ssd_s_post_kernel,
        grid=(1,),
        in_specs=[full(yT.shape), full(xs.shape),
                  pl.BlockSpec((n, SSM_D_INNER), lambda i: (0, COL_Z // SSM_D_INNER)),
                  full(drow.shape), full(nw.shape)],
        out_specs=full((n, SSM_D_INNER)),
        out_shape=jax.ShapeDtypeStruct((n, SSM_D_INNER), BF16),
        compiler_params=_cparams(("arbitrary",)),
        name="ssd_sample_post",
    )(yT, xs, proj_s, drow, nw)


PAGES_PER_STEP = 16


def _q_block(q_ref, b, n):
    r_q = _dot(q_ref[0], _onehot_cols(b, n))
    lane = lax.broadcasted_iota(jnp.int32, (1, n), 1)
    blocks = []
    for h in range(NSA_KV_HEADS):
        blk = None
        for g in range(NSA_GQA):
            hd = h * NSA_GQA + g
            piece = jnp.where(lane == hd, r_q[hd * NSA_HEAD_DIM:(hd + 1) * NSA_HEAD_DIM, :], 0.0)
            blk = piece if blk is None else blk + piece
        blocks.append(blk)
    return _bf(jnp.concatenate(blocks, axis=0))


def _scatter_heads(o_t, out_ref, b, n):
    lane = lax.broadcasted_iota(jnp.int32, (1, n), 1)
    lane_is_b = lane == b
    for hd in range(NSA_HEADS):
        h = hd // NSA_GQA
        piece = o_t[h * NSA_HEAD_DIM:(h + 1) * NSA_HEAD_DIM, :]
        col = jnp.sum(jnp.where(lane == hd, piece, 0.0), axis=1, keepdims=True)
        rows = slice(hd * NSA_HEAD_DIM, (hd + 1) * NSA_HEAD_DIM)
        out_ref[rows, :] = jnp.where(lane_is_b, col, out_ref[rows, :])


def _nsa_s_cmp_kernel(pt_ref, *refs):
    pages = refs[:PAGES_PER_STEP]
    (qT_ref, wcat_ref, w2bd_ref, pe_ref, w1f_ref, smapT_ref, gsum_ref,
     otc_ref, sel_ref, data_s, xcat_s, pab_s) = refs[PAGES_PER_STEP:]
    b = pl.program_id(0)
    j = pl.program_id(1)
    nj = pl.num_programs(1)
    n = qT_ref.shape[2]
    ns = CMP_CHUNK // CMP_STRIDE
    n_tiles = KV_COLS // LANES
    for i in range(PAGES_PER_STEP):
        for c in range(n_tiles):
            data_s[c, i * PAGE_SIZE:(i + 1) * PAGE_SIZE, :] = pages[i][0, :, c * LANES:(c + 1) * LANES]
    parts = _compress_partial([data_s.at[c] for c in range(n_tiles)], wcat_ref, xcat_s)
    for v in range(2):
        pab_s[v, pl.ds(pl.multiple_of(j * ns, ns), ns), :] = parts[v]

    @pl.when((b == 0) & (j == 0))
    def _():
        otc_ref[...] = jnp.zeros_like(otc_ref)

    @pl.when(j == nj - 1)
    def _():
        kc, vc = _compress_finish(pab_s, _pe_bias(pe_ref, w1f_ref), w2bd_ref)
        n_sub = kc.shape[0]
        past_len = n_sub * CMP_STRIDE
        qblk = _q_block(qT_ref, b, n)
        crow = lax.broadcasted_iota(jnp.int32, (n_sub, 1), 0)
        valid = (crow * CMP_STRIDE + (CMP_LEN - 1)) <= past_len
        sc = jnp.where(valid, _dot(_bf(kc), qblk), NEG)
        m = jnp.max(sc, axis=0, keepdims=True)
        e = jnp.where(valid, jnp.exp(sc - m), 0.0)
        p = e / jnp.maximum(jnp.sum(e, axis=0, keepdims=True), 1e-30)
        _scatter_heads(_dot(_bf(vc.T), _bf(p)), otc_ref, b, n)
        p_hi = _bf(p)
        p_lo = _bf(p - p_hi.astype(F32))
        psum = _dot(p_hi, gsum_ref[...]) + _dot(p_lo, gsum_ref[...])
        q_hi = _bf(psum)
        q_lo = _bf(psum - q_hi.astype(F32))
        imp_t = _dot(smapT_ref[...], q_hi) + _dot(smapT_ref[...], q_lo)
        nj_pad = imp_t.shape[0]
        n_sel = past_len // SEL_BLOCK + 1
        cur = float(past_len // SEL_BLOCK)
        jrow = lax.broadcasted_iota(jnp.int32, (nj_pad, n), 0).astype(F32)
        forced = (jnp.where(jrow == 0.0, 1.0, 0.0) + jnp.where(jrow == cur, 1.0, 0.0)
                  + jnp.where(jrow == cur - 1.0, 1.0, 0.0))
        score = jnp.where(jrow > cur, -BIG, jnp.where(forced > 0.5, BIG, imp_t))
        score = jnp.where(jrow >= float(n_sel), -jnp.inf, score)
        sel_ref[0] = _topk_mask(score, jrow, min(SEL_TOP_N, n_sel))


def _page_specs(n_pages_step):
    return [pl.BlockSpec((1, PAGE_SIZE, KV_COLS),
                         functools.partial(lambda b, j, pt, i: (pt[b, j * n_pages_step + i], 0, 0), i=i))
            for i in range(n_pages_step)]


def _nsa_s_cmp(page_table, cache_cmp, qT, wcat, w2bd, pe8, w1f4, past_len):
    nseq = page_table.shape[0]
    n_sub = past_len // CMP_STRIDE
    n_cmp = n_sub - CMP_LEN // CMP_STRIDE + 1
    n_sel = past_len // SEL_BLOCK + 1
    nj_pad = -(-n_sel // 8) * 8
    smap = np.zeros((nj_pad, n_sub), bool)
    smap[:n_sel] = _selection_map_t(n_sel, n_sub) & (np.arange(n_sub)[None, :] < n_cmp)
    lanes = np.arange(nseq)
    gsum = ((lanes[:, None] // NSA_GQA) == (lanes[None, :] // NSA_GQA)) & (lanes[:, None] < NSA_HEADS) & (lanes[None, :] < NSA_HEADS)
    nj = past_len // CMP_CHUNK
    ns = CMP_CHUNK // CMP_STRIDE
    c2 = lambda a: pl.BlockSpec(a.shape, lambda b, j, pt: (0, 0))
    c3 = lambda a: pl.BlockSpec(a.shape, lambda b, j, pt: (0, 0, 0))
    smap_j = jnp.asarray(smap, dtype=BF16)
    gsum_j = jnp.asarray(gsum, dtype=BF16)
    grid_spec = pltpu.PrefetchScalarGridSpec(
        num_scalar_prefetch=1,
        grid=(nseq, nj),
        in_specs=_page_specs(PAGES_PER_STEP) + [c3(qT), c3(wcat), c3(w2bd), c3(pe8), c3(w1f4), c2(smap_j), c2(gsum_j)],
        out_specs=[pl.BlockSpec((D_MODEL, nseq), lambda b, j, pt: (0, 0)),
                   pl.BlockSpec((1, nj_pad, nseq), lambda b, j, pt: (b, 0, 0))],
        scratch_shapes=[pltpu.VMEM((KV_COLS // LANES, CMP_CHUNK, LANES), F32),
                        pltpu.VMEM((2, ns, CMP_STRIDE * KV_HALF), BF16),
                        pltpu.VMEM((2, n_sub, KV_COLS), F32)],
    )
    return pl.pallas_call(
        _nsa_s_cmp_kernel,
        grid_spec=grid_spec,
        out_shape=[jax.ShapeDtypeStruct((D_MODEL, nseq), F32),
                   jax.ShapeDtypeStruct((nseq, nj_pad, nseq), F32)],
        compiler_params=_cparams(("arbitrary", "arbitrary")),
        name="nsa_sample_cmp",
    )(page_table, *([cache_cmp] * PAGES_PER_STEP), qT, wcat, w2bd, pe8, w1f4, smap_j, gsum_j)


def _nsa_s_att_kernel(n_past_blk, pt_ref, *refs):
    pages = refs[:PAGES_PER_STEP]
    (qrT_ref, sel_ref, nsel_ref, win_ref, nwin_ref,
     ots_ref, otw_ref, nwo_ref, qblk_s, m_s, acc_s) = refs[PAGES_PER_STEP:]
    b = pl.program_id(0)
    j = pl.program_id(1)
    nj = pl.num_programs(1)
    n = qrT_ref.shape[2]
    ones16 = jnp.ones((16, PAGE_SIZE), BF16)
    blk_per_page = PAGE_SIZE // SEL_BLOCK

    @pl.when((b == 0) & (j == 0))
    def _():
        ots_ref[...] = jnp.zeros_like(ots_ref)
        otw_ref[...] = jnp.zeros_like(otw_ref)

    @pl.when(j == 0)
    def _():
        qblk_s[...] = _q_block(qrT_ref, b, n)
        m_s[...] = jnp.full(m_s.shape, NEG, F32)
        acc_s[...] = jnp.zeros_like(acc_s)

    qblk = qblk_s[...]

    def tile_update(kv, mask, m, acc):
        st = jnp.where(mask, _dot(_bf(kv[:, :KV_HALF]), qblk), NEG)
        vt = jnp.concatenate([_bf(kv[:, KV_HALF:].T), ones16], axis=0)
        return _online_step(st, vt, m, acc)

    def new_token_tile(row):
        r_i = lax.broadcasted_iota(jnp.int32, (PAGE_SIZE, 1), 0)
        return jnp.where(r_i == 0, jnp.broadcast_to(row, (PAGE_SIZE, KV_COLS)), 0.0), r_i == 0

    m, acc = m_s[...], acc_s[...]
    for i in range(PAGES_PER_STEP):
        base = j * (PAGES_PER_STEP * blk_per_page) + (i * blk_per_page // 8) * 8
        rows8 = sel_ref[0, pl.ds(pl.multiple_of(base, 8), 8), :]
        r0 = (i * blk_per_page) % 8
        mask = jnp.concatenate([jnp.broadcast_to(rows8[r0 + r:r0 + r + 1, :], (SEL_BLOCK, n))
                                for r in range(blk_per_page)], axis=0) > 0.5
        m, acc = tile_update(pages[i][0], mask, m, acc)
    m_s[...] = m
    acc_s[...] = acc

    @pl.when(j == nj - 1)
    def _():
        kv_new, is_row0 = new_token_tile(_dyn_row(nsel_ref, b))
        sel_new = sel_ref[0, n_past_blk:n_past_blk + 1, :] > 0.5
        m2, acc2 = tile_update(kv_new, is_row0 & sel_new, m_s[...], acc_s[...])
        _scatter_heads(acc2[0:KV_HALF, :] / acc2[KV_HALF:KV_HALF + 1, :], ots_ref, b, n)
        win = win_ref[0]
        mw = jnp.full(m_s.shape, NEG, F32)
        aw = jnp.zeros_like(acc2)
        all_true = jnp.full((PAGE_SIZE, 1), True)
        for i in range(win.shape[0] // PAGE_SIZE):
            mw, aw = tile_update(win[i * PAGE_SIZE:(i + 1) * PAGE_SIZE, :], all_true, mw, aw)
        nw_row = _dyn_row(nwin_ref, b)
        kv_new, is_row0 = new_token_tile(nw_row)
        mw, aw = tile_update(kv_new, is_row0, mw, aw)
        _scatter_heads(aw[0:KV_HALF, :] / aw[KV_HALF:KV_HALF + 1, :], otw_ref, b, n)
        w_rows = win.shape[0]
        shifted = pltpu.roll(win, w_rows - 1, 0)
        r_w = lax.broadcasted_iota(jnp.int32, (w_rows, 1), 0)
        nwo_ref[0] = jnp.where(r_w == w_rows - 1, jnp.broadcast_to(nw_row, win.shape), shifted)


def _nsa_s_att(page_table, cache_sel, qrT, sel, nsel_rows, cache_win, nwin_rows, past_len):
    nseq = page_table.shape[0]
    nj = past_len // (PAGES_PER_STEP * PAGE_SIZE)
    wbuf = cache_win.shape[1]
    c2 = lambda a: pl.BlockSpec(a.shape, lambda b, j, pt: (0, 0))
    c3 = lambda a: pl.BlockSpec(a.shape, lambda b, j, pt: (0, 0, 0))
    acc_rows = KV_HALF + 16
    grid_spec = pltpu.PrefetchScalarGridSpec(
        num_scalar_prefetch=1,
        grid=(nseq, nj),
        in_specs=_page_specs(PAGES_PER_STEP) + [
            c3(qrT),
            pl.BlockSpec((1,) + sel.shape[1:], lambda b, j, pt: (b, 0, 0)),
            c2(nsel_rows),
            pl.BlockSpec((1, wbuf, KV_COLS), lambda b, j, pt: (b, 0, 0)),
            c2(nwin_rows)],
        out_specs=[pl.BlockSpec((D_MODEL, nseq), lambda b, j, pt: (0, 0)),
                   pl.BlockSpec((D_MODEL, nseq), lambda b, j, pt: (0, 0)),
                   pl.BlockSpec((1, wbuf, KV_COLS), lambda b, j, pt: (b, 0, 0))],
        scratch_shapes=[pltpu.VMEM((KV_HALF, nseq), BF16), pltpu.VMEM((1, nseq), F32),
                        pltpu.VMEM((acc_rows, nseq), F32)],
    )
    return pl.pallas_call(
        functools.partial(_nsa_s_att_kernel, past_len // SEL_BLOCK),
        grid_spec=grid_spec,
        out_shape=[jax.ShapeDtypeStruct((D_MODEL, nseq), F32), jax.ShapeDtypeStruct((D_MODEL, nseq), F32),
                   jax.ShapeDtypeStruct((nseq, wbuf, KV_COLS), F32)],
        compiler_params=_cparams(("arbitrary", "arbitrary")),
        name="nsa_sample_att",
    )(page_table, *([cache_sel] * PAGES_PER_STEP), qrT, sel, nsel_rows, cache_win, nwin_rows)


def _nsa_s_out_kernel(otc_ref, ots_ref, otw_ref, gT_ref, wn_ref, o_ref):
    n = otc_ref.shape[1]
    parts = []
    for hd in range(NSA_HEADS):
        rows = slice(hd * NSA_HEAD_DIM, (hd + 1) * NSA_HEAD_DIM)
        parts.append(gT_ref[0, 3 * hd:3 * hd + 1, :] * otc_ref[rows, :]
                     + gT_ref[0, 3 * hd + 1:3 * hd + 2, :] * ots_ref[rows, :]
                     + gT_ref[0, 3 * hd + 2:3 * hd + 3, :] * otw_ref[rows, :])
    o_t = _bf(jnp.concatenate(parts, axis=0))
    o_ref[...] = _dot_tn(o_t, wn_ref[...])


def _nsa_s_out(otc, ots, otw, gT, wn):
    n = otc.shape[1]
    f2 = lambda a: pl.BlockSpec(a.shape, lambda i: (0, 0))
    f3 = lambda a: pl.BlockSpec(a.shape, lambda i: (0, 0, 0))
    return pl.pallas_call(
        _nsa_s_out_kernel,
        grid=(1,),
        in_specs=[f2(otc), f2(ots), f2(otw), f3(gT), f2(wn)],
        out_specs=pl.BlockSpec((n, D_MODEL), lambda i: (0, 0)),
        out_shape=jax.ShapeDtypeStruct((n, D_MODEL), F32),
        compiler_params=_cparams(("arbitrary",)),
        name="nsa_sample_out",
    )(otc, ots, otw, gT, wn)


def _rope_tables(pos):
    half = ROPE_DIM // 2
    inv_freq = jnp.power(ROPE_THETA, -jnp.arange(half, dtype=F32) * 2.0 / ROPE_DIM)
    ang = pos.astype(F32)[None, :] * inv_freq[:, None]
    return jnp.cos(ang), jnp.sin(ang)


def _permute_w_in(w_in):
    sizes = (SSM_D_INNER, SSM_CONV_DIM, SSM_HEADS, NSA_HEADS * NSA_HEAD_DIM, KV_COLS, KV_COLS, KV_COLS,
             3 * NSA_HEADS, 2 * D_MODEL)
    offs = np.concatenate([[0], np.cumsum(sizes)])
    z, xbc, dt, q, kvc, kvs, kvw, ng, mg = (w_in[:, offs[i]:offs[i + 1]] for i in range(len(sizes)))
    pad = jnp.zeros((D_MODEL, LANES - SSM_HEADS - 3 * NSA_HEADS), w_in.dtype)
    return _bf(jnp.concatenate([xbc, q, z, mg, kvc, kvs, kvw, dt, ng, pad], axis=1))


def _compress_weights(cmp_pe, cmp_w1, cmp_w2):
    eye = jnp.eye(NSA_KV_HEADS, dtype=F32)
    w1a = cmp_w1[:, :CMP_STRIDE]
    w1b = cmp_w1[:, CMP_STRIDE:]
    bd = lambda w: jnp.einsum('vlde,hk->vlhdke', w, eye).reshape(2, CMP_STRIDE * KV_HALF, KV_HALF)
    wcat = _bf(jnp.concatenate([bd(w1a), bd(w1b)], axis=2))
    w2bd = _bf(jnp.einsum('vef,hk->vhekf', cmp_w2, eye).reshape(2, KV_HALF, KV_HALF))
    pe8 = jnp.zeros((2, 8, CMP_LEN * NSA_HEAD_DIM), F32).at[:, 0, :].set(cmp_pe.reshape(2, -1))
    w1f4 = jnp.tile(cmp_w1.reshape(2, CMP_LEN * NSA_HEAD_DIM, NSA_HEAD_DIM), (1, 1, NSA_KV_HEADS))
    return wcat, w2bd, pe8, w1f4


def kernel(x_prompt, x_sample, cache_cmp_kv, cache_sel_kv, cache_win_kv, state_ssm, state_conv, page_table,
           w_in, conv_w, conv_b, dt_bias, a_log, d_skip, ssm_norm_w, w_ssm_out, cmp_pe, cmp_w1, cmp_w2,
           w_nsa_out, w_o, ln1_g, ln1_b, w_gate, w_up, w_down, ln2_g, ln2_b):
    bsz, t, _ = x_prompt.shape
    nseq, dec_seq, _ = x_sample.shape
    n_pool = cache_cmp_kv.shape[1]
    past_len = page_table.shape[1] * PAGE_SIZE
    assert w_in.shape[0] == 1 and dec_seq == 1 and nseq == LANES
    assert t % CMP_CHUNK == 0 and past_len % CMP_CHUNK == 0 and cache_win_kv.shape[2] == WINDOW

    w_in_p = _permute_w_in(w_in[0])
    pad_row = lambda v: jnp.zeros((1, LANES), F32).at[0, :SSM_HEADS].set(v)
    dtb_pad, alog_pad = pad_row(dt_bias[0]), pad_row(a_log[0])
    drow = jnp.repeat(d_skip[0], SSM_HEAD_DIM)[None, :]
    nw = ssm_norm_w[0][None, :]
    cw, cb = conv_w[0], conv_b[0][None, :]
    wcat, w2bd, pe8, w1f4 = _compress_weights(cmp_pe[0], cmp_w1[0], cmp_w2[0])
    w_ssm_b, w_nsa_b, w_o_b = _bf(w_ssm_out[0]), _bf(w_nsa_out[0]), _bf(w_o[0])
    nc = FFN_HIDDEN // FFN_CHUNK
    wg3 = _bf(w_gate[0]).reshape(D_MODEL, nc, FFN_CHUNK).transpose(1, 0, 2)
    wu3 = _bf(w_up[0]).reshape(D_MODEL, nc, FFN_CHUNK).transpose(1, 0, 2)
    wd3 = _bf(w_down[0]).reshape(nc, FFN_CHUNK, D_MODEL)
    g1, b1, g2, b2 = ln1_g[0][None, :], ln1_b[0][None, :], ln2_g[0][None, :], ln2_b[0][None, :]

    xp = x_prompt.reshape(bsz * t, D_MODEL)
    proj = _matmul(xp, w_in_p, 1024, PROJ_TN, "in_proj")
    y_ssm, new_ssm_p, new_conv_p = _ssd_prompt(proj, bsz, t, cw, cb, dtb_pad, alog_pad, drow, nw)
    b_ssm = _matmul(y_ssm, w_ssm_b, 1024, D_MODEL, "ssm_out_proj")
    cos_p, sin_p = _rope_tables(jnp.arange(t, dtype=jnp.int32))
    qT, qrT, ks, vsT, kw, vwT, gT, ncmp, nsel, nwin = _attn_prep(proj, bsz, t, KEY_SLAB, cos_p, sin_p)
    kc, vcT = _compress_prompt(proj, bsz, t, wcat, w2bd, pe8, w1f4)
    b_nsa = _nsa_prompt(qT, qrT, gT, kc, vcT, ks, vsT, kw, vwT, w_nsa_b, bsz, t)
    h = _merge(xp, proj, b_ssm, b_nsa, w_o_b, g1, b1, 512)
    y_p = _ffn(h, wg3, wu3, wd3, g2, b2, 512).reshape(bsz, t, D_MODEL)
    kv6 = lambda a, n_b, n_t: a.reshape(1, n_b, n_t, 2, NSA_KV_HEADS, NSA_HEAD_DIM)
    w_keep = min(WINDOW, t)
    new_win_p = kv6(nwin.reshape(bsz, t, KV_COLS)[:, t - w_keep:], bsz, w_keep)

    xs_in = x_sample.reshape(nseq, D_MODEL)
    proj_s = _matmul(xs_in, w_in_p, nseq, PROJ_TN, "in_proj_s")
    cst = state_conv[0].reshape(nseq, (SSM_CONV_W - 1) * SSM_CONV_DIM)
    xs_s, xdt_s, bm_s, cm_s, dec_s, ncv_s = _ssd_s_pre(proj_s, cst, cw, cb, dtb_pad, alog_pad)
    new_ssm_s, yT_s = _ssd_s_state(state_ssm[0], xdt_s, bm_s, cm_s, dec_s)
    y_ssm_s = _ssd_s_post(yT_s, xs_s, proj_s, drow, nw)
    b_ssm_s = _matmul(y_ssm_s, w_ssm_b, nseq, D_MODEL, "ssm_out_proj_s")
    cos_s, sin_s = _rope_tables(jnp.full((nseq,), past_len, dtype=jnp.int32))
    qT_s, qrT_s, _, _, _, _, gT_s, ncmp_s, nsel_s, nwin_s = _attn_prep(proj_s, 1, nseq, nseq, cos_s, sin_s)
    cache_cmp = cache_cmp_kv[0].reshape(n_pool, PAGE_SIZE, KV_COLS)
    cache_sel = cache_sel_kv[0].reshape(n_pool, PAGE_SIZE, KV_COLS)
    cache_win = cache_win_kv[0].reshape(nseq, WINDOW, KV_COLS)
    otc, sel = _nsa_s_cmp(page_table, cache_cmp, qT_s, wcat, w2bd, pe8, w1f4, past_len)
    ots, otw, new_win_s = _nsa_s_att(page_table, cache_sel, qrT_s, sel, nsel_s, cache_win, nwin_s, past_len)
    b_nsa_s = _nsa_s_out(otc, ots, otw, gT_s, w_nsa_b)
    h_s = _merge(xs_in, proj_s, b_ssm_s, b_nsa_s, w_o_b, g1, b1, nseq)
    y_s = _ffn(h_s, wg3, wu3, wd3, g2, b2, nseq).reshape(nseq, 1, D_MODEL)

    return (y_p, y_s,
            kv6(ncmp, bsz, t), kv6(nsel, bsz, t), new_win_p,
            new_ssm_p[None], new_conv_p[None],
            kv6(ncmp_s, nseq, 1), kv6(nsel_s, nseq, 1),
            new_win_s.reshape(1, nseq, WINDOW, 2, NSA_KV_HEADS, NSA_HEAD_DIM),
            new_ssm_s[None],
            ncv_s.reshape(1, nseq, SSM_CONV_W - 1, SSM_CONV_DIM))
```

```python
import functools
import math

import numpy as np
import jax
import jax.numpy as jnp
from jax import lax
from jax.experimental import pallas as pl
from jax.experimental.pallas import tpu as pltpu

F32 = jnp.float32
BF16 = jnp.bfloat16

D_MODEL = 1024
SSM_D_INNER = 2048
SSM_HEAD_DIM = 64
SSM_HEADS = 32
SSM_GROUPS = 4
SSM_D_STATE = 128
SSM_CONV_W = 4
SSM_CONV_DIM = 3072
SSM_CHUNK = 128
NSA_HEADS = 16
NSA_KV_HEADS = 4
NSA_HEAD_DIM = 64
NSA_GQA = 4
KV_COLS = 512
KV_HALF = 256
CMP_LEN = 32
CMP_STRIDE = 16
SEL_BLOCK = 64
SEL_TOP_N = 16
WINDOW = 512
Q_BLOCK = 128
ROPE_DIM = 16
ROPE_THETA = 500000.0
FFN_HIDDEN = 2816
NORM_EPS = 1e-5
BIG = 1e30
NEG = -1e30
DEPTH = 1
DEEPNORM_ALPHA = (2 * DEPTH) ** 0.25
PAGE_SIZE = 128

LANES = 128
VMEM_LIMIT_BYTES = 56 * 1024 * 1024

COL_XBC = 0
COL_Q = 3072
COL_Z = 4096
COL_MG = 6144
COL_KVC = 8192
COL_KVS = 8704
COL_KVW = 9216
COL_SMALL = 9728
PROJ_COLS = 9856
PROJ_TN = 896

Q_SCALE = NSA_HEAD_DIM ** -0.5 * math.log2(math.e)
KEY_SLAB = 512
BLOCKS_PER_SLAB = KEY_SLAB // SEL_BLOCK
CHUNKS_PER_SLAB = KEY_SLAB // Q_BLOCK
CHUNK_SHIFT = 2
assert 1 << CHUNK_SHIFT == CHUNKS_PER_SLAB
QCHUNK = 256
N_QCHUNK = NSA_GQA * Q_BLOCK // QCHUNK
MXU_LOOKAHEAD = 12
CMP_CHUNK = 2048
FFN_CHUNK = 256


def _dot_dims(a, b, dims):
    return lax.dot_general(a, b, (dims, ((), ())), preferred_element_type=F32)


def _dot(a, b):
    return _dot_dims(a, b, ((1,), (0,)))


def _dot_nt(a, b):
    return _dot_dims(a, b, ((1,), (1,)))


def _dot_tn(a, b):
    return _dot_dims(a, b, ((0,), (0,)))


def _bf(x):
    return x.astype(BF16)


def _split3(x):
    hi = _bf(x)
    r1 = x - hi.astype(F32)
    mid = _bf(r1)
    lo = _bf(r1 - mid.astype(F32))
    return hi, mid, lo


def _dot_exact_lhs(x, w_bf16):
    hi, mid, lo = _split3(x)
    return _dot(hi, w_bf16) + _dot(mid, w_bf16) + _dot(lo, w_bf16)


def _dot_exact_rhs(w_bf16, x):
    hi, mid, lo = _split3(x)
    return _dot(w_bf16, hi) + _dot(w_bf16, mid) + _dot(w_bf16, lo)


def _silu(x):
    return x * jax.nn.sigmoid(x)


def _softplus(x):
    return jnp.maximum(x, 0.0) + jnp.log1p(jnp.exp(-jnp.abs(x)))


def _gelu_tanh(x):
    return 0.5 * x * (1.0 + jnp.tanh(0.7978845608028654 * (x + 0.044715 * (x * x * x))))


def _layer_norm(x, g, b):
    mu = jnp.mean(x, axis=-1, keepdims=True)
    xc = x - mu
    var = jnp.mean(xc * xc, axis=-1, keepdims=True)
    return xc * lax.rsqrt(var + NORM_EPS) * g + b


def _cparams(sem):
    return pltpu.CompilerParams(dimension_semantics=sem, vmem_limit_bytes=VMEM_LIMIT_BYTES)


def _mm_kernel(x_ref, w_ref, o_ref):
    o_ref[...] = _dot(_bf(x_ref[...]), w_ref[...])


def _matmul(x, w, tm, tn, name):
    m, k = x.shape
    n = w.shape[1]
    return pl.pallas_call(
        _mm_kernel,
        grid=(m // tm, n // tn),
        in_specs=[pl.BlockSpec((tm, k), lambda i, j: (i, 0)),
                  pl.BlockSpec((k, tn), lambda i, j: (0, j))],
        out_specs=pl.BlockSpec((tm, tn), lambda i, j: (i, j)),
        out_shape=jax.ShapeDtypeStruct((m, n), F32),
        compiler_params=_cparams(("parallel", "arbitrary")),
        name=name,
    )(x, w)


def _ssd_kernel(xbc_ref, z_ref, sm_ref, cw_ref, cb_ref, dtb_ref, alog_ref, drow_ref, nw_ref, r3_ref,
                y_ref, st_ref, cv_ref, xp_s, stT_s):
    j = pl.program_id(1)
    nj = pl.num_programs(1)
    q = SSM_CHUNK

    @pl.when(j == 0)
    def _():
        xp_s[0:8, :] = jnp.zeros((8, SSM_CONV_DIM), F32)
        stT_s[...] = jnp.zeros_like(stT_s)

    xp_s[8:8 + q, :] = xbc_ref[...]
    acc = cb_ref[...]
    for k in range(SSM_CONV_W):
        acc = acc + cw_ref[k:k + 1, :] * xp_s[5 + k:5 + k + q, :]
    act = _silu(acc)
    tail = xp_s[q + 5:q + 8, :]
    xp_s[5:8, :] = tail

    @pl.when(j == nj - 1)
    def _():
        cv_ref[0] = tail

    lane = lax.broadcasted_iota(jnp.int32, (1, LANES), 1)
    a_full = jnp.where(lane < SSM_HEADS, -jnp.exp(alog_ref[...]), 0.0)
    dt = _softplus(sm_ref[...] + dtb_ref[...])
    da = dt * a_full
    row_i = lax.broadcasted_iota(jnp.int32, (q, q), 0)
    col_i = lax.broadcasted_iota(jnp.int32, (q, q), 1)
    causal = col_i <= row_i
    tril = jnp.where(causal, 1.0, 0.0).astype(BF16)
    cum = _dot_exact_rhs(tril, da)
    cumT = cum.T
    dtT = dt.T
    hi, mid, lo = _split3(cum)
    packed = _bf(hi.astype(F32) + pltpu.roll(mid.astype(F32), 32, 1) + pltpu.roll(lo.astype(F32), 64, 1))
    lane_lo = lax.broadcasted_iota(jnp.int32, (1, LANES), 1) < SSM_HEAD_DIM

    heads_per_group = SSM_HEADS // SSM_GROUPS
    y_parts = []
    for g in range(SSM_GROUPS):
        bm_g = act[:, SSM_D_INNER + g * SSM_D_STATE:SSM_D_INNER + (g + 1) * SSM_D_STATE]
        cm_g = act[:, SSM_D_INNER + SSM_GROUPS * SSM_D_STATE + g * SSM_D_STATE:
                   SSM_D_INNER + SSM_GROUPS * SSM_D_STATE + (g + 1) * SSM_D_STATE]
        cmb = _bf(cm_g)
        cb = _dot_nt(cmb, _bf(bm_g))
        bT = bm_g.T
        for pp in range(heads_per_group // 2):
            pair = g * (heads_per_group // 2) + pp
            h0 = 2 * pair
            xs_pair = act[:, pair * LANES:(pair + 1) * LANES]
            xs_a = _bf(jnp.where(lane_lo, xs_pair, 0.0))
            xs_b = _bf(jnp.where(lane_lo, 0.0, xs_pair))
            cols2 = _dot(packed, r3_ref[:, pair * 2 * LANES:(pair + 1) * 2 * LANES])
            y_pair = None
            ds_pair = None
            lasts = []
            cols = []
            for hh, xs_m in ((0, xs_a), (1, xs_b)):
                h = h0 + hh
                col = cols2[:, hh * LANES:(hh + 1) * LANES]
                row = cumT[h:h + 1, :]
                dtrow = dtT[h:h + 1, :]
                dec = jnp.exp(jnp.where(causal, col - row, NEG))
                m_h = _bf(cb * dec * dtrow)
                y_h = _dot(m_h, xs_m)
                last = col[q - 1:q, :]
                wrow = jnp.exp(last - row) * dtrow
                ds_h = _dot(_bf(bT * wrow), xs_m)
                y_pair = y_h if y_pair is None else y_pair + y_h
                ds_pair = ds_h if ds_pair is None else ds_pair + ds_h
                lasts.append(last)
                cols.append(col)
            st_pair = stT_s[:, pair * LANES:(pair + 1) * LANES]
            scale_t = jnp.exp(jnp.where(lane_lo, cols[0], cols[1]))
            y_pair = y_pair + _dot(cmb, _bf(st_pair)) * scale_t
            stT_s[:, pair * LANES:(pair + 1) * LANES] = (
                st_pair * jnp.exp(jnp.where(lane_lo, lasts[0], lasts[1])) + ds_pair)
            y_pair = y_pair + drow_ref[:, pair * LANES:(pair + 1) * LANES] * xs_pair
            y_parts.append(y_pair)
    y = jnp.concatenate(y_parts, axis=1)
    v = y * _silu(z_ref[...])
    gw = SSM_D_INNER // SSM_GROUPS
    outs = []
    for g in range(SSM_GROUPS):
        vg = v[:, g * gw:(g + 1) * gw]
        ms = jnp.sum(vg * vg, axis=-1, keepdims=True) * (1.0 / gw)
        outs.append(vg * lax.rsqrt(ms + NORM_EPS) * nw_ref[:, g * gw:(g + 1) * gw])
    y_ref[...] = _bf(jnp.concatenate(outs, axis=1))

    @pl.when(j == nj - 1)
    def _():
        st_ref[0] = stT_s[...].T.reshape(SSM_HEADS, SSM_HEAD_DIM, SSM_D_STATE)


def _r3_table():
    k = np.arange(LANES)[:, None]
    c = np.arange(SSM_HEADS * LANES)[None, :]
    return jnp.asarray(((k % SSM_HEADS) == (c // LANES)) & (k < 3 * SSM_HEADS), dtype=BF16)


def _ssd_prompt(proj, bsz, t, cw, cb, dtb_pad, alog_pad, drow, nw):
    nch = t // SSM_CHUNK
    q = SSM_CHUNK
    row = lambda b, j: b * nch + j
    const = lambda shape: pl.BlockSpec(shape, lambda b, j: (0, 0))
    return pl.pallas_call(
        _ssd_kernel,
        grid=(bsz, nch),
        in_specs=[
            pl.BlockSpec((q, SSM_CONV_DIM), lambda b, j: (row(b, j), COL_XBC // SSM_CONV_DIM)),
            pl.BlockSpec((q, SSM_D_INNER), lambda b, j: (row(b, j), COL_Z // SSM_D_INNER)),
            pl.BlockSpec((q, LANES), lambda b, j: (row(b, j), COL_SMALL // LANES)),
            const((SSM_CONV_W, SSM_CONV_DIM)), const((1, SSM_CONV_DIM)),
            const((1, LANES)), const((1, LANES)), const((1, SSM_D_INNER)), const((1, SSM_D_INNER)),
            const((LANES, SSM_HEADS * LANES)),
        ],
        out_specs=[
            pl.BlockSpec((q, SSM_D_INNER), lambda b, j: (row(b, j), 0)),
            pl.BlockSpec((1, SSM_HEADS, SSM_HEAD_DIM, SSM_D_STATE), lambda b, j: (b, 0, 0, 0)),
            pl.BlockSpec((1, SSM_CONV_W - 1, SSM_CONV_DIM), lambda b, j: (b, 0, 0)),
        ],
        out_shape=[
            jax.ShapeDtypeStruct((bsz * t, SSM_D_INNER), BF16),
            jax.ShapeDtypeStruct((bsz, SSM_HEADS, SSM_HEAD_DIM, SSM_D_STATE), F32),
            jax.ShapeDtypeStruct((bsz, SSM_CONV_W - 1, SSM_CONV_DIM), F32),
        ],
        scratch_shapes=[pltpu.VMEM((q + 8, SSM_CONV_DIM), F32),
                        pltpu.VMEM((SSM_D_STATE, SSM_D_INNER), F32)],
        compiler_params=_cparams(("parallel", "arbitrary")),
        name="ssd_prompt",
    )(proj, proj, proj, cw, cb, dtb_pad, alog_pad, drow, nw, _r3_table())


def _rope_t(x_t, nh, c, s):
    n = x_t.shape[1]
    x3 = x_t.reshape(nh, NSA_HEAD_DIM, n)
    half = ROPE_DIM // 2
    x1 = x3[:, 0:half, :]
    x2 = x3[:, half:ROPE_DIM, :]
    r1 = x1 * c - x2 * s
    r2 = x2 * c + x1 * s
    return jnp.concatenate([r1, r2, x3[:, ROPE_DIM:, :]], axis=1).reshape(nh * NSA_HEAD_DIM, n)


def _prep_kernel(q_ref, kvc_ref, kvs_ref, kvw_ref, sm_ref, cos_ref, sin_ref,
                 qT_ref, qrT_ref, ks_ref, vsT_ref, kw_ref, vwT_ref, gT_ref, ncmp_ref, nsel_ref, nwin_ref):
    c = cos_ref[...]
    s = sin_ref[...]
    tt = q_ref.shape[0]
    q_t = q_ref[...].T
    qT_ref[0] = _bf(q_t * Q_SCALE)
    qrT_ref[0] = _bf(_rope_t(q_t, NSA_HEADS, c, s) * Q_SCALE)
    ncmp_ref[0] = kvc_ref[...].T
    for src, full_out, k_out, vt_out in ((kvs_ref, nsel_ref, ks_ref, vsT_ref), (kvw_ref, nwin_ref, kw_ref, vwT_ref)):
        kv = src[...]
        k_rot_t = _rope_t(kv[:, :KV_HALF].T, NSA_KV_HEADS, c, s)
        k_rot = k_rot_t.T
        v_t = kv[:, KV_HALF:].T
        full_out[0, :KV_HALF, :] = k_rot_t
        full_out[0, KV_HALF:, :] = v_t
        for i in range(tt // Q_BLOCK):
            k_out[0, i] = _bf(k_rot[i * Q_BLOCK:(i + 1) * Q_BLOCK, :])
            vt_out[0, i] = _bf(v_t[:, i * Q_BLOCK:(i + 1) * Q_BLOCK])
    g_t = jax.nn.sigmoid(sm_ref[...]).T
    gT_ref[0] = g_t[SSM_HEADS:SSM_HEADS + 3 * NSA_HEADS, :]


def _attn_prep(proj, bsz, t, tt, cos_t, sin_t):
    nt = t // tt
    nb = tt // Q_BLOCK
    row = lambda b, j: b * nt + j
    n_gate = 3 * NSA_HEADS
    return pl.pallas_call(
        _prep_kernel,
        grid=(bsz, nt),
        in_specs=[
            pl.BlockSpec((tt, D_MODEL), lambda b, j: (row(b, j), COL_Q // D_MODEL)),
            pl.BlockSpec((tt, KV_COLS), lambda b, j: (row(b, j), COL_KVC // KV_COLS)),
            pl.BlockSpec((tt, KV_COLS), lambda b, j: (row(b, j), COL_KVS // KV_COLS)),
            pl.BlockSpec((tt, KV_COLS), lambda b, j: (row(b, j), COL_KVW // KV_COLS)),
            pl.BlockSpec((tt, LANES), lambda b, j: (row(b, j), COL_SMALL // LANES)),
            pl.BlockSpec((ROPE_DIM // 2, tt), lambda b, j: (0, j)),
            pl.BlockSpec((ROPE_DIM // 2, tt), lambda b, j: (0, j)),
        ],
        out_specs=[
            pl.BlockSpec((1, D_MODEL, tt), lambda b, j: (b, 0, j)),
            pl.BlockSpec((1, D_MODEL, tt), lambda b, j: (b, 0, j)),
            pl.BlockSpec((1, nb, Q_BLOCK, KV_HALF), lambda b, j: (b, j, 0, 0)),
            pl.BlockSpec((1, nb, KV_HALF, Q_BLOCK), lambda b, j: (b, j, 0, 0)),
            pl.BlockSpec((1, nb, Q_BLOCK, KV_HALF), lambda b, j: (b, j, 0, 0)),
            pl.BlockSpec((1, nb, KV_HALF, Q_BLOCK), lambda b, j: (b, j, 0, 0)),
            pl.BlockSpec((1, n_gate, tt), lambda b, j: (b, 0, j)),
            pl.BlockSpec((1, KV_COLS, tt), lambda b, j: (b, 0, j)),
            pl.BlockSpec((1, KV_COLS, tt), lambda b, j: (b, 0, j)),
            pl.BlockSpec((1, KV_COLS, tt), lambda b, j: (b, 0, j)),
        ],
        out_shape=[
            jax.ShapeDtypeStruct((bsz, D_MODEL, t), BF16),
            jax.ShapeDtypeStruct((bsz, D_MODEL, t), BF16),
            jax.ShapeDtypeStruct((bsz, t // Q_BLOCK, Q_BLOCK, KV_HALF), BF16),
            jax.ShapeDtypeStruct((bsz, t // Q_BLOCK, KV_HALF, Q_BLOCK), BF16),
            jax.ShapeDtypeStruct((bsz, t // Q_BLOCK, Q_BLOCK, KV_HALF), BF16),
            jax.ShapeDtypeStruct((bsz, t // Q_BLOCK, KV_HALF, Q_BLOCK), BF16),
            jax.ShapeDtypeStruct((bsz, n_gate, t), F32),
            jax.ShapeDtypeStruct((bsz, KV_COLS, t), F32),
            jax.ShapeDtypeStruct((bsz, KV_COLS, t), F32),
            jax.ShapeDtypeStruct((bsz, KV_COLS, t), F32),
        ],
        compiler_params=_cparams(("parallel", "parallel")),
        name="attn_prep",
    )(proj, proj, proj, proj, proj, cos_t, sin_t)


def _compress_partial(data_refs, wcat_ref, xcat_s):
    ns = data_refs[0].shape[0] // CMP_STRIDE
    tiles_per_half = KV_HALF // LANES
    outs = []
    for lp in range(CMP_STRIDE):
        for c, ref in enumerate(data_refs):
            rows = ref[pl.ds(lp, ns, stride=CMP_STRIDE), :]
            v, cc = divmod(c, tiles_per_half)
            xcat_s[v, :, lp * KV_HALF + cc * LANES:lp * KV_HALF + (cc + 1) * LANES] = _bf(rows)
    for v in range(2):
        outs.append(_dot(xcat_s[v], wcat_ref[v]))
    return outs


def _compress_finish(pab_s, bias, w2bd_ref):
    n_sub = pab_s.shape[1]
    res = []
    for v in range(2):
        pab = pab_s[v]
        pre = pab[:, :KV_HALF] + pltpu.roll(pab[:, KV_HALF:], n_sub - 1, 0) + bias[v:v + 1, :]
        res.append(_dot(_bf(_gelu_tanh(pre)), w2bd_ref[v]))
    return res


def _pe_bias(pe_ref, w1f_ref):
    rows = []
    for v in range(2):
        rows.append(jnp.dot(pe_ref[v], w1f_ref[v], preferred_element_type=F32,
                            precision=lax.Precision.HIGHEST)[0:1, :])
    return jnp.concatenate(rows + [jnp.zeros((6, KV_HALF), F32)], axis=0)


def _compress_prompt_kernel(d0_ref, d1_ref, d2_ref, d3_ref, wcat_ref, w2bd_ref, pe_ref, w1f_ref,
                            kc_ref, vcT_ref, xcat_s, pab_s):
    j = pl.program_id(1)
    nj = pl.num_programs(1)
    ns = CMP_CHUNK // CMP_STRIDE
    parts = _compress_partial((d0_ref, d1_ref, d2_ref, d3_ref), wcat_ref, xcat_s)
    for v in range(2):
        pab_s[v, pl.ds(pl.multiple_of(j * ns, ns), ns), :] = parts[v]

    @pl.when(j == nj - 1)
    def _():
        kc, vc = _compress_finish(pab_s, _pe_bias(pe_ref, w1f_ref), w2bd_ref)
        kc_ref[0] = _bf(kc)
        vcT_ref[0] = _bf(vc.T)


def _compress_prompt(proj, bsz, t, wcat, w2bd, pe8, w1f4):
    nj = t // CMP_CHUNK
    n_sub = t // CMP_STRIDE
    ns = CMP_CHUNK // CMP_STRIDE
    c3 = lambda shape: pl.BlockSpec(shape, lambda b, j: (0, 0, 0))
    return pl.pallas_call(
        _compress_prompt_kernel,
        grid=(bsz, nj),
        in_specs=[pl.BlockSpec((CMP_CHUNK, LANES),
                               functools.partial(lambda b, j, c: (b * nj + j, COL_KVC // LANES + c), c=c))
                  for c in range(KV_COLS // LANES)]
        + [c3(wcat.shape), c3(w2bd.shape), c3(pe8.shape), c3(w1f4.shape)],
        out_specs=[pl.BlockSpec((1, n_sub, KV_HALF), lambda b, j: (b, 0, 0)),
                   pl.BlockSpec((1, KV_HALF, n_sub), lambda b, j: (b, 0, 0))],
        out_shape=[jax.ShapeDtypeStruct((bsz, n_sub, KV_HALF), BF16),
                   jax.ShapeDtypeStruct((bsz, KV_HALF, n_sub), BF16)],
        scratch_shapes=[pltpu.VMEM((2, ns, CMP_STRIDE * KV_HALF), BF16),
                        pltpu.VMEM((2, n_sub, KV_COLS), F32)],
        compiler_params=_cparams(("parallel", "arbitrary")),
        name="compress_prompt",
    )(proj, proj, proj, proj, wcat, w2bd, pe8, w1f4)


def _topk_mask(score, jrow, n_pick):
    sel = jnp.zeros_like(score)
    for _ in range(n_pick):
        mx = jnp.max(score, axis=0, keepdims=True)
        idx = jnp.min(jnp.where(score == mx, jrow, 1e9), axis=0, keepdims=True)
        chosen = jrow == idx
        sel = jnp.where(chosen, 1.0, sel)
        score = jnp.where(chosen, -jnp.inf, score)
    return sel


def _online_step(st, vt_aug, m, acc):
    mn = jnp.maximum(m, jnp.max(st, axis=0, keepdims=True))
    p = jnp.exp2(st - mn)
    acc = acc * jnp.exp2(m - mn) + _dot(vt_aug, _bf(p))
    return mn, acc


def _nsa_prompt_kernel(qT_ref, qrT_ref, gT_ref, kc_ref, vcT_ref, ks_ref, vsT_ref, kw_ref, vwT_ref, smapT_ref,
                       econst_ref, wn_ref, o_ref, qr_s, neg_s, oc_s, m_s, acc_s):
    qb = pl.program_id(1)
    nq = NSA_GQA * Q_BLOCK
    n_cmp_rows = kc_ref.shape[1]
    n_selblk = smapT_ref.shape[0]
    n_slab = n_selblk // BLOCKS_PER_SLAB
    lane_q = lax.broadcasted_iota(jnp.int32, (1, nq), 1) % Q_BLOCK
    qpos = qb * Q_BLOCK + lane_q
    zero_q = jnp.zeros((NSA_HEAD_DIM, nq), BF16)
    acc_rows = NSA_HEAD_DIM + 16

    jrow = lax.broadcasted_iota(jnp.int32, (n_selblk, Q_BLOCK), 0).astype(F32)
    cur = (qb * (Q_BLOCK // SEL_BLOCK)
           + lax.broadcasted_iota(jnp.int32, (1, Q_BLOCK), 1) // SEL_BLOCK).astype(F32)
    future = jrow > cur
    forced = jnp.where(jrow == 0.0, 1.0, 0.0) + jnp.where(jrow == cur, 1.0, 0.0) + jnp.where(jrow == cur - 1.0, 1.0, 0.0)
    crow = lax.broadcasted_iota(jnp.int32, (n_cmp_rows, 1), 0)
    cmp_valid = (crow * CMP_STRIDE + (CMP_LEN - 1)) <= qpos

    def q_cat(ref, h):
        return jnp.concatenate([ref[0, (h * NSA_GQA + g) * NSA_HEAD_DIM:(h * NSA_GQA + g + 1) * NSA_HEAD_DIM, :]
                                for g in range(NSA_GQA)], axis=1)

    for h in range(NSA_KV_HEADS):
        qp = jnp.concatenate([q_cat(qT_ref, h) if hh == h else zero_q for hh in range(NSA_KV_HEADS)], axis=0)
        sc = jnp.where(cmp_valid, _dot(kc_ref[0], qp), NEG)
        m_c = jnp.max(sc, axis=0, keepdims=True)
        e_c = jnp.where(cmp_valid, jnp.exp2(sc - m_c), 0.0)
        inv_l = 1.0 / jnp.maximum(jnp.sum(e_c, axis=0, keepdims=True), 1e-30)
        p_c = e_c * inv_l
        oc_s[h] = _dot(vcT_ref[0, h * NSA_HEAD_DIM:(h + 1) * NSA_HEAD_DIM, :], _bf(p_c))
        psum = (p_c[:, 0:Q_BLOCK] + p_c[:, Q_BLOCK:2 * Q_BLOCK]
                + p_c[:, 2 * Q_BLOCK:3 * Q_BLOCK] + p_c[:, 3 * Q_BLOCK:4 * Q_BLOCK])
        p_hi = _bf(psum)
        p_lo = _bf(psum - p_hi.astype(F32))
        imp_t = _dot(smapT_ref[...], p_hi) + _dot(smapT_ref[...], p_lo)
        score = jnp.where(future, -BIG, jnp.where(forced > 0.5, BIG, imp_t))
        sel = _topk_mask(score, jrow, min(SEL_TOP_N, n_selblk))
        neg = jnp.where(sel > 0.5, 0.0, NEG)
        neg = jnp.concatenate([neg] * NSA_GQA, axis=1).reshape(n_slab, BLOCKS_PER_SLAB, nq)
        neg_s[h] = _bf(jnp.concatenate([neg, jnp.zeros_like(neg)], axis=1))
        qr_h = q_cat(qrT_ref, h)
        rhs_top = jnp.concatenate([qr_h, zero_q] if h % 2 == 0 else [zero_q, qr_h], axis=0)
        for qc in range(N_QCHUNK):
            cols = slice(qc * QCHUNK, (qc + 1) * QCHUNK)
            qr_s[h, qc, 0:2 * NSA_HEAD_DIM, :] = rhs_top[:, cols]
            qr_s[h, qc, 2 * NSA_HEAD_DIM:, :] = jnp.zeros((2 * NSA_HEAD_DIM, QCHUNK), BF16)
            for br in range(2):
                m_s[br, h, qc] = jnp.full((8, QCHUNK), NEG, F32)
                acc_s[br, h, qc] = jnp.zeros((acc_rows, QCHUNK), F32)

    ones16 = jnp.ones((16, Q_BLOCK), BF16)
    qpos_c = qpos[:, 0:QCHUNK]
    blk_row = lax.broadcasted_iota(jnp.int32, (Q_BLOCK, 1), 0)

    def tile_update(br, h, qc, st, vt_ref, kc):
        vt = jnp.concatenate([vt_ref[0, kc, h * NSA_HEAD_DIM:(h + 1) * NSA_HEAD_DIM, :], ones16], axis=0)
        mn, acc = _online_step(st, vt, m_s[br, h, qc, 0:1, :], acc_s[br, h, qc])
        m_s[br, h, qc] = jnp.broadcast_to(mn, (8, QCHUNK))
        acc_s[br, h, qc] = acc

    def sel_scores(kc, h, qc):
        slab = lax.shift_right_logical(kc, CHUNK_SHIFT)
        qr_s[h, qc, 2 * NSA_HEAD_DIM:2 * NSA_HEAD_DIM + 16, :] = neg_s[h, slab, :, qc * QCHUNK:(qc + 1) * QCHUNK]
        pair = h // 2
        lhs = jnp.concatenate([ks_ref[0, kc, :, pair * LANES:(pair + 1) * LANES],
                               econst_ref[kc & (CHUNKS_PER_SLAB - 1)]], axis=1)
        return _dot(lhs, qr_s[h, qc])

    tiles = [(h, qc) for h in range(NSA_KV_HEADS) for qc in range(N_QCHUNK)]

    def run_pipelined(items):
        pending = []
        for score_fn, update_fn in items:
            pending.append((update_fn, score_fn()))
            if len(pending) > MXU_LOOKAHEAD:
                fn, st = pending.pop(0)
                fn(st)
        for fn, st in pending:
            fn(st)

    def sel_items(kc, masked):
        def score(h, qc):
            st = sel_scores(kc, h, qc)
            return jnp.where((kc * Q_BLOCK + blk_row) <= qpos_c, st, NEG) if masked else st
        return [(functools.partial(score, h, qc),
                 functools.partial(lambda st, h, qc: tile_update(0, h, qc, st, vsT_ref, kc), h=h, qc=qc))
                for h, qc in tiles]

    n_full = lax.shift_right_logical(qb, CHUNK_SHIFT)

    def slab_body(s, carry):
        items = []
        for c in range(CHUNKS_PER_SLAB):
            items += sel_items(s * CHUNKS_PER_SLAB + c, False)
        run_pipelined(items)
        return carry

    lax.fori_loop(0, n_full, slab_body, 0)

    causal = (qb * Q_BLOCK + blk_row) <= qpos_c
    n_wblk = WINDOW // Q_BLOCK + 1

    def win_scores(i, h, qc):
        kb = qb - (n_wblk - 1) + i
        st = _dot(kw_ref[0, jnp.maximum(kb, 0), :, (h // 2) * LANES:(h // 2 + 1) * LANES],
                  qr_s[h, qc, 0:2 * NSA_HEAD_DIM, :])
        if i == 0:
            st = jnp.where((qpos_c - (kb * Q_BLOCK + blk_row)) <= WINDOW, st, NEG)
        if i == n_wblk - 1:
            return jnp.where(causal, st, NEG)
        return jnp.where(kb >= 0, st, NEG)

    items = []
    for c in range(CHUNKS_PER_SLAB):
        items += sel_items(n_full * CHUNKS_PER_SLAB + c, True)
    for i in range(n_wblk):
        kb_ld = jnp.maximum(qb - (n_wblk - 1) + i, 0)
        items += [(functools.partial(win_scores, i, h, qc),
                   functools.partial(lambda st, h, qc, kb_ld: tile_update(1, h, qc, st, vwT_ref, kb_ld),
                                     h=h, qc=qc, kb_ld=kb_ld))
                  for h, qc in tiles]
    run_pipelined(items)

    ot_parts = []
    for h in range(NSA_KV_HEADS):
        o_parts = []
        for qc in range(N_QCHUNK):
            outs = []
            for br in range(2):
                acc = acc_s[br, h, qc]
                outs.append(acc[0:NSA_HEAD_DIM, :] * (1.0 / acc[NSA_HEAD_DIM:NSA_HEAD_DIM + 1, :]))
            o_parts.append(outs)

        def gate_row(br):
            return jnp.concatenate([gT_ref[0, (h * NSA_GQA + g) * 3 + br:(h * NSA_GQA + g) * 3 + br + 1, :]
                                    for g in range(NSA_GQA)], axis=1)

        os_t = jnp.concatenate([o_parts[qc][0] for qc in range(N_QCHUNK)], axis=1)
        ow_t = jnp.concatenate([o_parts[qc][1] for qc in range(N_QCHUNK)], axis=1)
        o_h = gate_row(0) * oc_s[h] + gate_row(1) * os_t + gate_row(2) * ow_t
        for g in range(NSA_GQA):
            ot_parts.append(o_h[:, g * Q_BLOCK:(g + 1) * Q_BLOCK])
    o_t = _bf(jnp.concatenate(ot_parts, axis=0))
    o_ref[...] = _dot_tn(o_t, wn_ref[...])


def _selection_map_t(n_sel, n_cmp_rows):
    i = np.arange(n_cmp_rows)[None, :]
    j = np.arange(n_sel)[:, None]
    ov = (i * CMP_STRIDE < (j + 1) * SEL_BLOCK) & (i * CMP_STRIDE + CMP_LEN > j * SEL_BLOCK)
    return ov


def _nsa_prompt(qT, qrT, gT, kc, vcT, ks, vsT, kw, vwT, wn, bsz, t):
    nqb = t // Q_BLOCK
    n_sub = t // CMP_STRIDE
    n_cmp = n_sub - CMP_LEN // CMP_STRIDE + 1
    n_sel = t // SEL_BLOCK
    smap = _selection_map_t(n_sel, n_sub) & (np.arange(n_sub)[None, :] < n_cmp)
    smap_t = jnp.asarray(smap, dtype=BF16)
    blk_of_key = (np.arange(KEY_SLAB) // SEL_BLOCK).reshape(CHUNKS_PER_SLAB, Q_BLOCK, 1)
    econst = jnp.asarray(blk_of_key == np.arange(LANES)[None, None, :], dtype=BF16)
    nsl = t // KEY_SLAB
    nq = NSA_GQA * Q_BLOCK
    per_b3 = lambda shape: pl.BlockSpec(shape, lambda b, j: (b, 0, 0))
    per_b4 = lambda shape: pl.BlockSpec(shape, lambda b, j: (b, 0, 0, 0))
    return pl.pallas_call(
        _nsa_prompt_kernel,
        grid=(bsz, nqb),
        in_specs=[
            pl.BlockSpec((1, D_MODEL, Q_BLOCK), lambda b, j: (b, 0, j)),
            pl.BlockSpec((1, D_MODEL, Q_BLOCK), lambda b, j: (b, 0, j)),
            pl.BlockSpec((1, 3 * NSA_HEADS, Q_BLOCK), lambda b, j: (b, 0, j)),
            per_b3((1, n_sub, KV_HALF)), per_b3((1, KV_HALF, n_sub)),
            per_b4((1, nqb, Q_BLOCK, KV_HALF)), per_b4((1, nqb, KV_HALF, Q_BLOCK)),
            per_b4((1, nqb, Q_BLOCK, KV_HALF)), per_b4((1, nqb, KV_HALF, Q_BLOCK)),
            pl.BlockSpec((n_sel, n_sub), lambda b, j: (0, 0)),
            pl.BlockSpec((CHUNKS_PER_SLAB, Q_BLOCK, LANES), lambda b, j: (0, 0, 0)),
            pl.BlockSpec((D_MODEL, D_MODEL), lambda b, j: (0, 0)),
        ],
        out_specs=pl.BlockSpec((Q_BLOCK, D_MODEL), lambda b, j: (b * nqb + j, 0)),
        out_shape=jax.ShapeDtypeStruct((bsz * t, D_MODEL), F32),
        scratch_shapes=[pltpu.VMEM((NSA_KV_HEADS, N_QCHUNK, KV_HALF, QCHUNK), BF16),
                        pltpu.VMEM((NSA_KV_HEADS, nsl, 16, nq), BF16),
                        pltpu.VMEM((NSA_KV_HEADS, NSA_HEAD_DIM, nq), F32),
                        pltpu.VMEM((2, NSA_KV_HEADS, N_QCHUNK, 8, QCHUNK), F32),
                        pltpu.VMEM((2, NSA_KV_HEADS, N_QCHUNK, NSA_HEAD_DIM + 16, QCHUNK), F32)],
        compiler_params=_cparams(("parallel", "arbitrary")),
        name="nsa_prompt",
    )(qT, qrT, gT, kc, vcT, ks, vsT, kw, vwT, smap_t, econst, wn)


def _merge_kernel(x_ref, mg_ref, bs_ref, bn_ref, wo_ref, g_ref, b_ref, h_ref):
    mg = mg_ref[...]
    mix = jax.nn.sigmoid(mg[:, :D_MODEL]) * bs_ref[...] + jax.nn.sigmoid(mg[:, D_MODEL:]) * bn_ref[...]
    pre = DEEPNORM_ALPHA * x_ref[...] + _dot(_bf(mix), wo_ref[...])
    h_ref[...] = _layer_norm(pre, g_ref[...], b_ref[...])


def _merge(x, proj, b_ssm, b_nsa, wo, g, b, tm):
    m = x.shape[0]
    rowblk = lambda shape, c=0: pl.BlockSpec(shape, lambda i, c=c: (i, c))
    const = lambda shape: pl.BlockSpec(shape, lambda i: (0, 0))
    return pl.pallas_call(
        _merge_kernel,
        grid=(m // tm,),
        in_specs=[rowblk((tm, D_MODEL)), rowblk((tm, 2 * D_MODEL), COL_MG // (2 * D_MODEL)),
                  rowblk((tm, D_MODEL)), rowblk((tm, D_MODEL)),
                  const((D_MODEL, D_MODEL)), const((1, D_MODEL)), const((1, D_MODEL))],
        out_specs=rowblk((tm, D_MODEL)),
        out_shape=jax.ShapeDtypeStruct((m, D_MODEL), F32),
        compiler_params=_cparams(("parallel",)),
        name="merge_ln1",
    )(x, proj, b_ssm, b_nsa, wo, g, b)


def _ffn_kernel(h_ref, wg_ref, wu_ref, wd_ref, g_ref, b_ref, y_ref, acc_s):
    h = h_ref[...]
    hb = _bf(h)
    acc_s[...] = jnp.zeros_like(acc_s)

    def body(c, carry):
        gate = _dot(hb, wg_ref[c])
        up = _dot(hb, wu_ref[c])
        acc_s[...] += _dot(_bf(_silu(gate) * up), wd_ref[c])
        return carry

    lax.fori_loop(0, wg_ref.shape[0], body, 0)
    y_ref[...] = _layer_norm(DEEPNORM_ALPHA * h + acc_s[...], g_ref[...], b_ref[...])


def _ffn(h, wg3, wu3, wd3, g, b, tm):
    m = h.shape[0]
    nc = wg3.shape[0]
    const2 = lambda shape: pl.BlockSpec(shape, lambda i: (0, 0))
    const3 = lambda shape: pl.BlockSpec(shape, lambda i: (0, 0, 0))
    return pl.pallas_call(
        _ffn_kernel,
        grid=(m // tm,),
        in_specs=[pl.BlockSpec((tm, D_MODEL), lambda i: (i, 0)),
                  const3((nc, D_MODEL, FFN_CHUNK)), const3((nc, D_MODEL, FFN_CHUNK)), const3((nc, FFN_CHUNK, D_MODEL)),
                  const2((1, D_MODEL)), const2((1, D_MODEL))],
        out_specs=pl.BlockSpec((tm, D_MODEL), lambda i: (i, 0)),
        out_shape=jax.ShapeDtypeStruct((m, D_MODEL), F32),
        scratch_shapes=[pltpu.VMEM((tm, D_MODEL), F32)],
        compiler_params=_cparams(("parallel",)),
        name="ffn_ln2",
    )(h, wg3, wu3, wd3, g, b)


def _ssd_s_pre_kernel(xbc_ref, cst_ref, sm_ref, cw_ref, cb_ref, dtb_ref, alog_ref, e_ref,
                      xs_ref, xdt_ref, bm_ref, cm_ref, dec_ref, ncv_ref):
    xbc = xbc_ref[...]
    acc = cb_ref[...] + cw_ref[SSM_CONV_W - 1:SSM_CONV_W, :] * xbc
    for k in range(SSM_CONV_W - 1):
        acc = acc + cw_ref[k:k + 1, :] * cst_ref[k]
    act = _silu(acc)
    for k in range(SSM_CONV_W - 2):
        ncv_ref[k] = cst_ref[k + 1]
    ncv_ref[SSM_CONV_W - 2] = xbc
    lane = lax.broadcasted_iota(jnp.int32, (1, LANES), 1)
    a_full = jnp.where(lane < SSM_HEADS, -jnp.exp(alog_ref[...]), 0.0)
    dt = _softplus(sm_ref[...] + dtb_ref[...])
    dec_ref[...] = jnp.exp(dt * a_full)
    xs = act[:, :SSM_D_INNER]
    xs_ref[...] = xs
    xdt_ref[...] = xs * _dot_exact_lhs(dt, e_ref[...])
    bm_ref[...] = act[:, SSM_D_INNER:SSM_D_INNER + SSM_GROUPS * SSM_D_STATE]
    cm_ref[...] = act[:, SSM_D_INNER + SSM_GROUPS * SSM_D_STATE:]


def _ssd_s_pre(proj_s, cst, cw, cb, dtb_pad, alog_pad):
    n = proj_s.shape[0]
    e = np.zeros((LANES, SSM_D_INNER), np.float32)
    for h in range(SSM_HEADS):
        e[h, h * SSM_HEAD_DIM:(h + 1) * SSM_HEAD_DIM] = 1.0
    gw = SSM_GROUPS * SSM_D_STATE
    full = lambda shape: pl.BlockSpec(shape, lambda i: (0,) * len(shape))
    return pl.pallas_call(
        _ssd_s_pre_kernel,
        grid=(1,),
        in_specs=[pl.BlockSpec((n, SSM_CONV_DIM), lambda i: (0, COL_XBC // SSM_CONV_DIM)),
                  full(cst.shape),
                  pl.BlockSpec((n, LANES), lambda i: (0, COL_SMALL // LANES)),
                  full(cw.shape), full(cb.shape), full((1, LANES)), full((1, LANES)), full(e.shape)],
        out_specs=[full((n, SSM_D_INNER)), full((n, SSM_D_INNER)), full((n, gw)), full((n, gw)),
                   full((n, LANES)), full(cst.shape)],
        out_shape=[jax.ShapeDtypeStruct((n, SSM_D_INNER), F32), jax.ShapeDtypeStruct((n, SSM_D_INNER), F32),
                   jax.ShapeDtypeStruct((n, gw), F32), jax.ShapeDtypeStruct((n, gw), F32),
                   jax.ShapeDtypeStruct((n, LANES), F32), jax.ShapeDtypeStruct(cst.shape, F32)],
        compiler_params=_cparams(("arbitrary",)),
        name="ssd_sample_pre",
    )(proj_s, cst, proj_s, cw, cb, dtb_pad, alog_pad, jnp.asarray(e, dtype=BF16))


def _dyn_row(ref, b, cols=slice(None)):
    tile = ref[pl.ds(pl.multiple_of((b >> 3) << 3, 8), 8), cols]
    r = lax.broadcasted_iota(jnp.int32, (8, 1), 0)
    return jnp.sum(jnp.where(r == (b & 7), tile, 0.0), axis=0, keepdims=True)


def _onehot_cols(b, n):
    return jnp.where(lax.broadcasted_iota(jnp.int32, (n, n), 0) == b, 1.0, 0.0).astype(BF16)


def _ssd_s_state_kernel(st_ref, xdt_ref, bm_ref, cm_ref, dec_ref, nst_ref, yT_ref, xT_s, dT_s, cT_s):
    b = pl.program_id(0)
    n = xdt_ref.shape[0]

    @pl.when(b == 0)
    def _():
        for i, part in enumerate(_split3(xdt_ref[...].T)):
            xT_s[i] = part
        for i, part in enumerate(_split3(dec_ref[...].T)):
            dT_s[i] = part
        cT_s[...] = _bf(cm_ref[...].T)
        yT_ref[...] = jnp.zeros_like(yT_ref)

    hb = _onehot_cols(b, n)
    r = _dot(xT_s[0], hb) + _dot(xT_s[1], hb) + _dot(xT_s[2], hb)
    dec_r = _dot(dT_s[0], hb) + _dot(dT_s[1], hb) + _dot(dT_s[2], hb)
    c_r = _bf(_dot(cT_s[...], hb))
    lane_is_b = lax.broadcasted_iota(jnp.int32, (1, n), 1) == b
    hpg = SSM_HEADS // SSM_GROUPS
    brow_all = _dyn_row(bm_ref, b)
    for h in range(SSM_HEADS):
        g = h // hpg
        brow = brow_all[:, g * SSM_D_STATE:(g + 1) * SSM_D_STATE]
        rows = slice(h * SSM_HEAD_DIM, (h + 1) * SSM_HEAD_DIM)
        new = st_ref[0, h] * dec_r[h:h + 1, :] + r[rows, :] * brow
        nst_ref[0, h] = new
        y_h = _dot(_bf(new), c_r[g * SSM_D_STATE:(g + 1) * SSM_D_STATE, :])
        yT_ref[rows, :] = jnp.where(lane_is_b, y_h, yT_ref[rows, :])


def _ssd_s_state(state, xdt, bm, cm, dec):
    n = xdt.shape[0]
    assert n == LANES and SSM_D_STATE == LANES
    full = lambda a: pl.BlockSpec(a.shape, lambda b: (0, 0))
    blk = pl.BlockSpec((1, SSM_HEADS, SSM_HEAD_DIM, SSM_D_STATE), lambda b: (b, 0, 0, 0))
    return pl.pallas_call(
        _ssd_s_state_kernel,
        grid=(n,),
        in_specs=[blk, full(xdt), full(bm), full(cm), full(dec)],
        out_specs=[blk, pl.BlockSpec((SSM_D_INNER, n), lambda b: (0, 0))],
        out_shape=[jax.ShapeDtypeStruct(state.shape, F32), jax.ShapeDtypeStruct((SSM_D_INNER, n), F32)],
        scratch_shapes=[pltpu.VMEM((3, SSM_D_INNER, n), BF16), pltpu.VMEM((3, LANES, n), BF16),
                        pltpu.VMEM((SSM_GROUPS * SSM_D_STATE, n), BF16)],
        compiler_params=_cparams(("arbitrary",)),
        name="ssd_sample_state",
    )(state, xdt, bm, cm, dec)


def _ssd_s_post_kernel(yT_ref, xs_ref, z_ref, drow_ref, nw_ref, o_ref):
    y = yT_ref[...].T + drow_ref[...] * xs_ref[...]
    v = y * _silu(z_ref[...])
    gw = SSM_D_INNER // SSM_GROUPS
    outs = []
    for g in range(SSM_GROUPS):
        vg = v[:, g * gw:(g + 1) * gw]
        ms = jnp.sum(vg * vg, axis=-1, keepdims=True) * (1.0 / gw)
        outs.append(vg * lax.rsqrt(ms + NORM_EPS) * nw_ref[:, g * gw:(g + 1) * gw])
    o_ref[...] = _bf(jnp.concatenate(outs, axis=1))


def _ssd_s_post(yT, xs, proj_s, drow, nw):
    n = xs.shape[0]
    full = lambda shape: pl.BlockSpec(shape, lambda i: (0, 0))
    return pl.pallas_call(
        _ssd_s_post_kernel,
        grid=(1,),
        in_specs=[full(yT.shape), full(xs.shape),
                  pl.BlockSpec((n, SSM_D_INNER), lambda i: (0, COL_Z // SSM_D_INNER)),
                  full(drow.shape), full(nw.shape)],
        out_specs=full((n, SSM_D_INNER)),
        out_shape=jax.ShapeDtypeStruct((n, SSM_D_INNER), BF16),
        compiler_params=_cparams(("arbitrary",)),
        name="ssd_sample_post",
    )(yT, xs, proj_s, drow, nw)


PAGES_PER_STEP = 16


def _q_block(q_ref, b, n):
    r_q = _dot(q_ref[0], _onehot_cols(b, n))
    lane = lax.broadcasted_iota(jnp.int32, (1, n), 1)
    blocks = []
    for h in range(NSA_KV_HEADS):
        blk = None
        for g in range(NSA_GQA):
            hd = h * NSA_GQA + g
            piece = jnp.where(lane == hd, r_q[hd * NSA_HEAD_DIM:(hd + 1) * NSA_HEAD_DIM, :], 0.0)
            blk = piece if blk is None else blk + piece
        blocks.append(blk)
    return _bf(jnp.concatenate(blocks, axis=0))


def _scatter_heads(o_t, out_ref, b, n):
    lane = lax.broadcasted_iota(jnp.int32, (1, n), 1)
    lane_is_b = lane == b
    for hd in range(NSA_HEADS):
        h = hd // NSA_GQA
        piece = o_t[h * NSA_HEAD_DIM:(h + 1) * NSA_HEAD_DIM, :]
        col = jnp.sum(jnp.where(lane == hd, piece, 0.0), axis=1, keepdims=True)
        rows = slice(hd * NSA_HEAD_DIM, (hd + 1) * NSA_HEAD_DIM)
        out_ref[rows, :] = jnp.where(lane_is_b, col, out_ref[rows, :])


def _nsa_s_cmp_kernel(pt_ref, *refs):
    pages = refs[:PAGES_PER_STEP]
    (qT_ref, wcat_ref, w2bd_ref, pe_ref, w1f_ref, smapT_ref, gsum_ref,
     otc_ref, sel_ref, data_s, xcat_s, pab_s) = refs[PAGES_PER_STEP:]
    b = pl.program_id(0)
    j = pl.program_id(1)
    nj = pl.num_programs(1)
    n = qT_ref.shape[2]
    ns = CMP_CHUNK // CMP_STRIDE
    n_tiles = KV_COLS // LANES
    for i in range(PAGES_PER_STEP):
        for c in range(n_tiles):
            data_s[c, i * PAGE_SIZE:(i + 1) * PAGE_SIZE, :] = pages[i][0, c * LANES:(c + 1) * LANES, :].T
    parts = _compress_partial([data_s.at[c] for c in range(n_tiles)], wcat_ref, xcat_s)
    for v in range(2):
        pab_s[v, pl.ds(pl.multiple_of(j * ns, ns), ns), :] = parts[v]

    @pl.when((b == 0) & (j == 0))
    def _():
        otc_ref[...] = jnp.zeros_like(otc_ref)

    @pl.when(j == nj - 1)
    def _():
        kc, vc = _compress_finish(pab_s, _pe_bias(pe_ref, w1f_ref), w2bd_ref)
        n_sub = kc.shape[0]
        past_len = n_sub * CMP_STRIDE
        qblk = _q_block(qT_ref, b, n)
        crow = lax.broadcasted_iota(jnp.int32, (n_sub, 1), 0)
        valid = (crow * CMP_STRIDE + (CMP_LEN - 1)) <= past_len
        sc = jnp.where(valid, _dot(_bf(kc), qblk), NEG)
        m = jnp.max(sc, axis=0, keepdims=True)
        e = jnp.where(valid, jnp.exp2(sc - m), 0.0)
        p = e * (1.0 / jnp.maximum(jnp.sum(e, axis=0, keepdims=True), 1e-30))
        _scatter_heads(_dot(_bf(vc.T), _bf(p)), otc_ref, b, n)
        p_hi = _bf(p)
        p_lo = _bf(p - p_hi.astype(F32))
        psum = _dot(p_hi, gsum_ref[...]) + _dot(p_lo, gsum_ref[...])
        q_hi = _bf(psum)
        q_lo = _bf(psum - q_hi.astype(F32))
        imp_t = _dot(smapT_ref[...], q_hi) + _dot(smapT_ref[...], q_lo)
        nj_pad = imp_t.shape[0]
        n_sel = past_len // SEL_BLOCK + 1
        cur = float(past_len // SEL_BLOCK)
        jrow = lax.broadcasted_iota(jnp.int32, (nj_pad, n), 0).astype(F32)
        forced = (jnp.where(jrow == 0.0, 1.0, 0.0) + jnp.where(jrow == cur, 1.0, 0.0)
                  + jnp.where(jrow == cur - 1.0, 1.0, 0.0))
        score = jnp.where(jrow > cur, -BIG, jnp.where(forced > 0.5, BIG, imp_t))
        score = jnp.where(jrow >= float(n_sel), -jnp.inf, score)
        sel_ref[0] = _topk_mask(score, jrow, min(SEL_TOP_N, n_sel))


def _page_specs(n_pages_step):
    return [pl.BlockSpec((1, KV_COLS, PAGE_SIZE),
                         functools.partial(lambda b, j, pt, i: (pt[b, j * n_pages_step + i], 0, 0), i=i))
            for i in range(n_pages_step)]


def _nsa_s_cmp(page_table, cache_cmp, qT, wcat, w2bd, pe8, w1f4, past_len):
    nseq = page_table.shape[0]
    n_sub = past_len // CMP_STRIDE
    n_cmp = n_sub - CMP_LEN // CMP_STRIDE + 1
    n_sel = past_len // SEL_BLOCK + 1
    nj_pad = -(-n_sel // 8) * 8
    smap = np.zeros((nj_pad, n_sub), bool)
    smap[:n_sel] = _selection_map_t(n_sel, n_sub) & (np.arange(n_sub)[None, :] < n_cmp)
    lanes = np.arange(nseq)
    gsum = ((lanes[:, None] // NSA_GQA) == (lanes[None, :] // NSA_GQA)) & (lanes[:, None] < NSA_HEADS) & (lanes[None, :] < NSA_HEADS)
    nj = past_len // CMP_CHUNK
    ns = CMP_CHUNK // CMP_STRIDE
    c2 = lambda a: pl.BlockSpec(a.shape, lambda b, j, pt: (0, 0))
    c3 = lambda a: pl.BlockSpec(a.shape, lambda b, j, pt: (0, 0, 0))
    smap_j = jnp.asarray(smap, dtype=BF16)
    gsum_j = jnp.asarray(gsum, dtype=BF16)
    grid_spec = pltpu.PrefetchScalarGridSpec(
        num_scalar_prefetch=1,
        grid=(nseq, nj),
        in_specs=_page_specs(PAGES_PER_STEP) + [c3(qT), c3(wcat), c3(w2bd), c3(pe8), c3(w1f4), c2(smap_j), c2(gsum_j)],
        out_specs=[pl.BlockSpec((D_MODEL, nseq), lambda b, j, pt: (0, 0)),
                   pl.BlockSpec((1, nj_pad, nseq), lambda b, j, pt: (b, 0, 0))],
        scratch_shapes=[pltpu.VMEM((KV_COLS // LANES, CMP_CHUNK, LANES), F32),
                        pltpu.VMEM((2, ns, CMP_STRIDE * KV_HALF), BF16),
                        pltpu.VMEM((2, n_sub, KV_COLS), F32)],
    )
    return pl.pallas_call(
        _nsa_s_cmp_kernel,
        grid_spec=grid_spec,
        out_shape=[jax.ShapeDtypeStruct((D_MODEL, nseq), F32),
                   jax.ShapeDtypeStruct((nseq, nj_pad, nseq), F32)],
        compiler_params=_cparams(("arbitrary", "arbitrary")),
        name="nsa_sample_cmp",
    )(page_table, *([cache_cmp] * PAGES_PER_STEP), qT, wcat, w2bd, pe8, w1f4, smap_j, gsum_j)


def _nsa_s_att_kernel(n_past_blk, pt_ref, *refs):
    pages = refs[:PAGES_PER_STEP]
    (qrT_ref, sel_ref, nselT_ref, win_ref, nwinT_ref,
     ots_ref, otw_ref, nwo_ref, qblk_s, m_s, acc_s) = refs[PAGES_PER_STEP:]
    b = pl.program_id(0)
    j = pl.program_id(1)
    nj = pl.num_programs(1)
    n = qrT_ref.shape[2]
    ones16 = jnp.ones((16, PAGE_SIZE), BF16)
    blk_per_page = PAGE_SIZE // SEL_BLOCK

    @pl.when((b == 0) & (j == 0))
    def _():
        ots_ref[...] = jnp.zeros_like(ots_ref)
        otw_ref[...] = jnp.zeros_like(otw_ref)

    @pl.when(j == 0)
    def _():
        qblk_s[...] = _q_block(qrT_ref, b, n)
        m_s[...] = jnp.full(m_s.shape, NEG, F32)
        acc_s[...] = jnp.zeros_like(acc_s)

    qblk = qblk_s[...]

    def tile_update(kv_t, mask, m, acc):
        st = jnp.where(mask, _dot_tn(_bf(kv_t[:KV_HALF, :]), qblk), NEG)
        vt = jnp.concatenate([_bf(kv_t[KV_HALF:, :]), ones16], axis=0)
        return _online_step(st, vt, m, acc)

    lane_n = lax.broadcasted_iota(jnp.int32, (1, n), 1)
    key_is0 = lax.broadcasted_iota(jnp.int32, (PAGE_SIZE, 1), 0) == 0

    def new_token_tile(ref):
        col = jnp.sum(jnp.where(lane_n == b, ref[...], 0.0), axis=1, keepdims=True)
        return jnp.where(lax.broadcasted_iota(jnp.int32, (1, PAGE_SIZE), 1) == 0, col, 0.0), col

    m, acc = m_s[...], acc_s[...]
    for i in range(PAGES_PER_STEP):
        base = j * (PAGES_PER_STEP * blk_per_page) + (i * blk_per_page // 8) * 8
        rows8 = sel_ref[0, pl.ds(pl.multiple_of(base, 8), 8), :]
        r0 = (i * blk_per_page) % 8
        mask = jnp.concatenate([jnp.broadcast_to(rows8[r0 + r:r0 + r + 1, :], (SEL_BLOCK, n))
                                for r in range(blk_per_page)], axis=0) > 0.5
        m, acc = tile_update(pages[i][0], mask, m, acc)
    m_s[...] = m
    acc_s[...] = acc

    @pl.when(j == nj - 1)
    def _():
        kv_new, _ = new_token_tile(nselT_ref)
        sel_new = sel_ref[0, n_past_blk:n_past_blk + 1, :] > 0.5
        m2, acc2 = tile_update(kv_new, key_is0 & sel_new, m_s[...], acc_s[...])
        _scatter_heads(acc2[0:KV_HALF, :] / acc2[KV_HALF:KV_HALF + 1, :], ots_ref, b, n)
        win_t = win_ref[0]
        w_keys = win_t.shape[1]
        mw = jnp.full(m_s.shape, NEG, F32)
        aw = jnp.zeros_like(acc2)
        all_true = jnp.full((PAGE_SIZE, 1), True)
        for i in range(w_keys // PAGE_SIZE):
            mw, aw = tile_update(win_t[:, i * PAGE_SIZE:(i + 1) * PAGE_SIZE], all_true, mw, aw)
        kv_new, new_col = new_token_tile(nwinT_ref)
        mw, aw = tile_update(kv_new, key_is0, mw, aw)
        _scatter_heads(aw[0:KV_HALF, :] / aw[KV_HALF:KV_HALF + 1, :], otw_ref, b, n)
        lane_w = lax.broadcasted_iota(jnp.int32, (1, w_keys), 1)
        nwo_ref[0] = jnp.where(lane_w == w_keys - 1, new_col, pltpu.roll(win_t, w_keys - 1, 1))


def _nsa_s_att(page_table, cache_sel, qrT, sel, nsel_rows, cache_win, nwin_rows, past_len):
    nseq = page_table.shape[0]
    nj = past_len // (PAGES_PER_STEP * PAGE_SIZE)
    wbuf = cache_win.shape[2]
    c2 = lambda a: pl.BlockSpec(a.shape, lambda b, j, pt: (0, 0))
    c3 = lambda a: pl.BlockSpec(a.shape, lambda b, j, pt: (0, 0, 0))
    acc_rows = KV_HALF + 16
    grid_spec = pltpu.PrefetchScalarGridSpec(
        num_scalar_prefetch=1,
        grid=(nseq, nj),
        in_specs=_page_specs(PAGES_PER_STEP) + [
            c3(qrT),
            pl.BlockSpec((1,) + sel.shape[1:], lambda b, j, pt: (b, 0, 0)),
            c2(nsel_rows),
            pl.BlockSpec((1, KV_COLS, wbuf), lambda b, j, pt: (b, 0, 0)),
            c2(nwin_rows)],
        out_specs=[pl.BlockSpec((D_MODEL, nseq), lambda b, j, pt: (0, 0)),
                   pl.BlockSpec((D_MODEL, nseq), lambda b, j, pt: (0, 0)),
                   pl.BlockSpec((1, KV_COLS, wbuf), lambda b, j, pt: (b, 0, 0))],
        scratch_shapes=[pltpu.VMEM((KV_HALF, nseq), BF16), pltpu.VMEM((1, nseq), F32),
                        pltpu.VMEM((acc_rows, nseq), F32)],
    )
    return pl.pallas_call(
        functools.partial(_nsa_s_att_kernel, past_len // SEL_BLOCK),
        grid_spec=grid_spec,
        out_shape=[jax.ShapeDtypeStruct((D_MODEL, nseq), F32), jax.ShapeDtypeStruct((D_MODEL, nseq), F32),
                   jax.ShapeDtypeStruct((nseq, KV_COLS, wbuf), F32)],
        compiler_params=_cparams(("arbitrary", "arbitrary")),
        name="nsa_sample_att",
    )(page_table, *([cache_sel] * PAGES_PER_STEP), qrT, sel, nsel_rows, cache_win, nwin_rows)


def _nsa_s_out_kernel(otc_ref, ots_ref, otw_ref, gT_ref, wn_ref, o_ref):
    n = otc_ref.shape[1]
    parts = []
    for hd in range(NSA_HEADS):
        rows = slice(hd * NSA_HEAD_DIM, (hd + 1) * NSA_HEAD_DIM)
        parts.append(gT_ref[0, 3 * hd:3 * hd + 1, :] * otc_ref[rows, :]
                     + gT_ref[0, 3 * hd + 1:3 * hd + 2, :] * ots_ref[rows, :]
                     + gT_ref[0, 3 * hd + 2:3 * hd + 3, :] * otw_ref[rows, :])
    o_t = _bf(jnp.concatenate(parts, axis=0))
    o_ref[...] = _dot_tn(o_t, wn_ref[...])


def _nsa_s_out(otc, ots, otw, gT, wn):
    n = otc.shape[1]
    f2 = lambda a: pl.BlockSpec(a.shape, lambda i: (0, 0))
    f3 = lambda a: pl.BlockSpec(a.shape, lambda i: (0, 0, 0))
    return pl.pallas_call(
        _nsa_s_out_kernel,
        grid=(1,),
        in_specs=[f2(otc), f2(ots), f2(otw), f3(gT), f2(wn)],
        out_specs=pl.BlockSpec((n, D_MODEL), lambda i: (0, 0)),
        out_shape=jax.ShapeDtypeStruct((n, D_MODEL), F32),
        compiler_params=_cparams(("arbitrary",)),
        name="nsa_sample_out",
    )(otc, ots, otw, gT, wn)


def _rope_tables(pos):
    half = ROPE_DIM // 2
    inv_freq = jnp.power(ROPE_THETA, -jnp.arange(half, dtype=F32) * 2.0 / ROPE_DIM)
    ang = pos.astype(F32)[None, :] * inv_freq[:, None]
    return jnp.cos(ang), jnp.sin(ang)


def _permute_w_in(w_in):
    sizes = (SSM_D_INNER, SSM_CONV_DIM, SSM_HEADS, NSA_HEADS * NSA_HEAD_DIM, KV_COLS, KV_COLS, KV_COLS,
             3 * NSA_HEADS, 2 * D_MODEL)
    offs = np.concatenate([[0], np.cumsum(sizes)])
    z, xbc, dt, q, kvc, kvs, kvw, ng, mg = (w_in[:, offs[i]:offs[i + 1]] for i in range(len(sizes)))
    pad = jnp.zeros((D_MODEL, LANES - SSM_HEADS - 3 * NSA_HEADS), w_in.dtype)
    return _bf(jnp.concatenate([xbc, q, z, mg, kvc, kvs, kvw, dt, ng, pad], axis=1))


def _compress_weights(cmp_pe, cmp_w1, cmp_w2):
    eye = jnp.eye(NSA_KV_HEADS, dtype=F32)
    w1a = cmp_w1[:, :CMP_STRIDE]
    w1b = cmp_w1[:, CMP_STRIDE:]
    bd = lambda w: jnp.einsum('vlde,hk->vlhdke', w, eye).reshape(2, CMP_STRIDE * KV_HALF, KV_HALF)
    wcat = _bf(jnp.concatenate([bd(w1a), bd(w1b)], axis=2))
    w2bd = _bf(jnp.einsum('vef,hk->vhekf', cmp_w2, eye).reshape(2, KV_HALF, KV_HALF))
    pe8 = jnp.zeros((2, 8, CMP_LEN * NSA_HEAD_DIM), F32).at[:, 0, :].set(cmp_pe.reshape(2, -1))
    w1f4 = jnp.tile(cmp_w1.reshape(2, CMP_LEN * NSA_HEAD_DIM, NSA_HEAD_DIM), (1, 1, NSA_KV_HEADS))
    return wcat, w2bd, pe8, w1f4


def kernel(x_prompt, x_sample, cache_cmp_kv, cache_sel_kv, cache_win_kv, state_ssm, state_conv, page_table,
           w_in, conv_w, conv_b, dt_bias, a_log, d_skip, ssm_norm_w, w_ssm_out, cmp_pe, cmp_w1, cmp_w2,
           w_nsa_out, w_o, ln1_g, ln1_b, w_gate, w_up, w_down, ln2_g, ln2_b):
    bsz, t, _ = x_prompt.shape
    nseq, dec_seq, _ = x_sample.shape
    n_pool = cache_cmp_kv.shape[1]
    past_len = page_table.shape[1] * PAGE_SIZE
    assert w_in.shape[0] == 1 and dec_seq == 1 and nseq == LANES
    assert t % CMP_CHUNK == 0 and past_len % CMP_CHUNK == 0 and cache_win_kv.shape[2] == WINDOW

    w_in_p = _permute_w_in(w_in[0])
    pad_row = lambda v: jnp.zeros((1, LANES), F32).at[0, :SSM_HEADS].set(v)
    dtb_pad, alog_pad = pad_row(dt_bias[0]), pad_row(a_log[0])
    drow = jnp.repeat(d_skip[0], SSM_HEAD_DIM)[None, :]
    nw = ssm_norm_w[0][None, :]
    cw, cb = conv_w[0], conv_b[0][None, :]
    wcat, w2bd, pe8, w1f4 = _compress_weights(cmp_pe[0], cmp_w1[0], cmp_w2[0])
    w_ssm_b, w_nsa_b, w_o_b = _bf(w_ssm_out[0]), _bf(w_nsa_out[0]), _bf(w_o[0])
    nc = FFN_HIDDEN // FFN_CHUNK
    wg3 = _bf(w_gate[0]).reshape(D_MODEL, nc, FFN_CHUNK).transpose(1, 0, 2)
    wu3 = _bf(w_up[0]).reshape(D_MODEL, nc, FFN_CHUNK).transpose(1, 0, 2)
    wd3 = _bf(w_down[0]).reshape(nc, FFN_CHUNK, D_MODEL)
    g1, b1, g2, b2 = ln1_g[0][None, :], ln1_b[0][None, :], ln2_g[0][None, :], ln2_b[0][None, :]

    xp = x_prompt.reshape(bsz * t, D_MODEL)
    proj = _matmul(xp, w_in_p, 1024, PROJ_TN, "in_proj")
    y_ssm, new_ssm_p, new_conv_p = _ssd_prompt(proj, bsz, t, cw, cb, dtb_pad, alog_pad, drow, nw)
    b_ssm = _matmul(y_ssm, w_ssm_b, 1024, D_MODEL, "ssm_out_proj")
    cos_p, sin_p = _rope_tables(jnp.arange(t, dtype=jnp.int32))
    qT, qrT, ks, vsT, kw, vwT, gT, ncmp, nsel, nwin = _attn_prep(proj, bsz, t, KEY_SLAB, cos_p, sin_p)
    kc, vcT = _compress_prompt(proj, bsz, t, wcat, w2bd, pe8, w1f4)
    b_nsa = _nsa_prompt(qT, qrT, gT, kc, vcT, ks, vsT, kw, vwT, w_nsa_b, bsz, t)
    h = _merge(xp, proj, b_ssm, b_nsa, w_o_b, g1, b1, 512)
    y_p = _ffn(h, wg3, wu3, wd3, g2, b2, 512).reshape(bsz, t, D_MODEL)

    def kv6(a_t):
        n_b, _, n_t = a_t.shape
        return jnp.moveaxis(a_t.reshape(n_b, 2, NSA_KV_HEADS, NSA_HEAD_DIM, n_t), -1, 1)[None]

    w_keep = min(WINDOW, t)
    new_win_p = kv6(nwin[:, :, t - w_keep:])

    xs_in = x_sample.reshape(nseq, D_MODEL)
    proj_s = _matmul(xs_in, w_in_p, nseq, PROJ_TN, "in_proj_s")
    cst = jnp.moveaxis(state_conv[0], 1, 0)
    xs_s, xdt_s, bm_s, cm_s, dec_s, ncv_s = _ssd_s_pre(proj_s, cst, cw, cb, dtb_pad, alog_pad)
    new_ssm_s, yT_s = _ssd_s_state(state_ssm[0], xdt_s, bm_s, cm_s, dec_s)
    y_ssm_s = _ssd_s_post(yT_s, xs_s, proj_s, drow, nw)
    b_ssm_s = _matmul(y_ssm_s, w_ssm_b, nseq, D_MODEL, "ssm_out_proj_s")
    cos_s, sin_s = _rope_tables(jnp.full((nseq,), past_len, dtype=jnp.int32))
    qT_s, qrT_s, _, _, _, _, gT_s, ncmp_s, nsel_s, nwin_s = _attn_prep(proj_s, 1, nseq, nseq, cos_s, sin_s)
    fmaj = lambda c, n_lead: jnp.moveaxis(c, 1, -1).reshape(n_lead, KV_COLS, c.shape[1])
    cache_cmp = fmaj(cache_cmp_kv[0], n_pool)
    cache_sel = fmaj(cache_sel_kv[0], n_pool)
    cache_win = fmaj(cache_win_kv[0], nseq)
    otc, sel = _nsa_s_cmp(page_table, cache_cmp, qT_s, wcat, w2bd, pe8, w1f4, past_len)
    ots, otw, new_win_t = _nsa_s_att(page_table, cache_sel, qrT_s, sel, nsel_s[0], cache_win, nwin_s[0], past_len)
    kv6_s = lambda a_t: kv6(jnp.transpose(a_t, (2, 1, 0)))
    b_nsa_s = _nsa_s_out(otc, ots, otw, gT_s, w_nsa_b)
    h_s = _merge(xs_in, proj_s, b_ssm_s, b_nsa_s, w_o_b, g1, b1, nseq)
    y_s = _ffn(h_s, wg3, wu3, wd3, g2, b2, nseq).reshape(nseq, 1, D_MODEL)

    return (y_p, y_s,
            kv6(ncmp), kv6(nsel), new_win_p,
            new_ssm_p[None], new_conv_p[None],
            kv6_s(ncmp_s), kv6_s(nsel_s),
            kv6(new_win_t),
            new_ssm_s[None],
            jnp.moveaxis(ncv_s, 0, 1)[None])
```

```python
import functools
import math

import numpy as np
import jax
import jax.numpy as jnp
from jax import lax
from jax.experimental import pallas as pl
from jax.experimental.pallas import tpu as pltpu

F32 = jnp.float32
BF16 = jnp.bfloat16

D_MODEL = 1024
SSM_D_INNER = 2048
SSM_HEAD_DIM = 64
SSM_HEADS = 32
SSM_GROUPS = 4
SSM_D_STATE = 128
SSM_CONV_W = 4
SSM_CONV_DIM = 3072
SSM_CHUNK = 128
NSA_HEADS = 16
NSA_KV_HEADS = 4
NSA_HEAD_DIM = 64
NSA_GQA = 4
KV_COLS = 512
KV_HALF = 256
CMP_LEN = 32
CMP_STRIDE = 16
SEL_BLOCK = 64
SEL_TOP_N = 16
WINDOW = 512
Q_BLOCK = 128
ROPE_DIM = 16
ROPE_THETA = 500000.0
FFN_HIDDEN = 2816
NORM_EPS = 1e-5
BIG = 1e30
NEG = -1e30
DEPTH = 1
DEEPNORM_ALPHA = (2 * DEPTH) ** 0.25
PAGE_SIZE = 128

LANES = 128
VMEM_LIMIT_BYTES = 56 * 1024 * 1024

COL_XBC = 0
COL_Q = 3072
COL_Z = 4096
COL_MG = 6144
COL_KVC = 8192
COL_KVS = 8704
COL_KVW = 9216
COL_SMALL = 9728
PROJ_COLS = 9984
PROJ_TN = 768

Q_SCALE = NSA_HEAD_DIM ** -0.5 * math.log2(math.e)
KEY_SLAB = 512
BLOCKS_PER_SLAB = KEY_SLAB // SEL_BLOCK
CHUNKS_PER_SLAB = KEY_SLAB // Q_BLOCK
CHUNK_SHIFT = 2
assert 1 << CHUNK_SHIFT == CHUNKS_PER_SLAB
QCHUNK = 256
N_QCHUNK = NSA_GQA * Q_BLOCK // QCHUNK
MXU_LOOKAHEAD = 12
CMP_CHUNK = 2048
FFN_CHUNK = 256


def _dot_dims(a, b, dims):
    return lax.dot_general(a, b, (dims, ((), ())), preferred_element_type=F32)


def _dot(a, b):
    return _dot_dims(a, b, ((1,), (0,)))


def _dot_nt(a, b):
    return _dot_dims(a, b, ((1,), (1,)))


def _dot_tn(a, b):
    return _dot_dims(a, b, ((0,), (0,)))


def _bf(x):
    return x.astype(BF16)


def _split3(x):
    hi = _bf(x)
    r1 = x - hi.astype(F32)
    mid = _bf(r1)
    lo = _bf(r1 - mid.astype(F32))
    return hi, mid, lo


def _dot_exact_lhs(x, w_bf16):
    hi, mid, lo = _split3(x)
    return _dot(hi, w_bf16) + _dot(mid, w_bf16) + _dot(lo, w_bf16)


def _dot_exact_rhs(w_bf16, x):
    hi, mid, lo = _split3(x)
    return _dot(w_bf16, hi) + _dot(w_bf16, mid) + _dot(w_bf16, lo)


def _silu(x):
    h = 0.5 * x
    return h + h * jnp.tanh(h)


def _softplus(x):
    e = jnp.exp(-jnp.abs(x))
    u = 1.0 + e
    log1p_e = jnp.where(u == 1.0, e, jnp.log(u) * e / jnp.where(u == 1.0, 1.0, u - 1.0))
    return jnp.maximum(x, 0.0) + log1p_e


def _gelu_tanh(x):
    return 0.5 * x * (1.0 + jnp.tanh(0.7978845608028654 * (x + 0.044715 * (x * x * x))))


def _layer_norm(x, g, b):
    mu = jnp.mean(x, axis=-1, keepdims=True)
    xc = x - mu
    var = jnp.mean(xc * xc, axis=-1, keepdims=True)
    return xc * lax.rsqrt(var + NORM_EPS) * g + b


def _cparams(sem):
    return pltpu.CompilerParams(dimension_semantics=sem, vmem_limit_bytes=VMEM_LIMIT_BYTES)


def _mm_kernel(x_ref, w_ref, o_ref):
    o_ref[...] = _dot(_bf(x_ref[...]), w_ref[...])


def _matmul(x, w, tm, tn, name):
    m, k = x.shape
    n = w.shape[1]
    return pl.pallas_call(
        _mm_kernel,
        grid=(m // tm, n // tn),
        in_specs=[pl.BlockSpec((tm, k), lambda i, j: (i, 0)),
                  pl.BlockSpec((k, tn), lambda i, j: (0, j))],
        out_specs=pl.BlockSpec((tm, tn), lambda i, j: (i, j)),
        out_shape=jax.ShapeDtypeStruct((m, n), F32),
        compiler_params=_cparams(("parallel", "arbitrary")),
        name=name,
    )(x, w)


def _ssd_kernel(xbc_ref, z_ref, sm_ref, cw_ref, cb_ref, dtb_ref, alog_ref, drow_ref, nw_ref, r3_ref,
                y_ref, st_ref, cv_ref, xp_s, stT_s):
    j = pl.program_id(1)
    nj = pl.num_programs(1)
    q = SSM_CHUNK

    @pl.when(j == 0)
    def _():
        xp_s[0:8, :] = jnp.zeros((8, SSM_CONV_DIM), F32)
        stT_s[...] = jnp.zeros_like(stT_s)

    xp_s[8:8 + q, :] = xbc_ref[...]
    acc = cb_ref[...]
    for k in range(SSM_CONV_W):
        acc = acc + cw_ref[k:k + 1, :] * xp_s[5 + k:5 + k + q, :]
    act = _silu(acc)
    tail = xp_s[q + 5:q + 8, :]
    xp_s[5:8, :] = tail

    @pl.when(j == nj - 1)
    def _():
        cv_ref[0] = tail

    lane = lax.broadcasted_iota(jnp.int32, (1, LANES), 1)
    a_full = jnp.where(lane < SSM_HEADS, -jnp.exp(alog_ref[...]), 0.0)
    dt = _softplus(sm_ref[...] + dtb_ref[...])
    da = dt * a_full
    row_i = lax.broadcasted_iota(jnp.int32, (q, q), 0)
    col_i = lax.broadcasted_iota(jnp.int32, (q, q), 1)
    causal = col_i <= row_i
    tril = jnp.where(causal, 1.0, 0.0).astype(BF16)
    cum = _dot_exact_rhs(tril, da)
    cumT = cum.T
    dtT = dt.T
    hi, mid, lo = _split3(cum)
    packed = _bf(hi.astype(F32) + pltpu.roll(mid.astype(F32), 32, 1) + pltpu.roll(lo.astype(F32), 64, 1))
    lane_lo = lax.broadcasted_iota(jnp.int32, (1, LANES), 1) < SSM_HEAD_DIM

    heads_per_group = SSM_HEADS // SSM_GROUPS
    pairs_per_group = heads_per_group // 2
    n_pairs = SSM_HEADS // 2
    cmbs, cbs, bTs = [], [], []
    for g in range(SSM_GROUPS):
        bm_g = act[:, SSM_D_INNER + g * SSM_D_STATE:SSM_D_INNER + (g + 1) * SSM_D_STATE]
        cm_g = act[:, SSM_D_INNER + SSM_GROUPS * SSM_D_STATE + g * SSM_D_STATE:
                   SSM_D_INNER + SSM_GROUPS * SSM_D_STATE + (g + 1) * SSM_D_STATE]
        cmbs.append(_bf(cm_g))
        cbs.append(_dot_nt(cmbs[g], _bf(bm_g)))
        bTs.append(bm_g.T)
    cols_all = [_dot(packed, r3_ref[:, pair * 2 * LANES:(pair + 1) * 2 * LANES]) for pair in range(n_pairs)]
    y_inter = [_dot(cmbs[pair // pairs_per_group], _bf(stT_s[:, pair * LANES:(pair + 1) * LANES]))
               for pair in range(n_pairs)]

    y_parts = []
    for g in range(SSM_GROUPS):
        cmb, cb, bT = cmbs[g], cbs[g], bTs[g]
        for pp in range(pairs_per_group):
            pair = g * pairs_per_group + pp
            h0 = 2 * pair
            xs_pair = act[:, pair * LANES:(pair + 1) * LANES]
            xs_a = _bf(jnp.where(lane_lo, xs_pair, 0.0))
            xs_b = _bf(jnp.where(lane_lo, 0.0, xs_pair))
            cols2 = cols_all[pair]
            y_pair = None
            ds_pair = None
            lasts = []
            cols = []
            for hh, xs_m in ((0, xs_a), (1, xs_b)):
                h = h0 + hh
                col = cols2[:, hh * LANES:(hh + 1) * LANES]
                row = cumT[h:h + 1, :]
                dtrow = dtT[h:h + 1, :]
                dec = jnp.exp(jnp.where(causal, col - row, NEG))
                m_h = _bf(cb * dec * dtrow)
                y_h = _dot(m_h, xs_m)
                last = col[q - 1:q, :]
                wrow = jnp.exp(last - row) * dtrow
                ds_h = _dot(_bf(bT * wrow), xs_m)
                y_pair = y_h if y_pair is None else y_pair + y_h
                ds_pair = ds_h if ds_pair is None else ds_pair + ds_h
                lasts.append(last)
                cols.append(col)
            st_pair = stT_s[:, pair * LANES:(pair + 1) * LANES]
            scale_t = jnp.exp(jnp.where(lane_lo, cols[0], cols[1]))
            y_pair = y_pair + y_inter[pair] * scale_t
            stT_s[:, pair * LANES:(pair + 1) * LANES] = (
                st_pair * jnp.exp(jnp.where(lane_lo, lasts[0], lasts[1])) + ds_pair)
            y_pair = y_pair + drow_ref[:, pair * LANES:(pair + 1) * LANES] * xs_pair
            y_parts.append(y_pair)
    y = jnp.concatenate(y_parts, axis=1)
    v = y * _silu(z_ref[...])
    gw = SSM_D_INNER // SSM_GROUPS
    outs = []
    for g in range(SSM_GROUPS):
        vg = v[:, g * gw:(g + 1) * gw]
        ms = jnp.sum(vg * vg, axis=-1, keepdims=True) * (1.0 / gw)
        outs.append(vg * lax.rsqrt(ms + NORM_EPS) * nw_ref[:, g * gw:(g + 1) * gw])
    y_ref[...] = _bf(jnp.concatenate(outs, axis=1))

    @pl.when(j == nj - 1)
    def _():
        st_ref[0] = stT_s[...].T.reshape(SSM_HEADS, SSM_HEAD_DIM, SSM_D_STATE)


def _r3_table():
    k = np.arange(LANES)[:, None]
    c = np.arange(SSM_HEADS * LANES)[None, :]
    return jnp.asarray(((k % SSM_HEADS) == (c // LANES)) & (k < 3 * SSM_HEADS), dtype=BF16)


def _ssd_prompt(proj, bsz, t, cw, cb, dtb_pad, alog_pad, drow, nw):
    nch = t // SSM_CHUNK
    q = SSM_CHUNK
    row = lambda b, j: b * nch + j
    const = lambda shape: pl.BlockSpec(shape, lambda b, j: (0, 0))
    return pl.pallas_call(
        _ssd_kernel,
        grid=(bsz, nch),
        in_specs=[
            pl.BlockSpec((q, SSM_CONV_DIM), lambda b, j: (row(b, j), COL_XBC // SSM_CONV_DIM)),
            pl.BlockSpec((q, SSM_D_INNER), lambda b, j: (row(b, j), COL_Z // SSM_D_INNER)),
            pl.BlockSpec((q, LANES), lambda b, j: (row(b, j), COL_SMALL // LANES)),
            const((SSM_CONV_W, SSM_CONV_DIM)), const((1, SSM_CONV_DIM)),
            const((1, LANES)), const((1, LANES)), const((1, SSM_D_INNER)), const((1, SSM_D_INNER)),
            const((LANES, SSM_HEADS * LANES)),
        ],
        out_specs=[
            pl.BlockSpec((q, SSM_D_INNER), lambda b, j: (row(b, j), 0)),
            pl.BlockSpec((1, SSM_HEADS, SSM_HEAD_DIM, SSM_D_STATE), lambda b, j: (b, 0, 0, 0)),
            pl.BlockSpec((1, SSM_CONV_W - 1, SSM_CONV_DIM), lambda b, j: (b, 0, 0)),
        ],
        out_shape=[
            jax.ShapeDtypeStruct((bsz * t, SSM_D_INNER), BF16),
            jax.ShapeDtypeStruct((bsz, SSM_HEADS, SSM_HEAD_DIM, SSM_D_STATE), F32),
            jax.ShapeDtypeStruct((bsz, SSM_CONV_W - 1, SSM_CONV_DIM), F32),
        ],
        scratch_shapes=[pltpu.VMEM((q + 8, SSM_CONV_DIM), F32),
                        pltpu.VMEM((SSM_D_STATE, SSM_D_INNER), F32)],
        compiler_params=_cparams(("parallel", "arbitrary")),
        name="ssd_prompt",
    )(proj, proj, proj, cw, cb, dtb_pad, alog_pad, drow, nw, _r3_table())


def _rope_t(x_t, nh, c, s):
    n = x_t.shape[1]
    x3 = x_t.reshape(nh, NSA_HEAD_DIM, n)
    half = ROPE_DIM // 2
    x1 = x3[:, 0:half, :]
    x2 = x3[:, half:ROPE_DIM, :]
    r1 = x1 * c - x2 * s
    r2 = x2 * c + x1 * s
    return jnp.concatenate([r1, r2, x3[:, ROPE_DIM:, :]], axis=1).reshape(nh * NSA_HEAD_DIM, n)


def _prep_kernel(q_ref, kvc_ref, kvs_ref, kvw_ref, sm_ref, cos_ref, sin_ref,
                 qT_ref, qrT_ref, ks_ref, vsT_ref, kw_ref, vwT_ref, gT_ref, ncmp_ref, nsel_ref, nwin_ref):
    c = cos_ref[...]
    s = sin_ref[...]
    tt = q_ref.shape[0]
    q_t = q_ref[...].T
    qT_ref[0] = _bf(q_t * Q_SCALE)
    qrT_ref[0] = _bf(_rope_t(q_t, NSA_HEADS, c, s) * Q_SCALE)
    ncmp_ref[0] = kvc_ref[...].T
    for src, full_out, k_out, vt_out in ((kvs_ref, nsel_ref, ks_ref, vsT_ref), (kvw_ref, nwin_ref, kw_ref, vwT_ref)):
        kv = src[...]
        k_rot_t = _rope_t(kv[:, :KV_HALF].T, NSA_KV_HEADS, c, s)
        k_rot = k_rot_t.T
        v_t = kv[:, KV_HALF:].T
        full_out[0, :KV_HALF, :] = k_rot_t
        full_out[0, KV_HALF:, :] = v_t
        for i in range(tt // Q_BLOCK):
            k_out[0, i] = _bf(k_rot[i * Q_BLOCK:(i + 1) * Q_BLOCK, :])
            vt_out[0, i] = _bf(v_t[:, i * Q_BLOCK:(i + 1) * Q_BLOCK])
    g_t = jax.nn.sigmoid(sm_ref[...]).T
    gT_ref[0] = g_t[SSM_HEADS:SSM_HEADS + 3 * NSA_HEADS, :]


def _attn_prep(proj, bsz, t, tt, cos_t, sin_t):
    nt = t // tt
    nb = tt // Q_BLOCK
    row = lambda b, j: b * nt + j
    n_gate = 3 * NSA_HEADS
    return pl.pallas_call(
        _prep_kernel,
        grid=(bsz, nt),
        in_specs=[
            pl.BlockSpec((tt, D_MODEL), lambda b, j: (row(b, j), COL_Q // D_MODEL)),
            pl.BlockSpec((tt, KV_COLS), lambda b, j: (row(b, j), COL_KVC // KV_COLS)),
            pl.BlockSpec((tt, KV_COLS), lambda b, j: (row(b, j), COL_KVS // KV_COLS)),
            pl.BlockSpec((tt, KV_COLS), lambda b, j: (row(b, j), COL_KVW // KV_COLS)),
            pl.BlockSpec((tt, LANES), lambda b, j: (row(b, j), COL_SMALL // LANES)),
            pl.BlockSpec((ROPE_DIM // 2, tt), lambda b, j: (0, j)),
            pl.BlockSpec((ROPE_DIM // 2, tt), lambda b, j: (0, j)),
        ],
        out_specs=[
            pl.BlockSpec((1, D_MODEL, tt), lambda b, j: (b, 0, j)),
            pl.BlockSpec((1, D_MODEL, tt), lambda b, j: (b, 0, j)),
            pl.BlockSpec((1, nb, Q_BLOCK, KV_HALF), lambda b, j: (b, j, 0, 0)),
            pl.BlockSpec((1, nb, KV_HALF, Q_BLOCK), lambda b, j: (b, j, 0, 0)),
            pl.BlockSpec((1, nb, Q_BLOCK, KV_HALF), lambda b, j: (b, j, 0, 0)),
            pl.BlockSpec((1, nb, KV_HALF, Q_BLOCK), lambda b, j: (b, j, 0, 0)),
            pl.BlockSpec((1, n_gate, tt), lambda b, j: (b, 0, j)),
            pl.BlockSpec((1, KV_COLS, tt), lambda b, j: (b, 0, j)),
            pl.BlockSpec((1, KV_COLS, tt), lambda b, j: (b, 0, j)),
            pl.BlockSpec((1, KV_COLS, tt), lambda b, j: (b, 0, j)),
        ],
        out_shape=[
            jax.ShapeDtypeStruct((bsz, D_MODEL, t), BF16),
            jax.ShapeDtypeStruct((bsz, D_MODEL, t), BF16),
            jax.ShapeDtypeStruct((bsz, t // Q_BLOCK, Q_BLOCK, KV_HALF), BF16),
            jax.ShapeDtypeStruct((bsz, t // Q_BLOCK, KV_HALF, Q_BLOCK), BF16),
            jax.ShapeDtypeStruct((bsz, t // Q_BLOCK, Q_BLOCK, KV_HALF), BF16),
            jax.ShapeDtypeStruct((bsz, t // Q_BLOCK, KV_HALF, Q_BLOCK), BF16),
            jax.ShapeDtypeStruct((bsz, n_gate, t), F32),
            jax.ShapeDtypeStruct((bsz, KV_COLS, t), F32),
            jax.ShapeDtypeStruct((bsz, KV_COLS, t), F32),
            jax.ShapeDtypeStruct((bsz, KV_COLS, t), F32),
        ],
        compiler_params=_cparams(("parallel", "parallel")),
        name="attn_prep",
    )(proj, proj, proj, proj, proj, cos_t, sin_t)


def _compress_partial(data_refs, wcat_ref, xcat_s):
    ns = data_refs[0].shape[0] // CMP_STRIDE
    tiles_per_half = KV_HALF // LANES
    outs = []
    for lp in range(CMP_STRIDE):
        for c, ref in enumerate(data_refs):
            rows = ref[pl.ds(lp, ns, stride=CMP_STRIDE), :]
            v, cc = divmod(c, tiles_per_half)
            xcat_s[v, :, lp * KV_HALF + cc * LANES:lp * KV_HALF + (cc + 1) * LANES] = _bf(rows)
    for v in range(2):
        outs.append(_dot(xcat_s[v], wcat_ref[v]))
    return outs


def _compress_finish(pab_s, bias, w2bd_ref):
    n_sub = pab_s.shape[1]
    res = []
    for v in range(2):
        pab = pab_s[v]
        pre = pab[:, :KV_HALF] + pltpu.roll(pab[:, KV_HALF:], n_sub - 1, 0) + bias[v:v + 1, :]
        res.append(_dot(_bf(_gelu_tanh(pre)), w2bd_ref[v]))
    return res


def _pe_bias_kernel(pe_ref, w1f_ref, o_ref):
    rows = []
    for v in range(2):
        rows.append(jnp.dot(pe_ref[v], w1f_ref[v], preferred_element_type=F32,
                            precision=lax.Precision.HIGHEST)[0:1, :])
    o_ref[...] = jnp.concatenate(rows + [jnp.zeros((6, KV_HALF), F32)], axis=0)


def _pe_bias(pe8, w1f4):
    c3 = lambda a: pl.BlockSpec(a.shape, lambda i: (0, 0, 0))
    return pl.pallas_call(
        _pe_bias_kernel, grid=(1,), in_specs=[c3(pe8), c3(w1f4)],
        out_specs=pl.BlockSpec((8, KV_HALF), lambda i: (0, 0)),
        out_shape=jax.ShapeDtypeStruct((8, KV_HALF), F32),
        compiler_params=_cparams(("arbitrary",)), name="cmp_pe_bias",
    )(pe8, w1f4)


def _compress_prompt_kernel(d0_ref, d1_ref, d2_ref, d3_ref, wcat_ref, w2bd_ref, bias_ref,
                            kc_ref, vcT_ref, xcat_s, pab_s):
    j = pl.program_id(1)
    nj = pl.num_programs(1)
    ns = CMP_CHUNK // CMP_STRIDE
    parts = _compress_partial((d0_ref, d1_ref, d2_ref, d3_ref), wcat_ref, xcat_s)
    for v in range(2):
        pab_s[v, pl.ds(pl.multiple_of(j * ns, ns), ns), :] = parts[v]

    @pl.when(j == nj - 1)
    def _():
        kc, vc = _compress_finish(pab_s, bias_ref[...], w2bd_ref)
        kc_ref[0] = _bf(kc)
        vcT_ref[0] = _bf(vc.T)


def _compress_prompt(proj, bsz, t, wcat, w2bd, bias):
    nj = t // CMP_CHUNK
    n_sub = t // CMP_STRIDE
    ns = CMP_CHUNK // CMP_STRIDE
    c3 = lambda shape: pl.BlockSpec(shape, lambda b, j: (0, 0, 0))
    return pl.pallas_call(
        _compress_prompt_kernel,
        grid=(bsz, nj),
        in_specs=[pl.BlockSpec((CMP_CHUNK, LANES),
                               functools.partial(lambda b, j, c: (b * nj + j, COL_KVC // LANES + c), c=c))
                  for c in range(KV_COLS // LANES)]
        + [c3(wcat.shape), c3(w2bd.shape), pl.BlockSpec(bias.shape, lambda b, j: (0, 0))],
        out_specs=[pl.BlockSpec((1, n_sub, KV_HALF), lambda b, j: (b, 0, 0)),
                   pl.BlockSpec((1, KV_HALF, n_sub), lambda b, j: (b, 0, 0))],
        out_shape=[jax.ShapeDtypeStruct((bsz, n_sub, KV_HALF), BF16),
                   jax.ShapeDtypeStruct((bsz, KV_HALF, n_sub), BF16)],
        scratch_shapes=[pltpu.VMEM((2, ns, CMP_STRIDE * KV_HALF), BF16),
                        pltpu.VMEM((2, n_sub, KV_COLS), F32)],
        compiler_params=_cparams(("parallel", "arbitrary")),
        name="compress_prompt",
    )(proj, proj, proj, proj, wcat, w2bd, bias)


def _topk_mask(score, jrow, n_pick):
    sel = jnp.zeros_like(score)
    for _ in range(n_pick):
        mx = jnp.max(score, axis=0, keepdims=True)
        idx = jnp.min(jnp.where(score == mx, jrow, 1e9), axis=0, keepdims=True)
        chosen = jrow == idx
        sel = jnp.where(chosen, 1.0, sel)
        score = jnp.where(chosen, -jnp.inf, score)
    return sel


def _online_step(st, vt_aug, m, acc):
    mn = jnp.maximum(m, jnp.max(st, axis=0, keepdims=True))
    p = jnp.exp2(st - mn)
    acc = acc * jnp.exp2(m - mn) + _dot(vt_aug, _bf(p))
    return mn, acc


def _nsa_prompt_kernel(qT_ref, qrT_ref, gT_ref, kc_ref, vcT_ref, ks_ref, vsT_ref, kw_ref, vwT_ref, smapT_ref,
                       econst_ref, wn_ref, o_ref, qr_s, neg_s, oc_s, m_s, acc_s):
    qb = pl.program_id(1)
    nq = NSA_GQA * Q_BLOCK
    n_cmp_rows = kc_ref.shape[1]
    n_selblk = smapT_ref.shape[0]
    n_slab = n_selblk // BLOCKS_PER_SLAB
    lane_q = lax.broadcasted_iota(jnp.int32, (1, nq), 1) % Q_BLOCK
    qpos = qb * Q_BLOCK + lane_q
    zero_q = jnp.zeros((NSA_HEAD_DIM, nq), BF16)
    acc_rows = NSA_HEAD_DIM + 16

    jrow = lax.broadcasted_iota(jnp.int32, (n_selblk, Q_BLOCK), 0).astype(F32)
    cur = (qb * (Q_BLOCK // SEL_BLOCK)
           + lax.broadcasted_iota(jnp.int32, (1, Q_BLOCK), 1) // SEL_BLOCK).astype(F32)
    future = jrow > cur
    forced = jnp.where(jrow == 0.0, 1.0, 0.0) + jnp.where(jrow == cur, 1.0, 0.0) + jnp.where(jrow == cur - 1.0, 1.0, 0.0)
    crow = lax.broadcasted_iota(jnp.int32, (n_cmp_rows, 1), 0)
    cmp_valid = (crow * CMP_STRIDE + (CMP_LEN - 1)) <= qpos

    def q_cat(ref, h):
        return jnp.concatenate([ref[0, (h * NSA_GQA + g) * NSA_HEAD_DIM:(h * NSA_GQA + g + 1) * NSA_HEAD_DIM, :]
                                for g in range(NSA_GQA)], axis=1)

    for h in range(NSA_KV_HEADS):
        qp = jnp.concatenate([q_cat(qT_ref, h) if hh == h else zero_q for hh in range(NSA_KV_HEADS)], axis=0)
        sc = jnp.where(cmp_valid, _dot(kc_ref[0], qp), NEG)
        m_c = jnp.max(sc, axis=0, keepdims=True)
        e_c = jnp.where(cmp_valid, jnp.exp2(sc - m_c), 0.0)
        inv_l = 1.0 / jnp.maximum(jnp.sum(e_c, axis=0, keepdims=True), 1e-30)
        p_c = e_c * inv_l
        oc_s[h] = _dot(vcT_ref[0, h * NSA_HEAD_DIM:(h + 1) * NSA_HEAD_DIM, :], _bf(p_c))
        psum = (p_c[:, 0:Q_BLOCK] + p_c[:, Q_BLOCK:2 * Q_BLOCK]
                + p_c[:, 2 * Q_BLOCK:3 * Q_BLOCK] + p_c[:, 3 * Q_BLOCK:4 * Q_BLOCK])
        p_hi = _bf(psum)
        p_lo = _bf(psum - p_hi.astype(F32))
        imp_t = _dot(smapT_ref[...], p_hi) + _dot(smapT_ref[...], p_lo)
        score = jnp.where(future, -BIG, jnp.where(forced > 0.5, BIG, imp_t))
        sel = _topk_mask(score, jrow, min(SEL_TOP_N, n_selblk))
        neg = jnp.where(sel > 0.5, 0.0, NEG)
        neg = jnp.concatenate([neg] * NSA_GQA, axis=1).reshape(n_slab, BLOCKS_PER_SLAB, nq)
        neg_s[h] = _bf(jnp.concatenate([neg, jnp.zeros_like(neg)], axis=1))
        qr_h = q_cat(qrT_ref, h)
        rhs_top = jnp.concatenate([qr_h, zero_q] if h % 2 == 0 else [zero_q, qr_h], axis=0)
        for qc in range(N_QCHUNK):
            cols = slice(qc * QCHUNK, (qc + 1) * QCHUNK)
            qr_s[h, qc, 0:2 * NSA_HEAD_DIM, :] = rhs_top[:, cols]
            qr_s[h, qc, 2 * NSA_HEAD_DIM:, :] = jnp.zeros((2 * NSA_HEAD_DIM, QCHUNK), BF16)
            for br in range(2):
                m_s[br, h, qc] = jnp.full((8, QCHUNK), NEG, F32)
                acc_s[br, h, qc] = jnp.zeros((acc_rows, QCHUNK), F32)

    ones16 = jnp.ones((16, Q_BLOCK), BF16)
    qpos_c = qpos[:, 0:QCHUNK]
    blk_row = lax.broadcasted_iota(jnp.int32, (Q_BLOCK, 1), 0)

    def tile_update(br, h, qc, st, vt_ref, kc):
        vt = jnp.concatenate([vt_ref[0, kc, h * NSA_HEAD_DIM:(h + 1) * NSA_HEAD_DIM, :], ones16], axis=0)
        mn, acc = _online_step(st, vt, m_s[br, h, qc, 0:1, :], acc_s[br, h, qc])
        m_s[br, h, qc] = jnp.broadcast_to(mn, (8, QCHUNK))
        acc_s[br, h, qc] = acc

    def sel_scores(kc, h, qc):
        slab = lax.shift_right_logical(kc, CHUNK_SHIFT)
        qr_s[h, qc, 2 * NSA_HEAD_DIM:2 * NSA_HEAD_DIM + 16, :] = neg_s[h, slab, :, qc * QCHUNK:(qc + 1) * QCHUNK]
        pair = h // 2
        lhs = jnp.concatenate([ks_ref[0, kc, :, pair * LANES:(pair + 1) * LANES],
                               econst_ref[kc & (CHUNKS_PER_SLAB - 1)]], axis=1)
        return _dot(lhs, qr_s[h, qc])

    tiles = [(h, qc) for h in range(NSA_KV_HEADS) for qc in range(N_QCHUNK)]

    def run_pipelined(items):
        pending = []
        for score_fn, update_fn in items:
            pending.append((update_fn, score_fn()))
            if len(pending) > MXU_LOOKAHEAD:
                fn, st = pending.pop(0)
                fn(st)
        for fn, st in pending:
            fn(st)

    def sel_items(kc, masked):
        def score(h, qc):
            st = sel_scores(kc, h, qc)
            return jnp.where((kc * Q_BLOCK + blk_row) <= qpos_c, st, NEG) if masked else st
        return [(functools.partial(score, h, qc),
                 functools.partial(lambda st, h, qc: tile_update(0, h, qc, st, vsT_ref, kc), h=h, qc=qc))
                for h, qc in tiles]

    n_full = lax.shift_right_logical(qb, CHUNK_SHIFT)

    def slab_body(s, carry):
        items = []
        for c in range(CHUNKS_PER_SLAB):
            items += sel_items(s * CHUNKS_PER_SLAB + c, False)
        run_pipelined(items)
        return carry

    lax.fori_loop(0, n_full, slab_body, 0)

    causal = (qb * Q_BLOCK + blk_row) <= qpos_c
    n_wblk = WINDOW // Q_BLOCK + 1

    def win_scores(i, h, qc):
        kb = qb - (n_wblk - 1) + i
        st = _dot(kw_ref[0, jnp.maximum(kb, 0), :, (h // 2) * LANES:(h // 2 + 1) * LANES],
                  qr_s[h, qc, 0:2 * NSA_HEAD_DIM, :])
        if i == 0:
            st = jnp.where((qpos_c - (kb * Q_BLOCK + blk_row)) <= WINDOW, st, NEG)
        if i == n_wblk - 1:
            return jnp.where(causal, st, NEG)
        return jnp.where(kb >= 0, st, NEG)

    items = []
    for c in range(CHUNKS_PER_SLAB):
        items += sel_items(n_full * CHUNKS_PER_SLAB + c, True)
    for i in range(n_wblk):
        kb_ld = jnp.maximum(qb - (n_wblk - 1) + i, 0)
        items += [(functools.partial(win_scores, i, h, qc),
                   functools.partial(lambda st, h, qc, kb_ld: tile_update(1, h, qc, st, vwT_ref, kb_ld),
                                     h=h, qc=qc, kb_ld=kb_ld))
                  for h, qc in tiles]
    run_pipelined(items)

    ot_parts = []
    for h in range(NSA_KV_HEADS):
        o_parts = []
        for qc in range(N_QCHUNK):
            outs = []
            for br in range(2):
                acc = acc_s[br, h, qc]
                outs.append(acc[0:NSA_HEAD_DIM, :] * (1.0 / acc[NSA_HEAD_DIM:NSA_HEAD_DIM + 1, :]))
            o_parts.append(outs)

        def gate_row(br):
            return jnp.concatenate([gT_ref[0, (h * NSA_GQA + g) * 3 + br:(h * NSA_GQA + g) * 3 + br + 1, :]
                                    for g in range(NSA_GQA)], axis=1)

        os_t = jnp.concatenate([o_parts[qc][0] for qc in range(N_QCHUNK)], axis=1)
        ow_t = jnp.concatenate([o_parts[qc][1] for qc in range(N_QCHUNK)], axis=1)
        o_h = gate_row(0) * oc_s[h] + gate_row(1) * os_t + gate_row(2) * ow_t
        for g in range(NSA_GQA):
            ot_parts.append(o_h[:, g * Q_BLOCK:(g + 1) * Q_BLOCK])
    o_t = _bf(jnp.concatenate(ot_parts, axis=0))
    o_ref[...] = _dot_tn(o_t, wn_ref[...])


def _selection_map_t(n_sel, n_cmp_rows):
    i = np.arange(n_cmp_rows)[None, :]
    j = np.arange(n_sel)[:, None]
    ov = (i * CMP_STRIDE < (j + 1) * SEL_BLOCK) & (i * CMP_STRIDE + CMP_LEN > j * SEL_BLOCK)
    return ov


def _nsa_prompt(qT, qrT, gT, kc, vcT, ks, vsT, kw, vwT, wn, bsz, t):
    nqb = t // Q_BLOCK
    n_sub = t // CMP_STRIDE
    n_cmp = n_sub - CMP_LEN // CMP_STRIDE + 1
    n_sel = t // SEL_BLOCK
    smap = _selection_map_t(n_sel, n_sub) & (np.arange(n_sub)[None, :] < n_cmp)
    smap_t = jnp.asarray(smap, dtype=BF16)
    blk_of_key = (np.arange(KEY_SLAB) // SEL_BLOCK).reshape(CHUNKS_PER_SLAB, Q_BLOCK, 1)
    econst = jnp.asarray(blk_of_key == np.arange(LANES)[None, None, :], dtype=BF16)
    nsl = t // KEY_SLAB
    nq = NSA_GQA * Q_BLOCK
    per_b3 = lambda shape: pl.BlockSpec(shape, lambda b, j: (b, 0, 0))
    per_b4 = lambda shape: pl.BlockSpec(shape, lambda b, j: (b, 0, 0, 0))
    return pl.pallas_call(
        _nsa_prompt_kernel,
        grid=(bsz, nqb),
        in_specs=[
            pl.BlockSpec((1, D_MODEL, Q_BLOCK), lambda b, j: (b, 0, j)),
            pl.BlockSpec((1, D_MODEL, Q_BLOCK), lambda b, j: (b, 0, j)),
            pl.BlockSpec((1, 3 * NSA_HEADS, Q_BLOCK), lambda b, j: (b, 0, j)),
            per_b3((1, n_sub, KV_HALF)), per_b3((1, KV_HALF, n_sub)),
            per_b4((1, nqb, Q_BLOCK, KV_HALF)), per_b4((1, nqb, KV_HALF, Q_BLOCK)),
            per_b4((1, nqb, Q_BLOCK, KV_HALF)), per_b4((1, nqb, KV_HALF, Q_BLOCK)),
            pl.BlockSpec((n_sel, n_sub), lambda b, j: (0, 0)),
            pl.BlockSpec((CHUNKS_PER_SLAB, Q_BLOCK, LANES), lambda b, j: (0, 0, 0)),
            pl.BlockSpec((D_MODEL, D_MODEL), lambda b, j: (0, 0)),
        ],
        out_specs=pl.BlockSpec((Q_BLOCK, D_MODEL), lambda b, j: (b * nqb + j, 0)),
        out_shape=jax.ShapeDtypeStruct((bsz * t, D_MODEL), F32),
        scratch_shapes=[pltpu.VMEM((NSA_KV_HEADS, N_QCHUNK, KV_HALF, QCHUNK), BF16),
                        pltpu.VMEM((NSA_KV_HEADS, nsl, 16, nq), BF16),
                        pltpu.VMEM((NSA_KV_HEADS, NSA_HEAD_DIM, nq), F32),
                        pltpu.VMEM((2, NSA_KV_HEADS, N_QCHUNK, 8, QCHUNK), F32),
                        pltpu.VMEM((2, NSA_KV_HEADS, N_QCHUNK, NSA_HEAD_DIM + 16, QCHUNK), F32)],
        compiler_params=_cparams(("parallel", "arbitrary")),
        name="nsa_prompt",
    )(qT, qrT, gT, kc, vcT, ks, vsT, kw, vwT, smap_t, econst, wn)


def _merge_kernel(x_ref, mg_ref, bs_ref, bn_ref, wo_ref, g_ref, b_ref, h_ref):
    mg = mg_ref[...]
    mix = jax.nn.sigmoid(mg[:, :D_MODEL]) * bs_ref[...] + jax.nn.sigmoid(mg[:, D_MODEL:]) * bn_ref[...]
    pre = DEEPNORM_ALPHA * x_ref[...] + _dot(_bf(mix), wo_ref[...])
    h_ref[...] = _layer_norm(pre, g_ref[...], b_ref[...])


def _merge(x, proj, b_ssm, b_nsa, wo, g, b, tm):
    m = x.shape[0]
    rowblk = lambda shape, c=0: pl.BlockSpec(shape, lambda i, c=c: (i, c))
    const = lambda shape: pl.BlockSpec(shape, lambda i: (0, 0))
    return pl.pallas_call(
        _merge_kernel,
        grid=(m // tm,),
        in_specs=[rowblk((tm, D_MODEL)), rowblk((tm, 2 * D_MODEL), COL_MG // (2 * D_MODEL)),
                  rowblk((tm, D_MODEL)), rowblk((tm, D_MODEL)),
                  const((D_MODEL, D_MODEL)), const((1, D_MODEL)), const((1, D_MODEL))],
        out_specs=rowblk((tm, D_MODEL)),
        out_shape=jax.ShapeDtypeStruct((m, D_MODEL), F32),
        compiler_params=_cparams(("parallel",)),
        name="merge_ln1",
    )(x, proj, b_ssm, b_nsa, wo, g, b)


def _ffn_kernel(h_ref, wg_ref, wu_ref, wd_ref, g_ref, b_ref, y_ref, acc_s):
    h = h_ref[...]
    hb = _bf(h)
    acc_s[...] = jnp.zeros_like(acc_s)

    def body(c, carry):
        gate = _dot(hb, wg_ref[c])
        up = _dot(hb, wu_ref[c])
        acc_s[...] += _dot(_bf(_silu(gate) * up), wd_ref[c])
        return carry

    lax.fori_loop(0, wg_ref.shape[0], body, 0)
    y_ref[...] = _layer_norm(DEEPNORM_ALPHA * h + acc_s[...], g_ref[...], b_ref[...])


def _ffn(h, wg3, wu3, wd3, g, b, tm):
    m = h.shape[0]
    nc = wg3.shape[0]
    const2 = lambda shape: pl.BlockSpec(shape, lambda i: (0, 0))
    const3 = lambda shape: pl.BlockSpec(shape, lambda i: (0, 0, 0))
    return pl.pallas_call(
        _ffn_kernel,
        grid=(m // tm,),
        in_specs=[pl.BlockSpec((tm, D_MODEL), lambda i: (i, 0)),
                  const3((nc, D_MODEL, FFN_CHUNK)), const3((nc, D_MODEL, FFN_CHUNK)), const3((nc, FFN_CHUNK, D_MODEL)),
                  const2((1, D_MODEL)), const2((1, D_MODEL))],
        out_specs=pl.BlockSpec((tm, D_MODEL), lambda i: (i, 0)),
        out_shape=jax.ShapeDtypeStruct((m, D_MODEL), F32),
        scratch_shapes=[pltpu.VMEM((tm, D_MODEL), F32)],
        compiler_params=_cparams(("parallel",)),
        name="ffn_ln2",
    )(h, wg3, wu3, wd3, g, b)


def _ssd_s_pre_kernel(xbc_ref, cst_ref, sm_ref, cw_ref, cb_ref, dtb_ref, alog_ref, e_ref,
                      xs_ref, xdt_ref, bm_ref, cm_ref, dec_ref, ncv_ref):
    xbc = xbc_ref[...]
    acc = cb_ref[...] + cw_ref[SSM_CONV_W - 1:SSM_CONV_W, :] * xbc
    for k in range(SSM_CONV_W - 1):
        acc = acc + cw_ref[k:k + 1, :] * cst_ref[k]
    act = _silu(acc)
    for k in range(SSM_CONV_W - 2):
        ncv_ref[k] = cst_ref[k + 1]
    ncv_ref[SSM_CONV_W - 2] = xbc
    lane = lax.broadcasted_iota(jnp.int32, (1, LANES), 1)
    a_full = jnp.where(lane < SSM_HEADS, -jnp.exp(alog_ref[...]), 0.0)
    dt = _softplus(sm_ref[...] + dtb_ref[...])
    dec_ref[...] = jnp.exp(dt * a_full)
    xs = act[:, :SSM_D_INNER]
    xs_ref[...] = xs
    xdt_ref[...] = xs * _dot_exact_lhs(dt, e_ref[...])
    bm_ref[...] = act[:, SSM_D_INNER:SSM_D_INNER + SSM_GROUPS * SSM_D_STATE]
    cm_ref[...] = act[:, SSM_D_INNER + SSM_GROUPS * SSM_D_STATE:]


def _ssd_s_pre(proj_s, cst, cw, cb, dtb_pad, alog_pad):
    n = proj_s.shape[0]
    e = np.zeros((LANES, SSM_D_INNER), np.float32)
    for h in range(SSM_HEADS):
        e[h, h * SSM_HEAD_DIM:(h + 1) * SSM_HEAD_DIM] = 1.0
    gw = SSM_GROUPS * SSM_D_STATE
    full = lambda shape: pl.BlockSpec(shape, lambda i: (0,) * len(shape))
    return pl.pallas_call(
        _ssd_s_pre_kernel,
        grid=(1,),
        in_specs=[pl.BlockSpec((n, SSM_CONV_DIM), lambda i: (0, COL_XBC // SSM_CONV_DIM)),
                  full(cst.shape),
                  pl.BlockSpec((n, LANES), lambda i: (0, COL_SMALL // LANES)),
                  full(cw.shape), full(cb.shape), full((1, LANES)), full((1, LANES)), full(e.shape)],
        out_specs=[full((n, SSM_D_INNER)), full((n, SSM_D_INNER)), full((n, gw)), full((n, gw)),
                   full((n, LANES)), full(cst.shape)],
        out_shape=[jax.ShapeDtypeStruct((n, SSM_D_INNER), F32), jax.ShapeDtypeStruct((n, SSM_D_INNER), F32),
                   jax.ShapeDtypeStruct((n, gw), F32), jax.ShapeDtypeStruct((n, gw), F32),
                   jax.ShapeDtypeStruct((n, LANES), F32), jax.ShapeDtypeStruct(cst.shape, F32)],
        compiler_params=_cparams(("arbitrary",)),
        name="ssd_sample_pre",
    )(proj_s, cst, proj_s, cw, cb, dtb_pad, alog_pad, jnp.asarray(e, dtype=BF16))


def _dyn_row(ref, b, cols=slice(None)):
    tile = ref[pl.ds(pl.multiple_of((b >> 3) << 3, 8), 8), cols]
    r = lax.broadcasted_iota(jnp.int32, (8, 1), 0)
    return jnp.sum(jnp.where(r == (b & 7), tile, 0.0), axis=0, keepdims=True)


def _onehot_cols(b, n):
    return jnp.where(lax.broadcasted_iota(jnp.int32, (n, n), 0) == b, 1.0, 0.0).astype(BF16)


def _ssd_s_state_kernel(st_ref, xdt_ref, bm_ref, cm_ref, dec_ref, nst_ref, yT_ref, xT_s, dT_s, cT_s):
    b = pl.program_id(0)
    n = xdt_ref.shape[0]

    @pl.when(b == 0)
    def _():
        for i, part in enumerate(_split3(xdt_ref[...].T)):
            xT_s[i] = part
        for i, part in enumerate(_split3(dec_ref[...].T)):
            dT_s[i] = part
        cT_s[...] = _bf(cm_ref[...].T)
        yT_ref[...] = jnp.zeros_like(yT_ref)

    hb = _onehot_cols(b, n)
    r = _dot(xT_s[0], hb) + _dot(xT_s[1], hb) + _dot(xT_s[2], hb)
    dec_r = _dot(dT_s[0], hb) + _dot(dT_s[1], hb) + _dot(dT_s[2], hb)
    c_r = _bf(_dot(cT_s[...], hb))
    lane_is_b = lax.broadcasted_iota(jnp.int32, (1, n), 1) == b
    hpg = SSM_HEADS // SSM_GROUPS
    brow_all = _dyn_row(bm_ref, b)
    for h in range(SSM_HEADS):
        g = h // hpg
        brow = brow_all[:, g * SSM_D_STATE:(g + 1) * SSM_D_STATE]
        rows = slice(h * SSM_HEAD_DIM, (h + 1) * SSM_HEAD_DIM)
        new = st_ref[0, h] * dec_r[h:h + 1, :] + r[rows, :] * brow
        nst_ref[0, h] = new
        y_h = _dot(_bf(new), c_r[g * SSM_D_STATE:(g + 1) * SSM_D_STATE, :])
        yT_ref[rows, :] = jnp.where(lane_is_b, y_h, yT_ref[rows, :])


def _ssd_s_state(state, xdt, bm, cm, dec):
    n = xdt.shape[0]
    assert n == LANES and SSM_D_STATE == LANES
    full = lambda a: pl.BlockSpec(a.shape, lambda b: (0, 0))
    blk = pl.BlockSpec((1, SSM_HEADS, SSM_HEAD_DIM, SSM_D_STATE), lambda b: (b, 0, 0, 0))
    return pl.pallas_call(
        _ssd_s_state_kernel,
        grid=(n,),
        in_specs=[blk, full(xdt), full(bm), full(cm), full(dec)],
        out_specs=[blk, pl.BlockSpec((SSM_D_INNER, n), lambda b: (0, 0))],
        out_shape=[jax.ShapeDtypeStruct(state.shape, F32), jax.ShapeDtypeStruct((SSM_D_INNER, n), F32)],
        scratch_shapes=[pltpu.VMEM((3, SSM_D_INNER, n), BF16), pltpu.VMEM((3, LANES, n), BF16),
                        pltpu.VMEM((SSM_GROUPS * SSM_D_STATE, n), BF16)],
        compiler_params=_cparams(("arbitrary",)),
        name="ssd_sample_state",
    )(state, xdt, bm, cm, dec)


def _ssd_s_post_kernel(yT_ref, xs_ref, z_ref, drow_ref, nw_ref, o_ref):
    y = yT_ref[...].T + drow_ref[...] * xs_ref[...]
    v = y * _silu(z_ref[...])
    gw = SSM_D_INNER // SSM_GROUPS
    outs = []
    for g in range(SSM_GROUPS):
        vg = v[:, g * gw:(g + 1) * gw]
        ms = jnp.sum(vg * vg, axis=-1, keepdims=True) * (1.0 / gw)
        outs.append(vg * lax.rsqrt(ms + NORM_EPS) * nw_ref[:, g * gw:(g + 1) * gw])
    o_ref[...] = _bf(jnp.concatenate(outs, axis=1))


def _ssd_s_post(yT, xs, proj_s, drow, nw):
    n = xs.shape[0]
    full = lambda shape: pl.BlockSpec(shape, lambda i: (0, 0))
    return pl.pallas_call(
        _ssd_s_post_kernel,
        grid=(1,),
        in_specs=[full(yT.shape), full(xs.shape),
                  pl.BlockSpec((n, SSM_D_INNER), lambda i: (0, COL_Z // SSM_D_INNER)),
                  full(drow.shape), full(nw.shape)],
        out_specs=full((n, SSM_D_INNER)),
        out_shape=jax.ShapeDtypeStruct((n, SSM_D_INNER), BF16),
        compiler_params=_cparams(("arbitrary",)),
        name="ssd_sample_post",
    )(yT, xs, proj_s, drow, nw)


PAGES_PER_STEP = 16


def _q_block(q_ref, b, n):
    r_q = _dot(q_ref[0], _onehot_cols(b, n))
    lane = lax.broadcasted_iota(jnp.int32, (1, n), 1)
    blocks = []
    for h in range(NSA_KV_HEADS):
        blk = None
        for g in range(NSA_GQA):
            hd = h * NSA_GQA + g
            piece = jnp.where(lane == hd, r_q[hd * NSA_HEAD_DIM:(hd + 1) * NSA_HEAD_DIM, :], 0.0)
            blk = piece if blk is None else blk + piece
        blocks.append(blk)
    return _bf(jnp.concatenate(blocks, axis=0))


def _scatter_heads(o_t, out_ref, b, n):
    lane = lax.broadcasted_iota(jnp.int32, (1, n), 1)
    lane_is_b = lane == b
    for hd in range(NSA_HEADS):
        h = hd // NSA_GQA
        piece = o_t[h * NSA_HEAD_DIM:(h + 1) * NSA_HEAD_DIM, :]
        col = jnp.sum(jnp.where(lane == hd, piece, 0.0), axis=1, keepdims=True)
        rows = slice(hd * NSA_HEAD_DIM, (hd + 1) * NSA_HEAD_DIM)
        out_ref[rows, :] = jnp.where(lane_is_b, col, out_ref[rows, :])


def _nsa_s_cmp_kernel(pt_ref, *refs):
    pages = refs[:PAGES_PER_STEP]
    (qT_ref, wcat_ref, w2bd_ref, bias_ref, perm_ref, smapT_ref, gsum_ref,
     otc_ref, sel_ref, xcat_s, pab_s) = refs[PAGES_PER_STEP:]
    b = pl.program_id(0)
    j = pl.program_id(1)
    nj = pl.num_programs(1)
    n = qT_ref.shape[2]
    ns = CMP_CHUNK // CMP_STRIDE
    sub_per_pair = 2 * PAGE_SIZE // CMP_STRIDE
    for i in range(PAGES_PER_STEP // 2):
        pair = _bf(jnp.concatenate([pages[2 * i][0], pages[2 * i + 1][0]], axis=1))
        t_perm = _bf(_dot_nt(perm_ref[...], pair))
        for lp in range(CMP_STRIDE):
            for v in range(2):
                xcat_s[v, i * sub_per_pair:(i + 1) * sub_per_pair, lp * KV_HALF:(lp + 1) * KV_HALF] = (
                    t_perm[lp * sub_per_pair:(lp + 1) * sub_per_pair, v * KV_HALF:(v + 1) * KV_HALF])
    for v in range(2):
        pab_s[v, pl.ds(pl.multiple_of(j * ns, ns), ns), :] = _dot(xcat_s[v], wcat_ref[v])

    @pl.when((b == 0) & (j == 0))
    def _():
        otc_ref[...] = jnp.zeros_like(otc_ref)

    @pl.when(j == nj - 1)
    def _():
        kc, vc = _compress_finish(pab_s, bias_ref[...], w2bd_ref)
        n_sub = kc.shape[0]
        past_len = n_sub * CMP_STRIDE
        qblk = _q_block(qT_ref, b, n)
        crow = lax.broadcasted_iota(jnp.int32, (n_sub, 1), 0)
        valid = (crow * CMP_STRIDE + (CMP_LEN - 1)) <= past_len
        sc = jnp.where(valid, _dot(_bf(kc), qblk), NEG)
        m = jnp.max(sc, axis=0, keepdims=True)
        e = jnp.where(valid, jnp.exp2(sc - m), 0.0)
        p = e * (1.0 / jnp.maximum(jnp.sum(e, axis=0, keepdims=True), 1e-30))
        _scatter_heads(_dot(_bf(vc.T), _bf(p)), otc_ref, b, n)
        p_hi = _bf(p)
        p_lo = _bf(p - p_hi.astype(F32))
        psum = _dot(p_hi, gsum_ref[...]) + _dot(p_lo, gsum_ref[...])
        q_hi = _bf(psum)
        q_lo = _bf(psum - q_hi.astype(F32))
        imp_t = _dot(smapT_ref[...], q_hi) + _dot(smapT_ref[...], q_lo)
        nj_pad = imp_t.shape[0]
        n_sel = past_len // SEL_BLOCK + 1
        cur = float(past_len // SEL_BLOCK)
        jrow = lax.broadcasted_iota(jnp.int32, (nj_pad, n), 0).astype(F32)
        forced = (jnp.where(jrow == 0.0, 1.0, 0.0) + jnp.where(jrow == cur, 1.0, 0.0)
                  + jnp.where(jrow == cur - 1.0, 1.0, 0.0))
        score = jnp.where(jrow > cur, -BIG, jnp.where(forced > 0.5, BIG, imp_t))
        score = jnp.where(jrow >= float(n_sel), -jnp.inf, score)
        sel_ref[0] = _topk_mask(score, jrow, min(SEL_TOP_N, n_sel))


def _page_specs(n_pages_step):
    return [pl.BlockSpec((1, KV_COLS, PAGE_SIZE),
                         functools.partial(lambda b, j, pt, i: (pt[b, j * n_pages_step + i], 0, 0), i=i))
            for i in range(n_pages_step)]


def _nsa_s_cmp(page_table, cache_cmp, qT, wcat, w2bd, bias, past_len):
    nseq = page_table.shape[0]
    pair_tok = 2 * PAGE_SIZE
    r = np.arange(pair_tok)
    perm = jnp.asarray(r[None, :] == ((r % (pair_tok // CMP_STRIDE)) * CMP_STRIDE + r // (pair_tok // CMP_STRIDE))[:, None],
                       dtype=BF16)
    n_sub = past_len // CMP_STRIDE
    n_cmp = n_sub - CMP_LEN // CMP_STRIDE + 1
    n_sel = past_len // SEL_BLOCK + 1
    nj_pad = -(-n_sel // 8) * 8
    smap = np.zeros((nj_pad, n_sub), bool)
    smap[:n_sel] = _selection_map_t(n_sel, n_sub) & (np.arange(n_sub)[None, :] < n_cmp)
    lanes = np.arange(nseq)
    gsum = ((lanes[:, None] // NSA_GQA) == (lanes[None, :] // NSA_GQA)) & (lanes[:, None] < NSA_HEADS) & (lanes[None, :] < NSA_HEADS)
    nj = past_len // CMP_CHUNK
    ns = CMP_CHUNK // CMP_STRIDE
    c2 = lambda a: pl.BlockSpec(a.shape, lambda b, j, pt: (0, 0))
    c3 = lambda a: pl.BlockSpec(a.shape, lambda b, j, pt: (0, 0, 0))
    smap_j = jnp.asarray(smap, dtype=BF16)
    gsum_j = jnp.asarray(gsum, dtype=BF16)
    grid_spec = pltpu.PrefetchScalarGridSpec(
        num_scalar_prefetch=1,
        grid=(nseq, nj),
        in_specs=_page_specs(PAGES_PER_STEP) + [c3(qT), c3(wcat), c3(w2bd), c2(bias), c2(perm), c2(smap_j), c2(gsum_j)],
        out_specs=[pl.BlockSpec((D_MODEL, nseq), lambda b, j, pt: (0, 0)),
                   pl.BlockSpec((1, nj_pad, nseq), lambda b, j, pt: (b, 0, 0))],
        scratch_shapes=[pltpu.VMEM((2, ns, CMP_STRIDE * KV_HALF), BF16),
                        pltpu.VMEM((2, n_sub, KV_COLS), F32)],
    )
    return pl.pallas_call(
        _nsa_s_cmp_kernel,
        grid_spec=grid_spec,
        out_shape=[jax.ShapeDtypeStruct((D_MODEL, nseq), F32),
                   jax.ShapeDtypeStruct((nseq, nj_pad, nseq), F32)],
        compiler_params=_cparams(("arbitrary", "arbitrary")),
        name="nsa_sample_cmp",
    )(page_table, *([cache_cmp] * PAGES_PER_STEP), qT, wcat, w2bd, bias, perm, smap_j, gsum_j)


def _nsa_s_att_kernel(n_past_blk, pt_ref, *refs):
    pages = refs[:PAGES_PER_STEP]
    (qrT_ref, sel_ref, nselT_ref, win_ref, nwinT_ref,
     ots_ref, otw_ref, nwo_ref, qblk_s, m_s, acc_s) = refs[PAGES_PER_STEP:]
    b = pl.program_id(0)
    j = pl.program_id(1)
    nj = pl.num_programs(1)
    n = qrT_ref.shape[2]
    blk_per_step = PAGES_PER_STEP * PAGE_SIZE // SEL_BLOCK

    @pl.when((b == 0) & (j == 0))
    def _():
        ots_ref[...] = jnp.zeros_like(ots_ref)
        otw_ref[...] = jnp.zeros_like(otw_ref)

    @pl.when(j == 0)
    def _():
        qblk_s[...] = _q_block(qrT_ref, b, n)
        m_s[...] = jnp.full(m_s.shape, NEG, F32)
        acc_s[...] = jnp.zeros_like(acc_s)

    qblk = qblk_s[...]

    def keys_update(kv_t, mask, m, acc):
        st = jnp.where(mask, _dot_tn(_bf(kv_t[:KV_HALF, :]), qblk), NEG)
        vt = jnp.concatenate([_bf(kv_t[KV_HALF:, :]), jnp.ones((16, kv_t.shape[1]), BF16)], axis=0)
        return _online_step(st, vt, m, acc)

    lane_n = lax.broadcasted_iota(jnp.int32, (1, n), 1)
    key_is0 = lax.broadcasted_iota(jnp.int32, (PAGE_SIZE, 1), 0) == 0

    def new_token_tile(ref):
        col = jnp.sum(jnp.where(lane_n == b, ref[...], 0.0), axis=1, keepdims=True)
        return jnp.where(lax.broadcasted_iota(jnp.int32, (1, PAGE_SIZE), 1) == 0, col, 0.0), col

    kv_step = jnp.concatenate([pages[i][0] for i in range(PAGES_PER_STEP)], axis=1)
    sel_rows = sel_ref[0, pl.ds(pl.multiple_of(j * blk_per_step, blk_per_step), blk_per_step), :]
    mask = jnp.concatenate([jnp.broadcast_to(sel_rows[r:r + 1, :], (SEL_BLOCK, n))
                            for r in range(blk_per_step)], axis=0) > 0.5
    m, acc = keys_update(kv_step, mask, m_s[...], acc_s[...])
    m_s[...] = m
    acc_s[...] = acc

    @pl.when(j == nj - 1)
    def _():
        kv_new, _ = new_token_tile(nselT_ref)
        sel_new = sel_ref[0, n_past_blk:n_past_blk + 1, :] > 0.5
        m2, acc2 = keys_update(kv_new, key_is0 & sel_new, m_s[...], acc_s[...])
        _scatter_heads(acc2[0:KV_HALF, :] / acc2[KV_HALF:KV_HALF + 1, :], ots_ref, b, n)
        win_t = win_ref[0]
        w_keys = win_t.shape[1]
        kv_new, new_col = new_token_tile(nwinT_ref)
        valid = jnp.concatenate([jnp.full((w_keys, 1), True), key_is0], axis=0)
        _, aw = keys_update(jnp.concatenate([win_t, kv_new], axis=1), valid,
                            jnp.full(m_s.shape, NEG, F32), jnp.zeros_like(acc2))
        _scatter_heads(aw[0:KV_HALF, :] / aw[KV_HALF:KV_HALF + 1, :], otw_ref, b, n)
        lane_w = lax.broadcasted_iota(jnp.int32, (1, w_keys), 1)
        nwo_ref[0] = jnp.where(lane_w == w_keys - 1, new_col, pltpu.roll(win_t, w_keys - 1, 1))


def _nsa_s_att(page_table, cache_sel, qrT, sel, nsel_rows, cache_win, nwin_rows, past_len):
    nseq = page_table.shape[0]
    nj = past_len // (PAGES_PER_STEP * PAGE_SIZE)
    wbuf = cache_win.shape[2]
    c2 = lambda a: pl.BlockSpec(a.shape, lambda b, j, pt: (0, 0))
    c3 = lambda a: pl.BlockSpec(a.shape, lambda b, j, pt: (0, 0, 0))
    acc_rows = KV_HALF + 16
    grid_spec = pltpu.PrefetchScalarGridSpec(
        num_scalar_prefetch=1,
        grid=(nseq, nj),
        in_specs=_page_specs(PAGES_PER_STEP) + [
            c3(qrT),
            pl.BlockSpec((1,) + sel.shape[1:], lambda b, j, pt: (b, 0, 0)),
            c2(nsel_rows),
            pl.BlockSpec((1, KV_COLS, wbuf), lambda b, j, pt: (b, 0, 0)),
            c2(nwin_rows)],
        out_specs=[pl.BlockSpec((D_MODEL, nseq), lambda b, j, pt: (0, 0)),
                   pl.BlockSpec((D_MODEL, nseq), lambda b, j, pt: (0, 0)),
                   pl.BlockSpec((1, KV_COLS, wbuf), lambda b, j, pt: (b, 0, 0))],
        scratch_shapes=[pltpu.VMEM((KV_HALF, nseq), BF16), pltpu.VMEM((1, nseq), F32),
                        pltpu.VMEM((acc_rows, nseq), F32)],
    )
    return pl.pallas_call(
        functools.partial(_nsa_s_att_kernel, past_len // SEL_BLOCK),
        grid_spec=grid_spec,
        out_shape=[jax.ShapeDtypeStruct((D_MODEL, nseq), F32), jax.ShapeDtypeStruct((D_MODEL, nseq), F32),
                   jax.ShapeDtypeStruct((nseq, KV_COLS, wbuf), F32)],
        compiler_params=_cparams(("arbitrary", "arbitrary")),
        name="nsa_sample_att",
    )(page_table, *([cache_sel] * PAGES_PER_STEP), qrT, sel, nsel_rows, cache_win, nwin_rows)


def _nsa_s_out_kernel(otc_ref, ots_ref, otw_ref, gT_ref, wn_ref, o_ref):
    n = otc_ref.shape[1]
    parts = []
    for hd in range(NSA_HEADS):
        rows = slice(hd * NSA_HEAD_DIM, (hd + 1) * NSA_HEAD_DIM)
        parts.append(gT_ref[0, 3 * hd:3 * hd + 1, :] * otc_ref[rows, :]
                     + gT_ref[0, 3 * hd + 1:3 * hd + 2, :] * ots_ref[rows, :]
                     + gT_ref[0, 3 * hd + 2:3 * hd + 3, :] * otw_ref[rows, :])
    o_t = _bf(jnp.concatenate(parts, axis=0))
    o_ref[...] = _dot_tn(o_t, wn_ref[...])


def _nsa_s_out(otc, ots, otw, gT, wn):
    n = otc.shape[1]
    f2 = lambda a: pl.BlockSpec(a.shape, lambda i: (0, 0))
    f3 = lambda a: pl.BlockSpec(a.shape, lambda i: (0, 0, 0))
    return pl.pallas_call(
        _nsa_s_out_kernel,
        grid=(1,),
        in_specs=[f2(otc), f2(ots), f2(otw), f3(gT), f2(wn)],
        out_specs=pl.BlockSpec((n, D_MODEL), lambda i: (0, 0)),
        out_shape=jax.ShapeDtypeStruct((n, D_MODEL), F32),
        compiler_params=_cparams(("arbitrary",)),
        name="nsa_sample_out",
    )(otc, ots, otw, gT, wn)


def _rope_tables(pos):
    half = ROPE_DIM // 2
    inv_freq = jnp.power(ROPE_THETA, -jnp.arange(half, dtype=F32) * 2.0 / ROPE_DIM)
    ang = pos.astype(F32)[None, :] * inv_freq[:, None]
    return jnp.cos(ang), jnp.sin(ang)


def _permute_w_in(w_in):
    sizes = (SSM_D_INNER, SSM_CONV_DIM, SSM_HEADS, NSA_HEADS * NSA_HEAD_DIM, KV_COLS, KV_COLS, KV_COLS,
             3 * NSA_HEADS, 2 * D_MODEL)
    offs = np.concatenate([[0], np.cumsum(sizes)])
    z, xbc, dt, q, kvc, kvs, kvw, ng, mg = (w_in[:, offs[i]:offs[i + 1]] for i in range(len(sizes)))
    pad = jnp.zeros((D_MODEL, PROJ_COLS - COL_SMALL - SSM_HEADS - 3 * NSA_HEADS), w_in.dtype)
    return _bf(jnp.concatenate([xbc, q, z, mg, kvc, kvs, kvw, dt, ng, pad], axis=1))


def _compress_weights(cmp_pe, cmp_w1, cmp_w2):
    eye = jnp.eye(NSA_KV_HEADS, dtype=F32)
    w1a = cmp_w1[:, :CMP_STRIDE]
    w1b = cmp_w1[:, CMP_STRIDE:]
    bd = lambda w: jnp.einsum('vlde,hk->vlhdke', w, eye).reshape(2, CMP_STRIDE * KV_HALF, KV_HALF)
    wcat = _bf(jnp.concatenate([bd(w1a), bd(w1b)], axis=2))
    w2bd = _bf(jnp.einsum('vef,hk->vhekf', cmp_w2, eye).reshape(2, KV_HALF, KV_HALF))
    pe8 = jnp.zeros((2, 8, CMP_LEN * NSA_HEAD_DIM), F32).at[:, 0, :].set(cmp_pe.reshape(2, -1))
    w1f4 = jnp.tile(cmp_w1.reshape(2, CMP_LEN * NSA_HEAD_DIM, NSA_HEAD_DIM), (1, 1, NSA_KV_HEADS))
    return wcat, w2bd, pe8, w1f4


def kernel(x_prompt, x_sample, cache_cmp_kv, cache_sel_kv, cache_win_kv, state_ssm, state_conv, page_table,
           w_in, conv_w, conv_b, dt_bias, a_log, d_skip, ssm_norm_w, w_ssm_out, cmp_pe, cmp_w1, cmp_w2,
           w_nsa_out, w_o, ln1_g, ln1_b, w_gate, w_up, w_down, ln2_g, ln2_b):
    bsz, t, _ = x_prompt.shape
    nseq, dec_seq, _ = x_sample.shape
    n_pool = cache_cmp_kv.shape[1]
    past_len = page_table.shape[1] * PAGE_SIZE
    assert w_in.shape[0] == 1 and dec_seq == 1 and nseq == LANES
    assert t % CMP_CHUNK == 0 and past_len % CMP_CHUNK == 0 and cache_win_kv.shape[2] == WINDOW

    w_in_p = _permute_w_in(w_in[0])
    pad_row = lambda v: jnp.zeros((1, LANES), F32).at[0, :SSM_HEADS].set(v)
    dtb_pad, alog_pad = pad_row(dt_bias[0]), pad_row(a_log[0])
    drow = jnp.repeat(d_skip[0], SSM_HEAD_DIM)[None, :]
    nw = ssm_norm_w[0][None, :]
    cw, cb = conv_w[0], conv_b[0][None, :]
    wcat, w2bd, pe8, w1f4 = _compress_weights(cmp_pe[0], cmp_w1[0], cmp_w2[0])
    w_ssm_b, w_nsa_b, w_o_b = _bf(w_ssm_out[0]), _bf(w_nsa_out[0]), _bf(w_o[0])
    nc = FFN_HIDDEN // FFN_CHUNK
    wg3 = _bf(w_gate[0]).reshape(D_MODEL, nc, FFN_CHUNK).transpose(1, 0, 2)
    wu3 = _bf(w_up[0]).reshape(D_MODEL, nc, FFN_CHUNK).transpose(1, 0, 2)
    wd3 = _bf(w_down[0]).reshape(nc, FFN_CHUNK, D_MODEL)
    g1, b1, g2, b2 = ln1_g[0][None, :], ln1_b[0][None, :], ln2_g[0][None, :], ln2_b[0][None, :]

    xp = x_prompt.reshape(bsz * t, D_MODEL)
    proj = _matmul(xp, w_in_p, 1024, PROJ_TN, "in_proj")
    y_ssm, new_ssm_p, new_conv_p = _ssd_prompt(proj, bsz, t, cw, cb, dtb_pad, alog_pad, drow, nw)
    b_ssm = _matmul(y_ssm, w_ssm_b, 1024, D_MODEL, "ssm_out_proj")
    cos_p, sin_p = _rope_tables(jnp.arange(t, dtype=jnp.int32))
    qT, qrT, ks, vsT, kw, vwT, gT, ncmp, nsel, nwin = _attn_prep(proj, bsz, t, KEY_SLAB, cos_p, sin_p)
    cmp_bias = _pe_bias(pe8, w1f4)
    kc, vcT = _compress_prompt(proj, bsz, t, wcat, w2bd, cmp_bias)
    b_nsa = _nsa_prompt(qT, qrT, gT, kc, vcT, ks, vsT, kw, vwT, w_nsa_b, bsz, t)
    h = _merge(xp, proj, b_ssm, b_nsa, w_o_b, g1, b1, 512)
    y_p = _ffn(h, wg3, wu3, wd3, g2, b2, 512).reshape(bsz, t, D_MODEL)

    def kv6(a_t):
        n_b, _, n_t = a_t.shape
        return jnp.moveaxis(a_t.reshape(n_b, 2, NSA_KV_HEADS, NSA_HEAD_DIM, n_t), -1, 1)[None]

    w_keep = min(WINDOW, t)
    new_win_p = kv6(nwin[:, :, t - w_keep:])

    xs_in = x_sample.reshape(nseq, D_MODEL)
    proj_s = _matmul(xs_in, w_in_p, nseq, PROJ_TN, "in_proj_s")
    cst = jnp.moveaxis(state_conv[0], 1, 0)
    xs_s, xdt_s, bm_s, cm_s, dec_s, ncv_s = _ssd_s_pre(proj_s, cst, cw, cb, dtb_pad, alog_pad)
    new_ssm_s, yT_s = _ssd_s_state(state_ssm[0], xdt_s, bm_s, cm_s, dec_s)
    y_ssm_s = _ssd_s_post(yT_s, xs_s, proj_s, drow, nw)
    b_ssm_s = _matmul(y_ssm_s, w_ssm_b, nseq, D_MODEL, "ssm_out_proj_s")
    cos_s, sin_s = _rope_tables(jnp.full((nseq,), past_len, dtype=jnp.int32))
    qT_s, qrT_s, _, _, _, _, gT_s, ncmp_s, nsel_s, nwin_s = _attn_prep(proj_s, 1, nseq, nseq, cos_s, sin_s)
    fmaj = lambda c, n_lead: jnp.moveaxis(c, 1, -1).reshape(n_lead, KV_COLS, c.shape[1])
    cache_cmp = fmaj(cache_cmp_kv[0], n_pool)
    cache_sel = fmaj(cache_sel_kv[0], n_pool)
    cache_win = fmaj(cache_win_kv[0], nseq)
    otc, sel = _nsa_s_cmp(page_table, cache_cmp, qT_s, wcat, w2bd, cmp_bias, past_len)
    ots, otw, new_win_t = _nsa_s_att(page_table, cache_sel, qrT_s, sel, nsel_s[0], cache_win, nwin_s[0], past_len)
    kv6_s = lambda a_t: kv6(jnp.transpose(a_t, (2, 1, 0)))
    b_nsa_s = _nsa_s_out(otc, ots, otw, gT_s, w_nsa_b)
    h_s = _merge(xs_in, proj_s, b_ssm_s, b_nsa_s, w_o_b, g1, b1, nseq)
    y_s = _ffn(h_s, wg3, wu3, wd3, g2, b2, nseq).reshape(nseq, 1, D_MODEL)

    return (y_p, y_s,
            kv6(ncmp), kv6(nsel), new_win_p,
            new_ssm_p[None], new_conv_p[None],
            kv6_s(ncmp_s), kv6_s(nsel_s),
            kv6(new_win_t),
            new_ssm_s[None],
            jnp.moveaxis(ncv_s, 0, 1)[None])
```

```python
import functools
import math

import numpy as np
import jax
import jax.numpy as jnp
from jax import lax
from jax.experimental import pallas as pl
from jax.experimental.pallas import tpu as pltpu

F32 = jnp.float32
BF16 = jnp.bfloat16

D_MODEL = 1024
SSM_D_INNER = 2048
SSM_HEAD_DIM = 64
SSM_HEADS = 32
SSM_GROUPS = 4
SSM_D_STATE = 128
SSM_CONV_W = 4
SSM_CONV_DIM = 3072
SSM_CHUNK = 128
NSA_HEADS = 16
NSA_KV_HEADS = 4
NSA_HEAD_DIM = 64
NSA_GQA = 4
KV_COLS = 512
KV_HALF = 256
CMP_LEN = 32
CMP_STRIDE = 16
SEL_BLOCK = 64
SEL_TOP_N = 16
WINDOW = 512
Q_BLOCK = 128
ROPE_DIM = 16
ROPE_THETA = 500000.0
FFN_HIDDEN = 2816
NORM_EPS = 1e-5
BIG = 1e30
NEG = -1e30
DEPTH = 1
DEEPNORM_ALPHA = (2 * DEPTH) ** 0.25
PAGE_SIZE = 128

LANES = 128
VMEM_LIMIT_BYTES = 56 * 1024 * 1024

COL_XBC = 0
COL_Q = 3072
COL_Z = 4096
COL_MG = 6144
COL_KVC = 8192
COL_KVS = 8704
COL_KVW = 9216
COL_SMALL = 9728
PROJ_COLS = 9984
PROJ_TN = 768

Q_SCALE = NSA_HEAD_DIM ** -0.5 * math.log2(math.e)
KEY_SLAB = 512
BLOCKS_PER_SLAB = KEY_SLAB // SEL_BLOCK
CHUNKS_PER_SLAB = KEY_SLAB // Q_BLOCK
CHUNK_SHIFT = 2
assert 1 << CHUNK_SHIFT == CHUNKS_PER_SLAB
QCHUNK = 256
N_QCHUNK = NSA_GQA * Q_BLOCK // QCHUNK
CHUNKS_PER_ITEM = 1
MXU_LOOKAHEAD = 12
CMP_CHUNK = 2048
FFN_CHUNK = 256


def _dot_dims(a, b, dims):
    return lax.dot_general(a, b, (dims, ((), ())), preferred_element_type=F32)


def _dot(a, b):
    return _dot_dims(a, b, ((1,), (0,)))


def _dot_nt(a, b):
    return _dot_dims(a, b, ((1,), (1,)))


def _dot_tn(a, b):
    return _dot_dims(a, b, ((0,), (0,)))


def _bf(x):
    return x.astype(BF16)


def _split3(x):
    hi = _bf(x)
    r1 = x - hi.astype(F32)
    mid = _bf(r1)
    lo = _bf(r1 - mid.astype(F32))
    return hi, mid, lo


def _dot_exact_lhs(x, w_bf16):
    hi, mid, lo = _split3(x)
    return _dot(hi, w_bf16) + _dot(mid, w_bf16) + _dot(lo, w_bf16)


def _dot_exact_rhs(w_bf16, x):
    hi, mid, lo = _split3(x)
    return _dot(w_bf16, hi) + _dot(w_bf16, mid) + _dot(w_bf16, lo)


def _silu(x):
    h = 0.5 * x
    return h + h * jnp.tanh(h)


def _softplus(x):
    e = jnp.exp(-jnp.abs(x))
    u = 1.0 + e
    log1p_e = jnp.where(u == 1.0, e, jnp.log(u) * e / jnp.where(u == 1.0, 1.0, u - 1.0))
    return jnp.maximum(x, 0.0) + log1p_e


def _gelu_tanh(x):
    return 0.5 * x * (1.0 + jnp.tanh(0.7978845608028654 * (x + 0.044715 * (x * x * x))))


def _layer_norm(x, g, b):
    mu = jnp.mean(x, axis=-1, keepdims=True)
    xc = x - mu
    var = jnp.mean(xc * xc, axis=-1, keepdims=True)
    return xc * lax.rsqrt(var + NORM_EPS) * g + b


def _cparams(sem):
    return pltpu.CompilerParams(dimension_semantics=sem, vmem_limit_bytes=VMEM_LIMIT_BYTES)


def _mm_kernel(x_ref, w_ref, o_ref):
    o_ref[...] = _dot(_bf(x_ref[...]), w_ref[...])


def _matmul(x, w, tm, tn, name):
    m, k = x.shape
    n = w.shape[1]
    return pl.pallas_call(
        _mm_kernel,
        grid=(m // tm, n // tn),
        in_specs=[pl.BlockSpec((tm, k), lambda i, j: (i, 0)),
                  pl.BlockSpec((k, tn), lambda i, j: (0, j))],
        out_specs=pl.BlockSpec((tm, tn), lambda i, j: (i, j)),
        out_shape=jax.ShapeDtypeStruct((m, n), F32),
        compiler_params=_cparams(("parallel", "arbitrary")),
        name=name,
    )(x, w)


def _ssd_kernel(xbc_ref, z_ref, sm_ref, cw_ref, cb_ref, dtb_ref, alog_ref, drow_ref, nw_ref, r3_ref,
                y_ref, st_ref, cv_ref, xp_s, stT_s):
    j = pl.program_id(1)
    nj = pl.num_programs(1)
    q = SSM_CHUNK

    @pl.when(j == 0)
    def _():
        xp_s[0:8, :] = jnp.zeros((8, SSM_CONV_DIM), F32)
        stT_s[...] = jnp.zeros_like(stT_s)

    xp_s[8:8 + q, :] = xbc_ref[...]
    acc = cb_ref[...]
    for k in range(SSM_CONV_W):
        acc = acc + cw_ref[k:k + 1, :] * xp_s[5 + k:5 + k + q, :]
    act = _silu(acc)
    tail = xp_s[q + 5:q + 8, :]
    xp_s[5:8, :] = tail

    @pl.when(j == nj - 1)
    def _():
        cv_ref[0] = tail

    lane = lax.broadcasted_iota(jnp.int32, (1, LANES), 1)
    a_full = jnp.where(lane < SSM_HEADS, -jnp.exp(alog_ref[...]), 0.0)
    dt = _softplus(sm_ref[...] + dtb_ref[...])
    da = dt * a_full
    row_i = lax.broadcasted_iota(jnp.int32, (q, q), 0)
    col_i = lax.broadcasted_iota(jnp.int32, (q, q), 1)
    causal = col_i <= row_i
    tril = jnp.where(causal, 1.0, 0.0).astype(BF16)
    cum = _dot_exact_rhs(tril, da)
    cumT = cum.T
    dtT = dt.T
    hi, mid, lo = _split3(cum)
    packed = _bf(hi.astype(F32) + pltpu.roll(mid.astype(F32), 32, 1) + pltpu.roll(lo.astype(F32), 64, 1))
    lane_lo = lax.broadcasted_iota(jnp.int32, (1, LANES), 1) < SSM_HEAD_DIM

    heads_per_group = SSM_HEADS // SSM_GROUPS
    pairs_per_group = heads_per_group // 2
    n_pairs = SSM_HEADS // 2
    cmbs, cbs, bTs = [], [], []
    for g in range(SSM_GROUPS):
        bm_g = act[:, SSM_D_INNER + g * SSM_D_STATE:SSM_D_INNER + (g + 1) * SSM_D_STATE]
        cm_g = act[:, SSM_D_INNER + SSM_GROUPS * SSM_D_STATE + g * SSM_D_STATE:
                   SSM_D_INNER + SSM_GROUPS * SSM_D_STATE + (g + 1) * SSM_D_STATE]
        cmbs.append(_bf(cm_g))
        cbs.append(_dot_nt(cmbs[g], _bf(bm_g)))
        bTs.append(bm_g.T)
    cols_all = [_dot(packed, r3_ref[:, pair * 2 * LANES:(pair + 1) * 2 * LANES]) for pair in range(n_pairs)]
    y_inter = [_dot(cmbs[pair // pairs_per_group], _bf(stT_s[:, pair * LANES:(pair + 1) * LANES]))
               for pair in range(n_pairs)]

    y_parts = []
    for g in range(SSM_GROUPS):
        cmb, cb, bT = cmbs[g], cbs[g], bTs[g]
        for pp in range(pairs_per_group):
            pair = g * pairs_per_group + pp
            h0 = 2 * pair
            xs_pair = act[:, pair * LANES:(pair + 1) * LANES]
            xs_a = _bf(jnp.where(lane_lo, xs_pair, 0.0))
            xs_b = _bf(jnp.where(lane_lo, 0.0, xs_pair))
            cols2 = cols_all[pair]
            y_pair = None
            ds_pair = None
            lasts = []
            cols = []
            for hh, xs_m in ((0, xs_a), (1, xs_b)):
                h = h0 + hh
                col = cols2[:, hh * LANES:(hh + 1) * LANES]
                row = cumT[h:h + 1, :]
                dtrow = dtT[h:h + 1, :]
                dec = jnp.exp(jnp.where(causal, col - row, NEG))
                m_h = _bf(cb * dec * dtrow)
                y_h = _dot(m_h, xs_m)
                last = col[q - 1:q, :]
                wrow = jnp.exp(last - row) * dtrow
                ds_h = _dot(_bf(bT * wrow), xs_m)
                y_pair = y_h if y_pair is None else y_pair + y_h
                ds_pair = ds_h if ds_pair is None else ds_pair + ds_h
                lasts.append(last)
                cols.append(col)
            st_pair = stT_s[:, pair * LANES:(pair + 1) * LANES]
            scale_t = jnp.exp(jnp.where(lane_lo, cols[0], cols[1]))
            y_pair = y_pair + y_inter[pair] * scale_t
            stT_s[:, pair * LANES:(pair + 1) * LANES] = (
                st_pair * jnp.exp(jnp.where(lane_lo, lasts[0], lasts[1])) + ds_pair)
            y_pair = y_pair + drow_ref[:, pair * LANES:(pair + 1) * LANES] * xs_pair
            y_parts.append(y_pair)
    y = jnp.concatenate(y_parts, axis=1)
    v = y * _silu(z_ref[...])
    gw = SSM_D_INNER // SSM_GROUPS
    outs = []
    for g in range(SSM_GROUPS):
        vg = v[:, g * gw:(g + 1) * gw]
        ms = jnp.sum(vg * vg, axis=-1, keepdims=True) * (1.0 / gw)
        outs.append(vg * lax.rsqrt(ms + NORM_EPS) * nw_ref[:, g * gw:(g + 1) * gw])
    y_ref[...] = _bf(jnp.concatenate(outs, axis=1))

    @pl.when(j == nj - 1)
    def _():
        st_ref[0] = stT_s[...].T.reshape(SSM_HEADS, SSM_HEAD_DIM, SSM_D_STATE)


def _r3_table():
    k = np.arange(LANES)[:, None]
    c = np.arange(SSM_HEADS * LANES)[None, :]
    return jnp.asarray(((k % SSM_HEADS) == (c // LANES)) & (k < 3 * SSM_HEADS), dtype=BF16)


def _ssd_prompt(proj, bsz, t, cw, cb, dtb_pad, alog_pad, drow, nw):
    nch = t // SSM_CHUNK
    q = SSM_CHUNK
    row = lambda b, j: b * nch + j
    const = lambda shape: pl.BlockSpec(shape, lambda b, j: (0, 0))
    return pl.pallas_call(
        _ssd_kernel,
        grid=(bsz, nch),
        in_specs=[
            pl.BlockSpec((q, SSM_CONV_DIM), lambda b, j: (row(b, j), COL_XBC // SSM_CONV_DIM)),
            pl.BlockSpec((q, SSM_D_INNER), lambda b, j: (row(b, j), COL_Z // SSM_D_INNER)),
            pl.BlockSpec((q, LANES), lambda b, j: (row(b, j), COL_SMALL // LANES)),
            const((SSM_CONV_W, SSM_CONV_DIM)), const((1, SSM_CONV_DIM)),
            const((1, LANES)), const((1, LANES)), const((1, SSM_D_INNER)), const((1, SSM_D_INNER)),
            const((LANES, SSM_HEADS * LANES)),
        ],
        out_specs=[
            pl.BlockSpec((q, SSM_D_INNER), lambda b, j: (row(b, j), 0)),
            pl.BlockSpec((1, SSM_HEADS, SSM_HEAD_DIM, SSM_D_STATE), lambda b, j: (b, 0, 0, 0)),
            pl.BlockSpec((1, SSM_CONV_W - 1, SSM_CONV_DIM), lambda b, j: (b, 0, 0)),
        ],
        out_shape=[
            jax.ShapeDtypeStruct((bsz * t, SSM_D_INNER), BF16),
            jax.ShapeDtypeStruct((bsz, SSM_HEADS, SSM_HEAD_DIM, SSM_D_STATE), F32),
            jax.ShapeDtypeStruct((bsz, SSM_CONV_W - 1, SSM_CONV_DIM), F32),
        ],
        scratch_shapes=[pltpu.VMEM((q + 8, SSM_CONV_DIM), F32),
                        pltpu.VMEM((SSM_D_STATE, SSM_D_INNER), F32)],
        compiler_params=_cparams(("parallel", "arbitrary")),
        name="ssd_prompt",
    )(proj, proj, proj, cw, cb, dtb_pad, alog_pad, drow, nw, _r3_table())


def _rope_t(x_t, nh, c, s):
    n = x_t.shape[1]
    x3 = x_t.reshape(nh, NSA_HEAD_DIM, n)
    half = ROPE_DIM // 2
    x1 = x3[:, 0:half, :]
    x2 = x3[:, half:ROPE_DIM, :]
    r1 = x1 * c - x2 * s
    r2 = x2 * c + x1 * s
    return jnp.concatenate([r1, r2, x3[:, ROPE_DIM:, :]], axis=1).reshape(nh * NSA_HEAD_DIM, n)


def _prep_kernel(q_ref, kvc_ref, kvs_ref, kvw_ref, sm_ref, cos_ref, sin_ref,
                 qT_ref, qrT_ref, ks_ref, vsT_ref, kw_ref, vwT_ref, gT_ref, ncmp_ref, nsel_ref, nwin_ref):
    c = cos_ref[...]
    s = sin_ref[...]
    tt = q_ref.shape[0]
    q_t = q_ref[...].T
    qT_ref[0] = _bf(q_t * Q_SCALE)
    qrT_ref[0] = _bf(_rope_t(q_t, NSA_HEADS, c, s) * Q_SCALE)
    ncmp_ref[0] = kvc_ref[...].T
    for src, full_out, k_out, vt_out in ((kvs_ref, nsel_ref, ks_ref, vsT_ref), (kvw_ref, nwin_ref, kw_ref, vwT_ref)):
        kv = src[...]
        k_rot_t = _rope_t(kv[:, :KV_HALF].T, NSA_KV_HEADS, c, s)
        k_rot = k_rot_t.T
        v_t = kv[:, KV_HALF:].T
        full_out[0, :KV_HALF, :] = k_rot_t
        full_out[0, KV_HALF:, :] = v_t
        for i in range(tt // Q_BLOCK):
            k_out[0, i] = _bf(k_rot[i * Q_BLOCK:(i + 1) * Q_BLOCK, :])
            vt_out[0, i] = _bf(v_t[:, i * Q_BLOCK:(i + 1) * Q_BLOCK])
    g_t = jax.nn.sigmoid(sm_ref[...]).T
    gT_ref[0] = g_t[SSM_HEADS:SSM_HEADS + 3 * NSA_HEADS, :]


def _attn_prep(proj, bsz, t, tt, cos_t, sin_t):
    nt = t // tt
    nb = tt // Q_BLOCK
    row = lambda b, j: b * nt + j
    n_gate = 3 * NSA_HEADS
    return pl.pallas_call(
        _prep_kernel,
        grid=(bsz, nt),
        in_specs=[
            pl.BlockSpec((tt, D_MODEL), lambda b, j: (row(b, j), COL_Q // D_MODEL)),
            pl.BlockSpec((tt, KV_COLS), lambda b, j: (row(b, j), COL_KVC // KV_COLS)),
            pl.BlockSpec((tt, KV_COLS), lambda b, j: (row(b, j), COL_KVS // KV_COLS)),
            pl.BlockSpec((tt, KV_COLS), lambda b, j: (row(b, j), COL_KVW // KV_COLS)),
            pl.BlockSpec((tt, LANES), lambda b, j: (row(b, j), COL_SMALL // LANES)),
            pl.BlockSpec((ROPE_DIM // 2, tt), lambda b, j: (0, j)),
            pl.BlockSpec((ROPE_DIM // 2, tt), lambda b, j: (0, j)),
        ],
        out_specs=[
            pl.BlockSpec((1, D_MODEL, tt), lambda b, j: (b, 0, j)),
            pl.BlockSpec((1, D_MODEL, tt), lambda b, j: (b, 0, j)),
            pl.BlockSpec((1, nb, Q_BLOCK, KV_HALF), lambda b, j: (b, j, 0, 0)),
            pl.BlockSpec((1, nb, KV_HALF, Q_BLOCK), lambda b, j: (b, j, 0, 0)),
            pl.BlockSpec((1, nb, Q_BLOCK, KV_HALF), lambda b, j: (b, j, 0, 0)),
            pl.BlockSpec((1, nb, KV_HALF, Q_BLOCK), lambda b, j: (b, j, 0, 0)),
            pl.BlockSpec((1, n_gate, tt), lambda b, j: (b, 0, j)),
            pl.BlockSpec((1, KV_COLS, tt), lambda b, j: (b, 0, j)),
            pl.BlockSpec((1, KV_COLS, tt), lambda b, j: (b, 0, j)),
            pl.BlockSpec((1, KV_COLS, tt), lambda b, j: (b, 0, j)),
        ],
        out_shape=[
            jax.ShapeDtypeStruct((bsz, D_MODEL, t), BF16),
            jax.ShapeDtypeStruct((bsz, D_MODEL, t), BF16),
            jax.ShapeDtypeStruct((bsz, t // Q_BLOCK, Q_BLOCK, KV_HALF), BF16),
            jax.ShapeDtypeStruct((bsz, t // Q_BLOCK, KV_HALF, Q_BLOCK), BF16),
            jax.ShapeDtypeStruct((bsz, t // Q_BLOCK, Q_BLOCK, KV_HALF), BF16),
            jax.ShapeDtypeStruct((bsz, t // Q_BLOCK, KV_HALF, Q_BLOCK), BF16),
            jax.ShapeDtypeStruct((bsz, n_gate, t), F32),
            jax.ShapeDtypeStruct((bsz, KV_COLS, t), F32),
            jax.ShapeDtypeStruct((bsz, KV_COLS, t), F32),
            jax.ShapeDtypeStruct((bsz, KV_COLS, t), F32),
        ],
        compiler_params=_cparams(("parallel", "parallel")),
        name="attn_prep",
    )(proj, proj, proj, proj, proj, cos_t, sin_t)


def _compress_partial(data_refs, wcat_ref, xcat_s):
    ns = data_refs[0].shape[0] // CMP_STRIDE
    tiles_per_half = KV_HALF // LANES
    outs = []
    for lp in range(CMP_STRIDE):
        for c, ref in enumerate(data_refs):
            rows = ref[pl.ds(lp, ns, stride=CMP_STRIDE), :]
            v, cc = divmod(c, tiles_per_half)
            xcat_s[v, :, lp * KV_HALF + cc * LANES:lp * KV_HALF + (cc + 1) * LANES] = _bf(rows)
    for v in range(2):
        outs.append(_dot(xcat_s[v], wcat_ref[v]))
    return outs


def _compress_finish(pab_s, bias, w2bd_ref):
    n_sub = pab_s.shape[1]
    res = []
    for v in range(2):
        pab = pab_s[v]
        pre = pab[:, :KV_HALF] + pltpu.roll(pab[:, KV_HALF:], n_sub - 1, 0) + bias[v:v + 1, :]
        res.append(_dot(_bf(_gelu_tanh(pre)), w2bd_ref[v]))
    return res


def _pe_bias_kernel(pe_ref, w1f_ref, o_ref):
    rows = []
    for v in range(2):
        rows.append(jnp.dot(pe_ref[v], w1f_ref[v], preferred_element_type=F32,
                            precision=lax.Precision.HIGHEST)[0:1, :])
    o_ref[...] = jnp.concatenate(rows + [jnp.zeros((6, KV_HALF), F32)], axis=0)


def _pe_bias(pe8, w1f4):
    c3 = lambda a: pl.BlockSpec(a.shape, lambda i: (0, 0, 0))
    return pl.pallas_call(
        _pe_bias_kernel, grid=(1,), in_specs=[c3(pe8), c3(w1f4)],
        out_specs=pl.BlockSpec((8, KV_HALF), lambda i: (0, 0)),
        out_shape=jax.ShapeDtypeStruct((8, KV_HALF), F32),
        compiler_params=_cparams(("arbitrary",)), name="cmp_pe_bias",
    )(pe8, w1f4)


def _compress_prompt_kernel(d0_ref, d1_ref, d2_ref, d3_ref, wcat_ref, w2bd_ref, bias_ref,
                            kc_ref, vcT_ref, xcat_s, pab_s):
    j = pl.program_id(1)
    nj = pl.num_programs(1)
    ns = CMP_CHUNK // CMP_STRIDE
    parts = _compress_partial((d0_ref, d1_ref, d2_ref, d3_ref), wcat_ref, xcat_s)
    for v in range(2):
        pab_s[v, pl.ds(pl.multiple_of(j * ns, ns), ns), :] = parts[v]

    @pl.when(j == nj - 1)
    def _():
        kc, vc = _compress_finish(pab_s, bias_ref[...], w2bd_ref)
        kc_ref[0] = _bf(kc)
        vcT_ref[0] = _bf(vc.T)


def _compress_prompt(proj, bsz, t, wcat, w2bd, bias):
    nj = t // CMP_CHUNK
    n_sub = t // CMP_STRIDE
    ns = CMP_CHUNK // CMP_STRIDE
    c3 = lambda shape: pl.BlockSpec(shape, lambda b, j: (0, 0, 0))
    return pl.pallas_call(
        _compress_prompt_kernel,
        grid=(bsz, nj),
        in_specs=[pl.BlockSpec((CMP_CHUNK, LANES),
                               functools.partial(lambda b, j, c: (b * nj + j, COL_KVC // LANES + c), c=c))
                  for c in range(KV_COLS // LANES)]
        + [c3(wcat.shape), c3(w2bd.shape), pl.BlockSpec(bias.shape, lambda b, j: (0, 0))],
        out_specs=[pl.BlockSpec((1, n_sub, KV_HALF), lambda b, j: (b, 0, 0)),
                   pl.BlockSpec((1, KV_HALF, n_sub), lambda b, j: (b, 0, 0))],
        out_shape=[jax.ShapeDtypeStruct((bsz, n_sub, KV_HALF), BF16),
                   jax.ShapeDtypeStruct((bsz, KV_HALF, n_sub), BF16)],
        scratch_shapes=[pltpu.VMEM((2, ns, CMP_STRIDE * KV_HALF), BF16),
                        pltpu.VMEM((2, n_sub, KV_COLS), F32)],
        compiler_params=_cparams(("parallel", "arbitrary")),
        name="compress_prompt",
    )(proj, proj, proj, proj, wcat, w2bd, bias)


def _topk_mask(score, jrow, n_pick):
    sel = jnp.zeros_like(score)
    for _ in range(n_pick):
        mx = jnp.max(score, axis=0, keepdims=True)
        idx = jnp.min(jnp.where(score == mx, jrow, 1e9), axis=0, keepdims=True)
        chosen = jrow == idx
        sel = jnp.where(chosen, 1.0, sel)
        score = jnp.where(chosen, -jnp.inf, score)
    return sel


def _online_step(st, vt_aug, m, acc):
    mn = jnp.maximum(m, jnp.max(st, axis=0, keepdims=True))
    p = jnp.exp2(st - mn)
    acc = acc * jnp.exp2(m - mn) + _dot(vt_aug, _bf(p))
    return mn, acc


def _nsa_prompt_kernel(qT_ref, qrT_ref, gT_ref, kc_ref, vcT_ref, ks_ref, vsT_ref, kw_ref, vwT_ref, smapT_ref,
                       econst_ref, wn_ref, o_ref, qr_s, neg_s, oc_s, m_s, acc_s):
    qb = pl.program_id(1)
    nq = NSA_GQA * Q_BLOCK
    n_cmp_rows = kc_ref.shape[1]
    n_selblk = smapT_ref.shape[0]
    n_slab = n_selblk // BLOCKS_PER_SLAB
    lane_q = lax.broadcasted_iota(jnp.int32, (1, nq), 1) % Q_BLOCK
    qpos = qb * Q_BLOCK + lane_q
    zero_q = jnp.zeros((NSA_HEAD_DIM, nq), BF16)
    acc_rows = NSA_HEAD_DIM + 16

    jrow = lax.broadcasted_iota(jnp.int32, (n_selblk, Q_BLOCK), 0).astype(F32)
    cur = (qb * (Q_BLOCK // SEL_BLOCK)
           + lax.broadcasted_iota(jnp.int32, (1, Q_BLOCK), 1) // SEL_BLOCK).astype(F32)
    future = jrow > cur
    forced = jnp.where(jrow == 0.0, 1.0, 0.0) + jnp.where(jrow == cur, 1.0, 0.0) + jnp.where(jrow == cur - 1.0, 1.0, 0.0)
    crow = lax.broadcasted_iota(jnp.int32, (n_cmp_rows, 1), 0)
    cmp_valid = (crow * CMP_STRIDE + (CMP_LEN - 1)) <= qpos

    def q_cat(ref, h):
        return jnp.concatenate([ref[0, (h * NSA_GQA + g) * NSA_HEAD_DIM:(h * NSA_GQA + g + 1) * NSA_HEAD_DIM, :]
                                for g in range(NSA_GQA)], axis=1)

    for h in range(NSA_KV_HEADS):
        qp = jnp.concatenate([q_cat(qT_ref, h) if hh == h else zero_q for hh in range(NSA_KV_HEADS)], axis=0)
        sc = jnp.where(cmp_valid, _dot(kc_ref[0], qp), NEG)
        m_c = jnp.max(sc, axis=0, keepdims=True)
        e_c = jnp.where(cmp_valid, jnp.exp2(sc - m_c), 0.0)
        inv_l = 1.0 / jnp.maximum(jnp.sum(e_c, axis=0, keepdims=True), 1e-30)
        p_c = e_c * inv_l
        oc_s[h] = _dot(vcT_ref[0, h * NSA_HEAD_DIM:(h + 1) * NSA_HEAD_DIM, :], _bf(p_c))
        psum = (p_c[:, 0:Q_BLOCK] + p_c[:, Q_BLOCK:2 * Q_BLOCK]
                + p_c[:, 2 * Q_BLOCK:3 * Q_BLOCK] + p_c[:, 3 * Q_BLOCK:4 * Q_BLOCK])
        p_hi = _bf(psum)
        p_lo = _bf(psum - p_hi.astype(F32))
        imp_t = _dot(smapT_ref[...], p_hi) + _dot(smapT_ref[...], p_lo)
        score = jnp.where(future, -BIG, jnp.where(forced > 0.5, BIG, imp_t))
        sel = _topk_mask(score, jrow, min(SEL_TOP_N, n_selblk))
        neg = jnp.where(sel > 0.5, 0.0, NEG)
        neg = jnp.concatenate([neg] * NSA_GQA, axis=1).reshape(n_slab, BLOCKS_PER_SLAB, nq)
        neg_s[h] = _bf(jnp.concatenate([neg, jnp.zeros_like(neg)], axis=1))
        qr_h = q_cat(qrT_ref, h)
        rhs_top = jnp.concatenate([qr_h, zero_q] if h % 2 == 0 else [zero_q, qr_h], axis=0)
        for qc in range(N_QCHUNK):
            cols = slice(qc * QCHUNK, (qc + 1) * QCHUNK)
            qr_s[h, qc, 0:2 * NSA_HEAD_DIM, :] = rhs_top[:, cols]
            qr_s[h, qc, 2 * NSA_HEAD_DIM:, :] = jnp.zeros((2 * NSA_HEAD_DIM, QCHUNK), BF16)
            for br in range(2):
                m_s[br, h, qc] = jnp.full((8, QCHUNK), NEG, F32)
                acc_s[br, h, qc] = jnp.zeros((acc_rows, QCHUNK), F32)

    ones16 = jnp.ones((16, Q_BLOCK), BF16)
    qpos_c = qpos[:, 0:QCHUNK]
    blk_row = lax.broadcasted_iota(jnp.int32, (Q_BLOCK, 1), 0)

    def tile_update(br, h, qc, sts, vt_ref, kcs):
        vt = jnp.concatenate([vt_ref[0, kc, h * NSA_HEAD_DIM:(h + 1) * NSA_HEAD_DIM, :] for kc in kcs], axis=1)
        vt = jnp.concatenate([vt, jnp.ones((16, len(kcs) * Q_BLOCK), BF16)], axis=0)
        st = sts[0] if len(sts) == 1 else jnp.concatenate(sts, axis=0)
        mn, acc = _online_step(st, vt, m_s[br, h, qc, 0:1, :], acc_s[br, h, qc])
        m_s[br, h, qc] = jnp.broadcast_to(mn, (8, QCHUNK))
        acc_s[br, h, qc] = acc

    def sel_scores(kc, h, qc):
        slab = lax.shift_right_logical(kc, CHUNK_SHIFT)
        qr_s[h, qc, 2 * NSA_HEAD_DIM:2 * NSA_HEAD_DIM + 16, :] = neg_s[h, slab, :, qc * QCHUNK:(qc + 1) * QCHUNK]
        pair = h // 2
        lhs = jnp.concatenate([ks_ref[0, kc, :, pair * LANES:(pair + 1) * LANES],
                               econst_ref[kc & (CHUNKS_PER_SLAB - 1)]], axis=1)
        return _dot(lhs, qr_s[h, qc])

    tiles = [(h, qc) for h in range(NSA_KV_HEADS) for qc in range(N_QCHUNK)]

    def run_pipelined(items):
        pending = []
        for score_fn, update_fn in items:
            pending.append((update_fn, score_fn()))
            if len(pending) > MXU_LOOKAHEAD:
                fn, st = pending.pop(0)
                fn(st)
        for fn, st in pending:
            fn(st)

    def sel_items(kcs, masked):
        def score(h, qc):
            sts = [sel_scores(kc, h, qc) for kc in kcs]
            if masked:
                sts = [jnp.where((kc * Q_BLOCK + blk_row) <= qpos_c, st, NEG) for kc, st in zip(kcs, sts)]
            return sts
        return [(functools.partial(score, h, qc),
                 functools.partial(lambda sts, h, qc: tile_update(0, h, qc, sts, vsT_ref, kcs), h=h, qc=qc))
                for h, qc in tiles]

    n_full = lax.shift_right_logical(qb, CHUNK_SHIFT)

    def slab_body(s, carry):
        items = []
        for c in range(0, CHUNKS_PER_SLAB, CHUNKS_PER_ITEM):
            items += sel_items([s * CHUNKS_PER_SLAB + c + d for d in range(CHUNKS_PER_ITEM)], False)
        run_pipelined(items)
        return carry

    lax.fori_loop(0, n_full, slab_body, 0)

    causal = (qb * Q_BLOCK + blk_row) <= qpos_c
    n_wblk = WINDOW // Q_BLOCK + 1

    def win_scores(i, h, qc):
        kb = qb - (n_wblk - 1) + i
        st = _dot(kw_ref[0, jnp.maximum(kb, 0), :, (h // 2) * LANES:(h // 2 + 1) * LANES],
                  qr_s[h, qc, 0:2 * NSA_HEAD_DIM, :])
        if i == 0:
            st = jnp.where((qpos_c - (kb * Q_BLOCK + blk_row)) <= WINDOW, st, NEG)
        if i == n_wblk - 1:
            return jnp.where(causal, st, NEG)
        return jnp.where(kb >= 0, st, NEG)

    items = []
    for c in range(0, CHUNKS_PER_SLAB, CHUNKS_PER_ITEM):
        items += sel_items([n_full * CHUNKS_PER_SLAB + c + d for d in range(CHUNKS_PER_ITEM)], True)
    for i0 in range(0, n_wblk, CHUNKS_PER_ITEM):
        blks = list(range(i0, min(i0 + CHUNKS_PER_ITEM, n_wblk)))
        kbs_ld = [jnp.maximum(qb - (n_wblk - 1) + i, 0) for i in blks]
        items += [(functools.partial(lambda h, qc, blks: [win_scores(i, h, qc) for i in blks], h, qc, blks),
                   functools.partial(lambda sts, h, qc, kbs_ld: tile_update(1, h, qc, sts, vwT_ref, kbs_ld),
                                     h=h, qc=qc, kbs_ld=kbs_ld))
                  for h, qc in tiles]
    run_pipelined(items)

    ot_parts = []
    for h in range(NSA_KV_HEADS):
        o_parts = []
        for qc in range(N_QCHUNK):
            outs = []
            for br in range(2):
                acc = acc_s[br, h, qc]
                outs.append(acc[0:NSA_HEAD_DIM, :] * (1.0 / acc[NSA_HEAD_DIM:NSA_HEAD_DIM + 1, :]))
            o_parts.append(outs)

        def gate_row(br):
            return jnp.concatenate([gT_ref[0, (h * NSA_GQA + g) * 3 + br:(h * NSA_GQA + g) * 3 + br + 1, :]
                                    for g in range(NSA_GQA)], axis=1)

        os_t = jnp.concatenate([o_parts[qc][0] for qc in range(N_QCHUNK)], axis=1)
        ow_t = jnp.concatenate([o_parts[qc][1] for qc in range(N_QCHUNK)], axis=1)
        o_h = gate_row(0) * oc_s[h] + gate_row(1) * os_t + gate_row(2) * ow_t
        for g in range(NSA_GQA):
            ot_parts.append(o_h[:, g * Q_BLOCK:(g + 1) * Q_BLOCK])
    o_t = _bf(jnp.concatenate(ot_parts, axis=0))
    o_ref[...] = _dot_tn(o_t, wn_ref[...])


def _selection_map_t(n_sel, n_cmp_rows):
    i = np.arange(n_cmp_rows)[None, :]
    j = np.arange(n_sel)[:, None]
    ov = (i * CMP_STRIDE < (j + 1) * SEL_BLOCK) & (i * CMP_STRIDE + CMP_LEN > j * SEL_BLOCK)
    return ov


def _nsa_prompt(qT, qrT, gT, kc, vcT, ks, vsT, kw, vwT, wn, bsz, t):
    nqb = t // Q_BLOCK
    n_sub = t // CMP_STRIDE
    n_cmp = n_sub - CMP_LEN // CMP_STRIDE + 1
    n_sel = t // SEL_BLOCK
    smap = _selection_map_t(n_sel, n_sub) & (np.arange(n_sub)[None, :] < n_cmp)
    smap_t = jnp.asarray(smap, dtype=BF16)
    blk_of_key = (np.arange(KEY_SLAB) // SEL_BLOCK).reshape(CHUNKS_PER_SLAB, Q_BLOCK, 1)
    econst = jnp.asarray(blk_of_key == np.arange(LANES)[None, None, :], dtype=BF16)
    nsl = t // KEY_SLAB
    nq = NSA_GQA * Q_BLOCK
    per_b3 = lambda shape: pl.BlockSpec(shape, lambda b, j: (b, 0, 0))
    per_b4 = lambda shape: pl.BlockSpec(shape, lambda b, j: (b, 0, 0, 0))
    return pl.pallas_call(
        _nsa_prompt_kernel,
        grid=(bsz, nqb),
        in_specs=[
            pl.BlockSpec((1, D_MODEL, Q_BLOCK), lambda b, j: (b, 0, j)),
            pl.BlockSpec((1, D_MODEL, Q_BLOCK), lambda b, j: (b, 0, j)),
            pl.BlockSpec((1, 3 * NSA_HEADS, Q_BLOCK), lambda b, j: (b, 0, j)),
            per_b3((1, n_sub, KV_HALF)), per_b3((1, KV_HALF, n_sub)),
            per_b4((1, nqb, Q_BLOCK, KV_HALF)), per_b4((1, nqb, KV_HALF, Q_BLOCK)),
            per_b4((1, nqb, Q_BLOCK, KV_HALF)), per_b4((1, nqb, KV_HALF, Q_BLOCK)),
            pl.BlockSpec((n_sel, n_sub), lambda b, j: (0, 0)),
            pl.BlockSpec((CHUNKS_PER_SLAB, Q_BLOCK, LANES), lambda b, j: (0, 0, 0)),
            pl.BlockSpec((D_MODEL, D_MODEL), lambda b, j: (0, 0)),
        ],
        out_specs=pl.BlockSpec((Q_BLOCK, D_MODEL), lambda b, j: (b * nqb + j, 0)),
        out_shape=jax.ShapeDtypeStruct((bsz * t, D_MODEL), F32),
        scratch_shapes=[pltpu.VMEM((NSA_KV_HEADS, N_QCHUNK, KV_HALF, QCHUNK), BF16),
                        pltpu.VMEM((NSA_KV_HEADS, nsl, 16, nq), BF16),
                        pltpu.VMEM((NSA_KV_HEADS, NSA_HEAD_DIM, nq), F32),
                        pltpu.VMEM((2, NSA_KV_HEADS, N_QCHUNK, 8, QCHUNK), F32),
                        pltpu.VMEM((2, NSA_KV_HEADS, N_QCHUNK, NSA_HEAD_DIM + 16, QCHUNK), F32)],
        compiler_params=_cparams(("parallel", "arbitrary")),
        name="nsa_prompt",
    )(qT, qrT, gT, kc, vcT, ks, vsT, kw, vwT, smap_t, econst, wn)


def _merge_kernel(x_ref, mg_ref, ys_ref, bn_ref, ws_ref, wo_ref, g_ref, b_ref, h_ref):
    mg = mg_ref[...]
    b_ssm = _dot(ys_ref[...], ws_ref[...])
    mix = jax.nn.sigmoid(mg[:, :D_MODEL]) * b_ssm + jax.nn.sigmoid(mg[:, D_MODEL:]) * bn_ref[...]
    pre = DEEPNORM_ALPHA * x_ref[...] + _dot(_bf(mix), wo_ref[...])
    h_ref[...] = _layer_norm(pre, g_ref[...], b_ref[...])


def _merge(x, proj, y_ssm, b_nsa, ws, wo, g, b, tm):
    m = x.shape[0]
    rowblk = lambda shape, c=0: pl.BlockSpec(shape, lambda i, c=c: (i, c))
    const = lambda shape: pl.BlockSpec(shape, lambda i: (0, 0))
    return pl.pallas_call(
        _merge_kernel,
        grid=(m // tm,),
        in_specs=[rowblk((tm, D_MODEL)), rowblk((tm, 2 * D_MODEL), COL_MG // (2 * D_MODEL)),
                  rowblk((tm, SSM_D_INNER)), rowblk((tm, D_MODEL)),
                  const((SSM_D_INNER, D_MODEL)), const((D_MODEL, D_MODEL)), const((1, D_MODEL)), const((1, D_MODEL))],
        out_specs=rowblk((tm, D_MODEL)),
        out_shape=jax.ShapeDtypeStruct((m, D_MODEL), F32),
        compiler_params=_cparams(("parallel",)),
        name="merge_ln1",
    )(x, proj, y_ssm, b_nsa, ws, wo, g, b)


def _ffn_kernel(h_ref, wg_ref, wu_ref, wd_ref, g_ref, b_ref, y_ref, acc_s):
    h = h_ref[...]
    hb = _bf(h)
    acc_s[...] = jnp.zeros_like(acc_s)

    def body(c, carry):
        gate = _dot(hb, wg_ref[c])
        up = _dot(hb, wu_ref[c])
        acc_s[...] += _dot(_bf(_silu(gate) * up), wd_ref[c])
        return carry

    lax.fori_loop(0, wg_ref.shape[0], body, 0)
    y_ref[...] = _layer_norm(DEEPNORM_ALPHA * h + acc_s[...], g_ref[...], b_ref[...])


def _ffn(h, wg3, wu3, wd3, g, b, tm):
    m = h.shape[0]
    nc = wg3.shape[0]
    const2 = lambda shape: pl.BlockSpec(shape, lambda i: (0, 0))
    const3 = lambda shape: pl.BlockSpec(shape, lambda i: (0, 0, 0))
    return pl.pallas_call(
        _ffn_kernel,
        grid=(m // tm,),
        in_specs=[pl.BlockSpec((tm, D_MODEL), lambda i: (i, 0)),
                  const3((nc, D_MODEL, FFN_CHUNK)), const3((nc, D_MODEL, FFN_CHUNK)), const3((nc, FFN_CHUNK, D_MODEL)),
                  const2((1, D_MODEL)), const2((1, D_MODEL))],
        out_specs=pl.BlockSpec((tm, D_MODEL), lambda i: (i, 0)),
        out_shape=jax.ShapeDtypeStruct((m, D_MODEL), F32),
        scratch_shapes=[pltpu.VMEM((tm, D_MODEL), F32)],
        compiler_params=_cparams(("parallel",)),
        name="ffn_ln2",
    )(h, wg3, wu3, wd3, g, b)


def _ssd_s_pre_kernel(xbc_ref, cst_ref, sm_ref, cw_ref, cb_ref, dtb_ref, alog_ref, e_ref,
                      xs_ref, xdt_ref, bm_ref, cm_ref, dec_ref, ncv_ref):
    xbc = xbc_ref[...]
    acc = cb_ref[...] + cw_ref[SSM_CONV_W - 1:SSM_CONV_W, :] * xbc
    for k in range(SSM_CONV_W - 1):
        acc = acc + cw_ref[k:k + 1, :] * cst_ref[k]
    act = _silu(acc)
    for k in range(SSM_CONV_W - 2):
        ncv_ref[k] = cst_ref[k + 1]
    ncv_ref[SSM_CONV_W - 2] = xbc
    lane = lax.broadcasted_iota(jnp.int32, (1, LANES), 1)
    a_full = jnp.where(lane < SSM_HEADS, -jnp.exp(alog_ref[...]), 0.0)
    dt = _softplus(sm_ref[...] + dtb_ref[...])
    dec_ref[...] = jnp.exp(dt * a_full)
    xs = act[:, :SSM_D_INNER]
    xs_ref[...] = xs
    xdt_ref[...] = xs * _dot_exact_lhs(dt, e_ref[...])
    bm_ref[...] = act[:, SSM_D_INNER:SSM_D_INNER + SSM_GROUPS * SSM_D_STATE]
    cm_ref[...] = act[:, SSM_D_INNER + SSM_GROUPS * SSM_D_STATE:]


def _ssd_s_pre(proj_s, cst, cw, cb, dtb_pad, alog_pad):
    n = proj_s.shape[0]
    e = np.zeros((LANES, SSM_D_INNER), np.float32)
    for h in range(SSM_HEADS):
        e[h, h * SSM_HEAD_DIM:(h + 1) * SSM_HEAD_DIM] = 1.0
    gw = SSM_GROUPS * SSM_D_STATE
    full = lambda shape: pl.BlockSpec(shape, lambda i: (0,) * len(shape))
    return pl.pallas_call(
        _ssd_s_pre_kernel,
        grid=(1,),
        in_specs=[pl.BlockSpec((n, SSM_CONV_DIM), lambda i: (0, COL_XBC // SSM_CONV_DIM)),
                  full(cst.shape),
                  pl.BlockSpec((n, LANES), lambda i: (0, COL_SMALL // LANES)),
                  full(cw.shape), full(cb.shape), full((1, LANES)), full((1, LANES)), full(e.shape)],
        out_specs=[full((n, SSM_D_INNER)), full((n, SSM_D_INNER)), full((n, gw)), full((n, gw)),
                   full((n, LANES)), full(cst.shape)],
        out_shape=[jax.ShapeDtypeStruct((n, SSM_D_INNER), F32), jax.ShapeDtypeStruct((n, SSM_D_INNER), F32),
                   jax.ShapeDtypeStruct((n, gw), F32), jax.ShapeDtypeStruct((n, gw), F32),
                   jax.ShapeDtypeStruct((n, LANES), F32), jax.ShapeDtypeStruct(cst.shape, F32)],
        compiler_params=_cparams(("arbitrary",)),
        name="ssd_sample_pre",
    )(proj_s, cst, proj_s, cw, cb, dtb_pad, alog_pad, jnp.asarray(e, dtype=BF16))


def _dyn_row(ref, b, cols=slice(None)):
    tile = ref[pl.ds(pl.multiple_of((b >> 3) << 3, 8), 8), cols]
    r = lax.broadcasted_iota(jnp.int32, (8, 1), 0)
    return jnp.sum(jnp.where(r == (b & 7), tile, 0.0), axis=0, keepdims=True)


def _onehot_cols(b, n):
    return jnp.where(lax.broadcasted_iota(jnp.int32, (n, n), 0) == b, 1.0, 0.0).astype(BF16)


def _ssd_s_state_kernel(st_ref, xdt_ref, bm_ref, cm_ref, dec_ref, nst_ref, yT_ref, xT_s, dT_s, cT_s):
    b = pl.program_id(0)
    n = xdt_ref.shape[0]

    @pl.when(b == 0)
    def _():
        for i, part in enumerate(_split3(xdt_ref[...].T)):
            xT_s[i] = part
        for i, part in enumerate(_split3(dec_ref[...].T)):
            dT_s[i] = part
        cT_s[...] = _bf(cm_ref[...].T)
        yT_ref[...] = jnp.zeros_like(yT_ref)

    hb = _onehot_cols(b, n)
    r = _dot(xT_s[0], hb) + _dot(xT_s[1], hb) + _dot(xT_s[2], hb)
    dec_r = _dot(dT_s[0], hb) + _dot(dT_s[1], hb) + _dot(dT_s[2], hb)
    c_r = _bf(_dot(cT_s[...], hb))
    lane_is_b = lax.broadcasted_iota(jnp.int32, (1, n), 1) == b
    hpg = SSM_HEADS // SSM_GROUPS
    brow_all = _dyn_row(bm_ref, b)
    for h in range(SSM_HEADS):
        g = h // hpg
        brow = brow_all[:, g * SSM_D_STATE:(g + 1) * SSM_D_STATE]
        rows = slice(h * SSM_HEAD_DIM, (h + 1) * SSM_HEAD_DIM)
        new = st_ref[0, h] * dec_r[h:h + 1, :] + r[rows, :] * brow
        nst_ref[0, h] = new
        y_h = _dot(_bf(new), c_r[g * SSM_D_STATE:(g + 1) * SSM_D_STATE, :])
        yT_ref[rows, :] = jnp.where(lane_is_b, y_h, yT_ref[rows, :])


def _ssd_s_state(state, xdt, bm, cm, dec):
    n = xdt.shape[0]
    assert n == LANES and SSM_D_STATE == LANES
    full = lambda a: pl.BlockSpec(a.shape, lambda b: (0, 0))
    blk = pl.BlockSpec((1, SSM_HEADS, SSM_HEAD_DIM, SSM_D_STATE), lambda b: (b, 0, 0, 0))
    return pl.pallas_call(
        _ssd_s_state_kernel,
        grid=(n,),
        in_specs=[blk, full(xdt), full(bm), full(cm), full(dec)],
        out_specs=[blk, pl.BlockSpec((SSM_D_INNER, n), lambda b: (0, 0))],
        out_shape=[jax.ShapeDtypeStruct(state.shape, F32), jax.ShapeDtypeStruct((SSM_D_INNER, n), F32)],
        scratch_shapes=[pltpu.VMEM((3, SSM_D_INNER, n), BF16), pltpu.VMEM((3, LANES, n), BF16),
                        pltpu.VMEM((SSM_GROUPS * SSM_D_STATE, n), BF16)],
        compiler_params=_cparams(("arbitrary",)),
        name="ssd_sample_state",
    )(state, xdt, bm, cm, dec)


def _ssd_s_post_kernel(yT_ref, xs_ref, z_ref, drow_ref, nw_ref, o_ref):
    y = yT_ref[...].T + drow_ref[...] * xs_ref[...]
    v = y * _silu(z_ref[...])
    gw = SSM_D_INNER // SSM_GROUPS
    outs = []
    for g in range(SSM_GROUPS):
        vg = v[:, g * gw:(g + 1) * gw]
        ms = jnp.sum(vg * vg, axis=-1, keepdims=True) * (1.0 / gw)
        outs.append(vg * lax.rsqrt(ms + NORM_EPS) * nw_ref[:, g * gw:(g + 1) * gw])
    o_ref[...] = _bf(jnp.concatenate(outs, axis=1))


def _ssd_s_post(yT, xs, proj_s, drow, nw):
    n = xs.shape[0]
    full = lambda shape: pl.BlockSpec(shape, lambda i: (0, 0))
    return pl.pallas_call(
        _ssd_s_post_kernel,
        grid=(1,),
        in_specs=[full(yT.shape), full(xs.shape),
                  pl.BlockSpec((n, SSM_D_INNER), lambda i: (0, COL_Z // SSM_D_INNER)),
                  full(drow.shape), full(nw.shape)],
        out_specs=full((n, SSM_D_INNER)),
        out_shape=jax.ShapeDtypeStruct((n, SSM_D_INNER), BF16),
        compiler_params=_cparams(("arbitrary",)),
        name="ssd_sample_post",
    )(yT, xs, proj_s, drow, nw)


PAGES_PER_STEP = 16


def _q_block(q_ref, b, n):
    r_q = _dot(q_ref[0], _onehot_cols(b, n))
    lane = lax.broadcasted_iota(jnp.int32, (1, n), 1)
    blocks = []
    for h in range(NSA_KV_HEADS):
        blk = None
        for g in range(NSA_GQA):
            hd = h * NSA_GQA + g
            piece = jnp.where(lane == hd, r_q[hd * NSA_HEAD_DIM:(hd + 1) * NSA_HEAD_DIM, :], 0.0)
            blk = piece if blk is None else blk + piece
        blocks.append(blk)
    return _bf(jnp.concatenate(blocks, axis=0))


def _scatter_heads(o_t, out_ref, b, n):
    lane = lax.broadcasted_iota(jnp.int32, (1, n), 1)
    lane_is_b = lane == b
    for hd in range(NSA_HEADS):
        h = hd // NSA_GQA
        piece = o_t[h * NSA_HEAD_DIM:(h + 1) * NSA_HEAD_DIM, :]
        col = jnp.sum(jnp.where(lane == hd, piece, 0.0), axis=1, keepdims=True)
        rows = slice(hd * NSA_HEAD_DIM, (hd + 1) * NSA_HEAD_DIM)
        out_ref[rows, :] = jnp.where(lane_is_b, col, out_ref[rows, :])


def _nsa_s_cmp_kernel(pt_ref, *refs):
    pages = refs[:PAGES_PER_STEP]
    (qT_ref, wcat_ref, w2bd_ref, bias_ref, perm_ref, smapT_ref, gsum_ref,
     otc_ref, sel_ref, xcat_s, pab_s) = refs[PAGES_PER_STEP:]
    b = pl.program_id(0)
    j = pl.program_id(1)
    nj = pl.num_programs(1)
    n = qT_ref.shape[2]
    ns = CMP_CHUNK // CMP_STRIDE
    sub_per_pair = 2 * PAGE_SIZE // CMP_STRIDE
    for i in range(PAGES_PER_STEP // 2):
        pair = _bf(jnp.concatenate([pages[2 * i][0], pages[2 * i + 1][0]], axis=1))
        t_perm = _bf(_dot_nt(perm_ref[...], pair))
        for lp in range(CMP_STRIDE):
            for v in range(2):
                xcat_s[v, i * sub_per_pair:(i + 1) * sub_per_pair, lp * KV_HALF:(lp + 1) * KV_HALF] = (
                    t_perm[lp * sub_per_pair:(lp + 1) * sub_per_pair, v * KV_HALF:(v + 1) * KV_HALF])
    for v in range(2):
        pab_s[v, pl.ds(pl.multiple_of(j * ns, ns), ns), :] = _dot(xcat_s[v], wcat_ref[v])

    @pl.when((b == 0) & (j == 0))
    def _():
        otc_ref[...] = jnp.zeros_like(otc_ref)

    @pl.when(j == nj - 1)
    def _():
        kc, vc = _compress_finish(pab_s, bias_ref[...], w2bd_ref)
        n_sub = kc.shape[0]
        past_len = n_sub * CMP_STRIDE
        qblk = _q_block(qT_ref, b, n)
        crow = lax.broadcasted_iota(jnp.int32, (n_sub, 1), 0)
        valid = (crow * CMP_STRIDE + (CMP_LEN - 1)) <= past_len
        sc = jnp.where(valid, _dot(_bf(kc), qblk), NEG)
        m = jnp.max(sc, axis=0, keepdims=True)
        e = jnp.where(valid, jnp.exp2(sc - m), 0.0)
        p = e * (1.0 / jnp.maximum(jnp.sum(e, axis=0, keepdims=True), 1e-30))
        _scatter_heads(_dot(_bf(vc.T), _bf(p)), otc_ref, b, n)
        p_hi = _bf(p)
        p_lo = _bf(p - p_hi.astype(F32))
        psum = _dot(p_hi, gsum_ref[...]) + _dot(p_lo, gsum_ref[...])
        q_hi = _bf(psum)
        q_lo = _bf(psum - q_hi.astype(F32))
        imp_t = _dot(smapT_ref[...], q_hi) + _dot(smapT_ref[...], q_lo)
        nj_pad = imp_t.shape[0]
        n_sel = past_len // SEL_BLOCK + 1
        cur = float(past_len // SEL_BLOCK)
        jrow = lax.broadcasted_iota(jnp.int32, (nj_pad, n), 0).astype(F32)
        forced = (jnp.where(jrow == 0.0, 1.0, 0.0) + jnp.where(jrow == cur, 1.0, 0.0)
                  + jnp.where(jrow == cur - 1.0, 1.0, 0.0))
        score = jnp.where(jrow > cur, -BIG, jnp.where(forced > 0.5, BIG, imp_t))
        score = jnp.where(jrow >= float(n_sel), -jnp.inf, score)
        sel_ref[0] = _topk_mask(score, jrow, min(SEL_TOP_N, n_sel))


def _page_specs(n_pages_step):
    return [pl.BlockSpec((1, KV_COLS, PAGE_SIZE),
                         functools.partial(lambda b, j, pt, i: (pt[b, j * n_pages_step + i], 0, 0), i=i))
            for i in range(n_pages_step)]


def _nsa_s_cmp(page_table, cache_cmp, qT, wcat, w2bd, bias, past_len):
    nseq = page_table.shape[0]
    pair_tok = 2 * PAGE_SIZE
    r = np.arange(pair_tok)
    perm = jnp.asarray(r[None, :] == ((r % (pair_tok // CMP_STRIDE)) * CMP_STRIDE + r // (pair_tok // CMP_STRIDE))[:, None],
                       dtype=BF16)
    n_sub = past_len // CMP_STRIDE
    n_cmp = n_sub - CMP_LEN // CMP_STRIDE + 1
    n_sel = past_len // SEL_BLOCK + 1
    nj_pad = -(-n_sel // 8) * 8
    smap = np.zeros((nj_pad, n_sub), bool)
    smap[:n_sel] = _selection_map_t(n_sel, n_sub) & (np.arange(n_sub)[None, :] < n_cmp)
    lanes = np.arange(nseq)
    gsum = ((lanes[:, None] // NSA_GQA) == (lanes[None, :] // NSA_GQA)) & (lanes[:, None] < NSA_HEADS) & (lanes[None, :] < NSA_HEADS)
    nj = past_len // CMP_CHUNK
    ns = CMP_CHUNK // CMP_STRIDE
    c2 = lambda a: pl.BlockSpec(a.shape, lambda b, j, pt: (0, 0))
    c3 = lambda a: pl.BlockSpec(a.shape, lambda b, j, pt: (0, 0, 0))
    smap_j = jnp.asarray(smap, dtype=BF16)
    gsum_j = jnp.asarray(gsum, dtype=BF16)
    grid_spec = pltpu.PrefetchScalarGridSpec(
        num_scalar_prefetch=1,
        grid=(nseq, nj),
        in_specs=_page_specs(PAGES_PER_STEP) + [c3(qT), c3(wcat), c3(w2bd), c2(bias), c2(perm), c2(smap_j), c2(gsum_j)],
        out_specs=[pl.BlockSpec((D_MODEL, nseq), lambda b, j, pt: (0, 0)),
                   pl.BlockSpec((1, nj_pad, nseq), lambda b, j, pt: (b, 0, 0))],
        scratch_shapes=[pltpu.VMEM((2, ns, CMP_STRIDE * KV_HALF), BF16),
                        pltpu.VMEM((2, n_sub, KV_COLS), F32)],
    )
    return pl.pallas_call(
        _nsa_s_cmp_kernel,
        grid_spec=grid_spec,
        out_shape=[jax.ShapeDtypeStruct((D_MODEL, nseq), F32),
                   jax.ShapeDtypeStruct((nseq, nj_pad, nseq), F32)],
        compiler_params=_cparams(("arbitrary", "arbitrary")),
        name="nsa_sample_cmp",
    )(page_table, *([cache_cmp] * PAGES_PER_STEP), qT, wcat, w2bd, bias, perm, smap_j, gsum_j)


def _nsa_s_att_kernel(n_past_blk, pt_ref, *refs):
    pages = refs[:PAGES_PER_STEP]
    (qrT_ref, sel_ref, nselT_ref, win_ref, nwinT_ref,
     ots_ref, otw_ref, nwo_ref, qblk_s, m_s, acc_s) = refs[PAGES_PER_STEP:]
    b = pl.program_id(0)
    j = pl.program_id(1)
    nj = pl.num_programs(1)
    n = qrT_ref.shape[2]
    blk_per_step = PAGES_PER_STEP * PAGE_SIZE // SEL_BLOCK

    @pl.when((b == 0) & (j == 0))
    def _():
        ots_ref[...] = jnp.zeros_like(ots_ref)
        otw_ref[...] = jnp.zeros_like(otw_ref)

    @pl.when(j == 0)
    def _():
        qblk_s[...] = _q_block(qrT_ref, b, n)
        m_s[...] = jnp.full(m_s.shape, NEG, F32)
        acc_s[...] = jnp.zeros_like(acc_s)

    qblk = qblk_s[...]

    def keys_update(kv_t, mask, m, acc):
        st = jnp.where(mask, _dot_tn(_bf(kv_t[:KV_HALF, :]), qblk), NEG)
        vt = jnp.concatenate([_bf(kv_t[KV_HALF:, :]), jnp.ones((16, kv_t.shape[1]), BF16)], axis=0)
        return _online_step(st, vt, m, acc)

    lane_n = lax.broadcasted_iota(jnp.int32, (1, n), 1)
    key_is0 = lax.broadcasted_iota(jnp.int32, (PAGE_SIZE, 1), 0) == 0

    def new_token_tile(ref):
        col = jnp.sum(jnp.where(lane_n == b, ref[...], 0.0), axis=1, keepdims=True)
        return jnp.where(lax.broadcasted_iota(jnp.int32, (1, PAGE_SIZE), 1) == 0, col, 0.0), col

    kv_step = jnp.concatenate([pages[i][0] for i in range(PAGES_PER_STEP)], axis=1)
    sel_rows = sel_ref[0, pl.ds(pl.multiple_of(j * blk_per_step, blk_per_step), blk_per_step), :]
    mask = jnp.concatenate([jnp.broadcast_to(sel_rows[r:r + 1, :], (SEL_BLOCK, n))
                            for r in range(blk_per_step)], axis=0) > 0.5
    m, acc = keys_update(kv_step, mask, m_s[...], acc_s[...])
    m_s[...] = m
    acc_s[...] = acc

    @pl.when(j == nj - 1)
    def _():
        kv_new, _ = new_token_tile(nselT_ref)
        sel_new = sel_ref[0, n_past_blk:n_past_blk + 1, :] > 0.5
        m2, acc2 = keys_update(kv_new, key_is0 & sel_new, m_s[...], acc_s[...])
        _scatter_heads(acc2[0:KV_HALF, :] / acc2[KV_HALF:KV_HALF + 1, :], ots_ref, b, n)
        win_t = win_ref[0]
        w_keys = win_t.shape[1]
        kv_new, new_col = new_token_tile(nwinT_ref)
        valid = jnp.concatenate([jnp.full((w_keys, 1), True), key_is0], axis=0)
        _, aw = keys_update(jnp.concatenate([win_t, kv_new], axis=1), valid,
                            jnp.full(m_s.shape, NEG, F32), jnp.zeros_like(acc2))
        _scatter_heads(aw[0:KV_HALF, :] / aw[KV_HALF:KV_HALF + 1, :], otw_ref, b, n)
        lane_w = lax.broadcasted_iota(jnp.int32, (1, w_keys), 1)
        nwo_ref[0] = jnp.where(lane_w == w_keys - 1, new_col, pltpu.roll(win_t, w_keys - 1, 1))


def _nsa_s_att(page_table, cache_sel, qrT, sel, nsel_rows, cache_win, nwin_rows, past_len):
    nseq = page_table.shape[0]
    nj = past_len // (PAGES_PER_STEP * PAGE_SIZE)
    wbuf = cache_win.shape[2]
    c2 = lambda a: pl.BlockSpec(a.shape, lambda b, j, pt: (0, 0))
    c3 = lambda a: pl.BlockSpec(a.shape, lambda b, j, pt: (0, 0, 0))
    acc_rows = KV_HALF + 16
    grid_spec = pltpu.PrefetchScalarGridSpec(
        num_scalar_prefetch=1,
        grid=(nseq, nj),
        in_specs=_page_specs(PAGES_PER_STEP) + [
            c3(qrT),
            pl.BlockSpec((1,) + sel.shape[1:], lambda b, j, pt: (b, 0, 0)),
            c2(nsel_rows),
            pl.BlockSpec((1, KV_COLS, wbuf), lambda b, j, pt: (b, 0, 0)),
            c2(nwin_rows)],
        out_specs=[pl.BlockSpec((D_MODEL, nseq), lambda b, j, pt: (0, 0)),
                   pl.BlockSpec((D_MODEL, nseq), lambda b, j, pt: (0, 0)),
                   pl.BlockSpec((1, KV_COLS, wbuf), lambda b, j, pt: (b, 0, 0))],
        scratch_shapes=[pltpu.VMEM((KV_HALF, nseq), BF16), pltpu.VMEM((1, nseq), F32),
                        pltpu.VMEM((acc_rows, nseq), F32)],
    )
    return pl.pallas_call(
        functools.partial(_nsa_s_att_kernel, past_len // SEL_BLOCK),
        grid_spec=grid_spec,
        out_shape=[jax.ShapeDtypeStruct((D_MODEL, nseq), F32), jax.ShapeDtypeStruct((D_MODEL, nseq), F32),
                   jax.ShapeDtypeStruct((nseq, KV_COLS, wbuf), F32)],
        compiler_params=_cparams(("arbitrary", "arbitrary")),
        name="nsa_sample_att",
    )(page_table, *([cache_sel] * PAGES_PER_STEP), qrT, sel, nsel_rows, cache_win, nwin_rows)


def _nsa_s_out_kernel(otc_ref, ots_ref, otw_ref, gT_ref, wn_ref, o_ref):
    n = otc_ref.shape[1]
    parts = []
    for hd in range(NSA_HEADS):
        rows = slice(hd * NSA_HEAD_DIM, (hd + 1) * NSA_HEAD_DIM)
        parts.append(gT_ref[0, 3 * hd:3 * hd + 1, :] * otc_ref[rows, :]
                     + gT_ref[0, 3 * hd + 1:3 * hd + 2, :] * ots_ref[rows, :]
                     + gT_ref[0, 3 * hd + 2:3 * hd + 3, :] * otw_ref[rows, :])
    o_t = _bf(jnp.concatenate(parts, axis=0))
    o_ref[...] = _dot_tn(o_t, wn_ref[...])


def _nsa_s_out(otc, ots, otw, gT, wn):
    n = otc.shape[1]
    f2 = lambda a: pl.BlockSpec(a.shape, lambda i: (0, 0))
    f3 = lambda a: pl.BlockSpec(a.shape, lambda i: (0, 0, 0))
    return pl.pallas_call(
        _nsa_s_out_kernel,
        grid=(1,),
        in_specs=[f2(otc), f2(ots), f2(otw), f3(gT), f2(wn)],
        out_specs=pl.BlockSpec((n, D_MODEL), lambda i: (0, 0)),
        out_shape=jax.ShapeDtypeStruct((n, D_MODEL), F32),
        compiler_params=_cparams(("arbitrary",)),
        name="nsa_sample_out",
    )(otc, ots, otw, gT, wn)


def _rope_tables(pos):
    half = ROPE_DIM // 2
    inv_freq = jnp.power(ROPE_THETA, -jnp.arange(half, dtype=F32) * 2.0 / ROPE_DIM)
    ang = pos.astype(F32)[None, :] * inv_freq[:, None]
    return jnp.cos(ang), jnp.sin(ang)


def _permute_w_in(w_in):
    sizes = (SSM_D_INNER, SSM_CONV_DIM, SSM_HEADS, NSA_HEADS * NSA_HEAD_DIM, KV_COLS, KV_COLS, KV_COLS,
             3 * NSA_HEADS, 2 * D_MODEL)
    offs = np.concatenate([[0], np.cumsum(sizes)])
    z, xbc, dt, q, kvc, kvs, kvw, ng, mg = (w_in[:, offs[i]:offs[i + 1]] for i in range(len(sizes)))
    pad = jnp.zeros((D_MODEL, PROJ_COLS - COL_SMALL - SSM_HEADS - 3 * NSA_HEADS), w_in.dtype)
    return _bf(jnp.concatenate([xbc, q, z, mg, kvc, kvs, kvw, dt, ng, pad], axis=1))


def _compress_weights(cmp_pe, cmp_w1, cmp_w2):
    eye = jnp.eye(NSA_KV_HEADS, dtype=F32)
    w1a = cmp_w1[:, :CMP_STRIDE]
    w1b = cmp_w1[:, CMP_STRIDE:]
    bd = lambda w: jnp.einsum('vlde,hk->vlhdke', w, eye).reshape(2, CMP_STRIDE * KV_HALF, KV_HALF)
    wcat = _bf(jnp.concatenate([bd(w1a), bd(w1b)], axis=2))
    w2bd = _bf(jnp.einsum('vef,hk->vhekf', cmp_w2, eye).reshape(2, KV_HALF, KV_HALF))
    pe8 = jnp.zeros((2, 8, CMP_LEN * NSA_HEAD_DIM), F32).at[:, 0, :].set(cmp_pe.reshape(2, -1))
    w1f4 = jnp.tile(cmp_w1.reshape(2, CMP_LEN * NSA_HEAD_DIM, NSA_HEAD_DIM), (1, 1, NSA_KV_HEADS))
    return wcat, w2bd, pe8, w1f4


def kernel(x_prompt, x_sample, cache_cmp_kv, cache_sel_kv, cache_win_kv, state_ssm, state_conv, page_table,
           w_in, conv_w, conv_b, dt_bias, a_log, d_skip, ssm_norm_w, w_ssm_out, cmp_pe, cmp_w1, cmp_w2,
           w_nsa_out, w_o, ln1_g, ln1_b, w_gate, w_up, w_down, ln2_g, ln2_b):
    bsz, t, _ = x_prompt.shape
    nseq, dec_seq, _ = x_sample.shape
    n_pool = cache_cmp_kv.shape[1]
    past_len = page_table.shape[1] * PAGE_SIZE
    assert w_in.shape[0] == 1 and dec_seq == 1 and nseq == LANES
    assert t % CMP_CHUNK == 0 and past_len % CMP_CHUNK == 0 and cache_win_kv.shape[2] == WINDOW

    w_in_p = _permute_w_in(w_in[0])
    pad_row = lambda v: jnp.zeros((1, LANES), F32).at[0, :SSM_HEADS].set(v)
    dtb_pad, alog_pad = pad_row(dt_bias[0]), pad_row(a_log[0])
    drow = jnp.repeat(d_skip[0], SSM_HEAD_DIM)[None, :]
    nw = ssm_norm_w[0][None, :]
    cw, cb = conv_w[0], conv_b[0][None, :]
    wcat, w2bd, pe8, w1f4 = _compress_weights(cmp_pe[0], cmp_w1[0], cmp_w2[0])
    w_ssm_b, w_nsa_b, w_o_b = _bf(w_ssm_out[0]), _bf(w_nsa_out[0]), _bf(w_o[0])
    nc = FFN_HIDDEN // FFN_CHUNK
    wg3 = _bf(w_gate[0]).reshape(D_MODEL, nc, FFN_CHUNK).transpose(1, 0, 2)
    wu3 = _bf(w_up[0]).reshape(D_MODEL, nc, FFN_CHUNK).transpose(1, 0, 2)
    wd3 = _bf(w_down[0]).reshape(nc, FFN_CHUNK, D_MODEL)
    g1, b1, g2, b2 = ln1_g[0][None, :], ln1_b[0][None, :], ln2_g[0][None, :], ln2_b[0][None, :]

    xp = x_prompt.reshape(bsz * t, D_MODEL)
    proj = _matmul(xp, w_in_p, 2048, PROJ_TN, "in_proj")
    y_ssm, new_ssm_p, new_conv_p = _ssd_prompt(proj, bsz, t, cw, cb, dtb_pad, alog_pad, drow, nw)
    cos_p, sin_p = _rope_tables(jnp.arange(t, dtype=jnp.int32))
    qT, qrT, ks, vsT, kw, vwT, gT, ncmp, nsel, nwin = _attn_prep(proj, bsz, t, KEY_SLAB, cos_p, sin_p)
    cmp_bias = _pe_bias(pe8, w1f4)
    kc, vcT = _compress_prompt(proj, bsz, t, wcat, w2bd, cmp_bias)
    b_nsa = _nsa_prompt(qT, qrT, gT, kc, vcT, ks, vsT, kw, vwT, w_nsa_b, bsz, t)
    h = _merge(xp, proj, y_ssm, b_nsa, w_ssm_b, w_o_b, g1, b1, 512)
    y_p = _ffn(h, wg3, wu3, wd3, g2, b2, 512).reshape(bsz, t, D_MODEL)

    def kv6(a_t):
        n_b, _, n_t = a_t.shape
        return jnp.moveaxis(a_t.reshape(n_b, 2, NSA_KV_HEADS, NSA_HEAD_DIM, n_t), -1, 1)[None]

    w_keep = min(WINDOW, t)
    new_win_p = kv6(nwin[:, :, t - w_keep:])

    xs_in = x_sample.reshape(nseq, D_MODEL)
    proj_s = _matmul(xs_in, w_in_p, nseq, PROJ_TN, "in_proj_s")
    cst = jnp.moveaxis(state_conv[0], 1, 0)
    xs_s, xdt_s, bm_s, cm_s, dec_s, ncv_s = _ssd_s_pre(proj_s, cst, cw, cb, dtb_pad, alog_pad)
    new_ssm_s, yT_s = _ssd_s_state(state_ssm[0], xdt_s, bm_s, cm_s, dec_s)
    y_ssm_s = _ssd_s_post(yT_s, xs_s, proj_s, drow, nw)
    cos_s, sin_s = _rope_tables(jnp.full((nseq,), past_len, dtype=jnp.int32))
    qT_s, qrT_s, _, _, _, _, gT_s, ncmp_s, nsel_s, nwin_s = _attn_prep(proj_s, 1, nseq, nseq, cos_s, sin_s)
    fmaj = lambda c, n_lead: jnp.moveaxis(c, 1, -1).reshape(n_lead, KV_COLS, c.shape[1])
    cache_cmp = fmaj(cache_cmp_kv[0], n_pool)
    cache_sel = fmaj(cache_sel_kv[0], n_pool)
    cache_win = fmaj(cache_win_kv[0], nseq)
    otc, sel = _nsa_s_cmp(page_table, cache_cmp, qT_s, wcat, w2bd, cmp_bias, past_len)
    ots, otw, new_win_t = _nsa_s_att(page_table, cache_sel, qrT_s, sel, nsel_s[0], cache_win, nwin_s[0], past_len)
    kv6_s = lambda a_t: kv6(jnp.transpose(a_t, (2, 1, 0)))
    b_nsa_s = _nsa_s_out(otc, ots, otw, gT_s, w_nsa_b)
    h_s = _merge(xs_in, proj_s, y_ssm_s, b_nsa_s, w_ssm_b, w_o_b, g1, b1, nseq)
    y_s = _ffn(h_s, wg3, wu3, wd3, g2, b2, nseq).reshape(nseq, 1, D_MODEL)

    return (y_p, y_s,
            kv6(ncmp), kv6(nsel), new_win_p,
            new_ssm_p[None], new_conv_p[None],
            kv6_s(ncmp_s), kv6_s(nsel_s),
            kv6(new_win_t),
            new_ssm_s[None],
            jnp.moveaxis(ncv_s, 0, 1)[None])
```

```python
import functools
import math

import numpy as np
import jax
import jax.numpy as jnp
from jax import lax
from jax.experimental import pallas as pl
from jax.experimental.pallas import tpu as pltpu

F32 = jnp.float32
BF16 = jnp.bfloat16

D_MODEL = 1024
SSM_D_INNER = 2048
SSM_HEAD_DIM = 64
SSM_HEADS = 32
SSM_GROUPS = 4
SSM_D_STATE = 128
SSM_CONV_W = 4
SSM_CONV_DIM = 3072
SSM_CHUNK = 128
NSA_HEADS = 16
NSA_KV_HEADS = 4
NSA_HEAD_DIM = 64
NSA_GQA = 4
KV_COLS = 512
KV_HALF = 256
CMP_LEN = 32
CMP_STRIDE = 16
SEL_BLOCK = 64
SEL_TOP_N = 16
WINDOW = 512
Q_BLOCK = 128
ROPE_DIM = 16
ROPE_THETA = 500000.0
FFN_HIDDEN = 2816
NORM_EPS = 1e-5
BIG = 1e30
NEG = -1e30
DEPTH = 1
DEEPNORM_ALPHA = (2 * DEPTH) ** 0.25
PAGE_SIZE = 128

LANES = 128
VMEM_LIMIT_BYTES = 56 * 1024 * 1024

COL_XBC = 0
COL_Q = 3072
COL_Z = 4096
COL_MG = 6144
COL_KVC = 8192
COL_KVS = 8704
COL_KVW = 9216
COL_SMALL = 9728
PROJ_COLS = 9984
PROJ_TN = 768

Q_SCALE = NSA_HEAD_DIM ** -0.5 * math.log2(math.e)
KEY_SLAB = 512
BLOCKS_PER_SLAB = KEY_SLAB // SEL_BLOCK
CHUNKS_PER_SLAB = KEY_SLAB // Q_BLOCK
CHUNK_SHIFT = 2
assert 1 << CHUNK_SHIFT == CHUNKS_PER_SLAB
QCHUNK = 256
N_QCHUNK = NSA_GQA * Q_BLOCK // QCHUNK
SLABS_PER_TRIP = 2
assert SLABS_PER_TRIP & (SLABS_PER_TRIP - 1) == 0
CHUNKS_PER_ITEM = 1
MXU_LOOKAHEAD = 12
CMP_CHUNK = 2048
FFN_CHUNK = 256


def _dot_dims(a, b, dims):
    return lax.dot_general(a, b, (dims, ((), ())), preferred_element_type=F32)


def _dot(a, b):
    return _dot_dims(a, b, ((1,), (0,)))


def _dot_nt(a, b):
    return _dot_dims(a, b, ((1,), (1,)))


def _dot_tn(a, b):
    return _dot_dims(a, b, ((0,), (0,)))


def _bf(x):
    return x.astype(BF16)


def _split3(x):
    hi = _bf(x)
    r1 = x - hi.astype(F32)
    mid = _bf(r1)
    lo = _bf(r1 - mid.astype(F32))
    return hi, mid, lo


def _dot_exact_lhs(x, w_bf16):
    hi, mid, lo = _split3(x)
    return _dot(hi, w_bf16) + _dot(mid, w_bf16) + _dot(lo, w_bf16)


def _dot_exact_rhs(w_bf16, x):
    hi, mid, lo = _split3(x)
    return _dot(w_bf16, hi) + _dot(w_bf16, mid) + _dot(w_bf16, lo)


def _silu(x):
    h = 0.5 * x
    return h + h * jnp.tanh(h)


def _softplus(x):
    e = jnp.exp(-jnp.abs(x))
    u = 1.0 + e
    log1p_e = jnp.where(u == 1.0, e, jnp.log(u) * e / jnp.where(u == 1.0, 1.0, u - 1.0))
    return jnp.maximum(x, 0.0) + log1p_e


def _gelu_tanh(x):
    return 0.5 * x * (1.0 + jnp.tanh(0.7978845608028654 * (x + 0.044715 * (x * x * x))))


def _layer_norm(x, g, b):
    mu = jnp.mean(x, axis=-1, keepdims=True)
    xc = x - mu
    var = jnp.mean(xc * xc, axis=-1, keepdims=True)
    return xc * lax.rsqrt(var + NORM_EPS) * g + b


def _cparams(sem):
    return pltpu.CompilerParams(dimension_semantics=sem, vmem_limit_bytes=VMEM_LIMIT_BYTES)


def _mm_kernel(x_ref, w_ref, o_ref):
    o_ref[...] = _dot(_bf(x_ref[...]), w_ref[...])


def _matmul(x, w, tm, tn, name):
    m, k = x.shape
    n = w.shape[1]
    return pl.pallas_call(
        _mm_kernel,
        grid=(m // tm, n // tn),
        in_specs=[pl.BlockSpec((tm, k), lambda i, j: (i, 0)),
                  pl.BlockSpec((k, tn), lambda i, j: (0, j))],
        out_specs=pl.BlockSpec((tm, tn), lambda i, j: (i, j)),
        out_shape=jax.ShapeDtypeStruct((m, n), F32),
        compiler_params=_cparams(("parallel", "arbitrary")),
        name=name,
    )(x, w)


def _ssd_kernel(xbc_ref, z_ref, sm_ref, cw_ref, cb_ref, dtb_ref, alog_ref, drow_ref, nw_ref, r3_ref,
                y_ref, st_ref, cv_ref, xp_s, stT_s):
    j = pl.program_id(1)
    nj = pl.num_programs(1)
    q = SSM_CHUNK

    @pl.when(j == 0)
    def _():
        xp_s[0:8, :] = jnp.zeros((8, SSM_CONV_DIM), F32)
        stT_s[...] = jnp.zeros_like(stT_s)

    xp_s[8:8 + q, :] = xbc_ref[...]
    acc = cb_ref[...]
    for k in range(SSM_CONV_W):
        acc = acc + cw_ref[k:k + 1, :] * xp_s[5 + k:5 + k + q, :]
    act = _silu(acc)
    tail = xp_s[q + 5:q + 8, :]
    xp_s[5:8, :] = tail

    @pl.when(j == nj - 1)
    def _():
        cv_ref[0] = tail

    lane = lax.broadcasted_iota(jnp.int32, (1, LANES), 1)
    a_full = jnp.where(lane < SSM_HEADS, -jnp.exp(alog_ref[...]), 0.0)
    dt = _softplus(sm_ref[...] + dtb_ref[...])
    da = dt * a_full
    row_i = lax.broadcasted_iota(jnp.int32, (q, q), 0)
    col_i = lax.broadcasted_iota(jnp.int32, (q, q), 1)
    causal = col_i <= row_i
    tril = jnp.where(causal, 1.0, 0.0).astype(BF16)
    cum = _dot_exact_rhs(tril, da)
    cumT = cum.T
    dtT = dt.T
    hi, mid, lo = _split3(cum)
    packed = _bf(hi.astype(F32) + pltpu.roll(mid.astype(F32), 32, 1) + pltpu.roll(lo.astype(F32), 64, 1))
    lane_lo = lax.broadcasted_iota(jnp.int32, (1, LANES), 1) < SSM_HEAD_DIM

    heads_per_group = SSM_HEADS // SSM_GROUPS
    pairs_per_group = heads_per_group // 2
    n_pairs = SSM_HEADS // 2
    cmbs, cbs, bTs = [], [], []
    for g in range(SSM_GROUPS):
        bm_g = act[:, SSM_D_INNER + g * SSM_D_STATE:SSM_D_INNER + (g + 1) * SSM_D_STATE]
        cm_g = act[:, SSM_D_INNER + SSM_GROUPS * SSM_D_STATE + g * SSM_D_STATE:
                   SSM_D_INNER + SSM_GROUPS * SSM_D_STATE + (g + 1) * SSM_D_STATE]
        cmbs.append(_bf(cm_g))
        cbs.append(_dot_nt(cmbs[g], _bf(bm_g)))
        bTs.append(bm_g.T)
    cols_all = [_dot(packed, r3_ref[:, pair * 2 * LANES:(pair + 1) * 2 * LANES]) for pair in range(n_pairs)]
    y_inter = [_dot(cmbs[pair // pairs_per_group], _bf(stT_s[:, pair * LANES:(pair + 1) * LANES]))
               for pair in range(n_pairs)]

    y_parts = []
    for g in range(SSM_GROUPS):
        cmb, cb, bT = cmbs[g], cbs[g], bTs[g]
        for pp in range(pairs_per_group):
            pair = g * pairs_per_group + pp
            h0 = 2 * pair
            xs_pair = act[:, pair * LANES:(pair + 1) * LANES]
            xs_a = _bf(jnp.where(lane_lo, xs_pair, 0.0))
            xs_b = _bf(jnp.where(lane_lo, 0.0, xs_pair))
            cols2 = cols_all[pair]
            y_pair = None
            ds_pair = None
            lasts = []
            cols = []
            for hh, xs_m in ((0, xs_a), (1, xs_b)):
                h = h0 + hh
                col = cols2[:, hh * LANES:(hh + 1) * LANES]
                row = cumT[h:h + 1, :]
                dtrow = dtT[h:h + 1, :]
                dec = jnp.exp(jnp.where(causal, col - row, NEG))
                m_h = _bf(cb * dec * dtrow)
                y_h = _dot(m_h, xs_m)
                last = col[q - 1:q, :]
                wrow = jnp.exp(last - row) * dtrow
                ds_h = _dot(_bf(bT * wrow), xs_m)
                y_pair = y_h if y_pair is None else y_pair + y_h
                ds_pair = ds_h if ds_pair is None else ds_pair + ds_h
                lasts.append(last)
                cols.append(col)
            st_pair = stT_s[:, pair * LANES:(pair + 1) * LANES]
            scale_t = jnp.exp(jnp.where(lane_lo, cols[0], cols[1]))
            y_pair = y_pair + y_inter[pair] * scale_t
            stT_s[:, pair * LANES:(pair + 1) * LANES] = (
                st_pair * jnp.exp(jnp.where(lane_lo, lasts[0], lasts[1])) + ds_pair)
            y_pair = y_pair + drow_ref[:, pair * LANES:(pair + 1) * LANES] * xs_pair
            y_parts.append(y_pair)
    y = jnp.concatenate(y_parts, axis=1)
    v = y * _silu(z_ref[...])
    gw = SSM_D_INNER // SSM_GROUPS
    outs = []
    for g in range(SSM_GROUPS):
        vg = v[:, g * gw:(g + 1) * gw]
        ms = jnp.sum(vg * vg, axis=-1, keepdims=True) * (1.0 / gw)
        outs.append(vg * lax.rsqrt(ms + NORM_EPS) * nw_ref[:, g * gw:(g + 1) * gw])
    y_ref[...] = _bf(jnp.concatenate(outs, axis=1))

    @pl.when(j == nj - 1)
    def _():
        st_ref[0] = stT_s[...].T.reshape(SSM_HEADS, SSM_HEAD_DIM, SSM_D_STATE)


def _r3_table():
    k = np.arange(LANES)[:, None]
    c = np.arange(SSM_HEADS * LANES)[None, :]
    return jnp.asarray(((k % SSM_HEADS) == (c // LANES)) & (k < 3 * SSM_HEADS), dtype=BF16)


def _ssd_prompt(proj, bsz, t, cw, cb, dtb_pad, alog_pad, drow, nw):
    nch = t // SSM_CHUNK
    q = SSM_CHUNK
    row = lambda b, j: b * nch + j
    const = lambda shape: pl.BlockSpec(shape, lambda b, j: (0, 0))
    return pl.pallas_call(
        _ssd_kernel,
        grid=(bsz, nch),
        in_specs=[
            pl.BlockSpec((q, SSM_CONV_DIM), lambda b, j: (row(b, j), COL_XBC // SSM_CONV_DIM)),
            pl.BlockSpec((q, SSM_D_INNER), lambda b, j: (row(b, j), COL_Z // SSM_D_INNER)),
            pl.BlockSpec((q, LANES), lambda b, j: (row(b, j), COL_SMALL // LANES)),
            const((SSM_CONV_W, SSM_CONV_DIM)), const((1, SSM_CONV_DIM)),
            const((1, LANES)), const((1, LANES)), const((1, SSM_D_INNER)), const((1, SSM_D_INNER)),
            const((LANES, SSM_HEADS * LANES)),
        ],
        out_specs=[
            pl.BlockSpec((q, SSM_D_INNER), lambda b, j: (row(b, j), 0)),
            pl.BlockSpec((1, SSM_HEADS, SSM_HEAD_DIM, SSM_D_STATE), lambda b, j: (b, 0, 0, 0)),
            pl.BlockSpec((1, SSM_CONV_W - 1, SSM_CONV_DIM), lambda b, j: (b, 0, 0)),
        ],
        out_shape=[
            jax.ShapeDtypeStruct((bsz * t, SSM_D_INNER), BF16),
            jax.ShapeDtypeStruct((bsz, SSM_HEADS, SSM_HEAD_DIM, SSM_D_STATE), F32),
            jax.ShapeDtypeStruct((bsz, SSM_CONV_W - 1, SSM_CONV_DIM), F32),
        ],
        scratch_shapes=[pltpu.VMEM((q + 8, SSM_CONV_DIM), F32),
                        pltpu.VMEM((SSM_D_STATE, SSM_D_INNER), F32)],
        compiler_params=_cparams(("parallel", "arbitrary")),
        name="ssd_prompt",
    )(proj, proj, proj, cw, cb, dtb_pad, alog_pad, drow, nw, _r3_table())


def _rope_t(x_t, nh, c, s):
    n = x_t.shape[1]
    x3 = x_t.reshape(nh, NSA_HEAD_DIM, n)
    half = ROPE_DIM // 2
    x1 = x3[:, 0:half, :]
    x2 = x3[:, half:ROPE_DIM, :]
    r1 = x1 * c - x2 * s
    r2 = x2 * c + x1 * s
    return jnp.concatenate([r1, r2, x3[:, ROPE_DIM:, :]], axis=1).reshape(nh * NSA_HEAD_DIM, n)


def _prep_kernel(q_ref, kvc_ref, kvs_ref, kvw_ref, sm_ref, cos_ref, sin_ref,
                 qT_ref, qrT_ref, ks_ref, vsT_ref, kw_ref, vwT_ref, gT_ref, ncmp_ref, nsel_ref, nwin_ref):
    c = cos_ref[...]
    s = sin_ref[...]
    tt = q_ref.shape[0]
    q_t = q_ref[...].T
    qT_ref[0] = _bf(q_t * Q_SCALE)
    qrT_ref[0] = _bf(_rope_t(q_t, NSA_HEADS, c, s) * Q_SCALE)
    ncmp_ref[0] = kvc_ref[...].T
    for src, full_out, k_out, vt_out in ((kvs_ref, nsel_ref, ks_ref, vsT_ref), (kvw_ref, nwin_ref, kw_ref, vwT_ref)):
        kv = src[...]
        k_rot_t = _rope_t(kv[:, :KV_HALF].T, NSA_KV_HEADS, c, s)
        k_rot = k_rot_t.T
        v_t = kv[:, KV_HALF:].T
        full_out[0, :KV_HALF, :] = k_rot_t
        full_out[0, KV_HALF:, :] = v_t
        for i in range(tt // Q_BLOCK):
            k_out[0, i] = _bf(k_rot[i * Q_BLOCK:(i + 1) * Q_BLOCK, :])
            vt_out[0, i] = _bf(v_t[:, i * Q_BLOCK:(i + 1) * Q_BLOCK])
    g_t = jax.nn.sigmoid(sm_ref[...]).T
    gT_ref[0] = g_t[SSM_HEADS:SSM_HEADS + 3 * NSA_HEADS, :]


def _attn_prep(proj, bsz, t, tt, cos_t, sin_t):
    nt = t // tt
    nb = tt // Q_BLOCK
    row = lambda b, j: b * nt + j
    n_gate = 3 * NSA_HEADS
    return pl.pallas_call(
        _prep_kernel,
        grid=(bsz, nt),
        in_specs=[
            pl.BlockSpec((tt, D_MODEL), lambda b, j: (row(b, j), COL_Q // D_MODEL)),
            pl.BlockSpec((tt, KV_COLS), lambda b, j: (row(b, j), COL_KVC // KV_COLS)),
            pl.BlockSpec((tt, KV_COLS), lambda b, j: (row(b, j), COL_KVS // KV_COLS)),
            pl.BlockSpec((tt, KV_COLS), lambda b, j: (row(b, j), COL_KVW // KV_COLS)),
            pl.BlockSpec((tt, LANES), lambda b, j: (row(b, j), COL_SMALL // LANES)),
            pl.BlockSpec((ROPE_DIM // 2, tt), lambda b, j: (0, j)),
            pl.BlockSpec((ROPE_DIM // 2, tt), lambda b, j: (0, j)),
        ],
        out_specs=[
            pl.BlockSpec((1, D_MODEL, tt), lambda b, j: (b, 0, j)),
            pl.BlockSpec((1, D_MODEL, tt), lambda b, j: (b, 0, j)),
            pl.BlockSpec((1, nb, Q_BLOCK, KV_HALF), lambda b, j: (b, j, 0, 0)),
            pl.BlockSpec((1, nb, KV_HALF, Q_BLOCK), lambda b, j: (b, j, 0, 0)),
            pl.BlockSpec((1, nb, Q_BLOCK, KV_HALF), lambda b, j: (b, j, 0, 0)),
            pl.BlockSpec((1, nb, KV_HALF, Q_BLOCK), lambda b, j: (b, j, 0, 0)),
            pl.BlockSpec((1, n_gate, tt), lambda b, j: (b, 0, j)),
            pl.BlockSpec((1, KV_COLS, tt), lambda b, j: (b, 0, j)),
            pl.BlockSpec((1, KV_COLS, tt), lambda b, j: (b, 0, j)),
            pl.BlockSpec((1, KV_COLS, tt), lambda b, j: (b, 0, j)),
        ],
        out_shape=[
            jax.ShapeDtypeStruct((bsz, D_MODEL, t), BF16),
            jax.ShapeDtypeStruct((bsz, D_MODEL, t), BF16),
            jax.ShapeDtypeStruct((bsz, t // Q_BLOCK, Q_BLOCK, KV_HALF), BF16),
            jax.ShapeDtypeStruct((bsz, t // Q_BLOCK, KV_HALF, Q_BLOCK), BF16),
            jax.ShapeDtypeStruct((bsz, t // Q_BLOCK, Q_BLOCK, KV_HALF), BF16),
            jax.ShapeDtypeStruct((bsz, t // Q_BLOCK, KV_HALF, Q_BLOCK), BF16),
            jax.ShapeDtypeStruct((bsz, n_gate, t), F32),
            jax.ShapeDtypeStruct((bsz, KV_COLS, t), F32),
            jax.ShapeDtypeStruct((bsz, KV_COLS, t), F32),
            jax.ShapeDtypeStruct((bsz, KV_COLS, t), F32),
        ],
        compiler_params=_cparams(("parallel", "parallel")),
        name="attn_prep",
    )(proj, proj, proj, proj, proj, cos_t, sin_t)


def _compress_partial(data_refs, wcat_ref, xcat_s):
    ns = data_refs[0].shape[0] // CMP_STRIDE
    tiles_per_half = KV_HALF // LANES
    outs = []
    for lp in range(CMP_STRIDE):
        for c, ref in enumerate(data_refs):
            rows = ref[pl.ds(lp, ns, stride=CMP_STRIDE), :]
            v, cc = divmod(c, tiles_per_half)
            xcat_s[v, :, lp * KV_HALF + cc * LANES:lp * KV_HALF + (cc + 1) * LANES] = _bf(rows)
    for v in range(2):
        outs.append(_dot(xcat_s[v], wcat_ref[v]))
    return outs


def _compress_finish(pab_s, bias, w2bd_ref):
    n_sub = pab_s.shape[1]
    res = []
    for v in range(2):
        pab = pab_s[v]
        pre = pab[:, :KV_HALF] + pltpu.roll(pab[:, KV_HALF:], n_sub - 1, 0) + bias[v:v + 1, :]
        res.append(_dot(_bf(_gelu_tanh(pre)), w2bd_ref[v]))
    return res


def _pe_bias_kernel(pe_ref, w1f_ref, o_ref):
    rows = []
    for v in range(2):
        rows.append(jnp.dot(pe_ref[v], w1f_ref[v], preferred_element_type=F32,
                            precision=lax.Precision.HIGHEST)[0:1, :])
    o_ref[...] = jnp.concatenate(rows + [jnp.zeros((6, KV_HALF), F32)], axis=0)


def _pe_bias(pe8, w1f4):
    c3 = lambda a: pl.BlockSpec(a.shape, lambda i: (0, 0, 0))
    return pl.pallas_call(
        _pe_bias_kernel, grid=(1,), in_specs=[c3(pe8), c3(w1f4)],
        out_specs=pl.BlockSpec((8, KV_HALF), lambda i: (0, 0)),
        out_shape=jax.ShapeDtypeStruct((8, KV_HALF), F32),
        compiler_params=_cparams(("arbitrary",)), name="cmp_pe_bias",
    )(pe8, w1f4)


def _compress_prompt_kernel(d0_ref, d1_ref, d2_ref, d3_ref, wcat_ref, w2bd_ref, bias_ref,
                            kc_ref, vcT_ref, xcat_s, pab_s):
    j = pl.program_id(1)
    nj = pl.num_programs(1)
    ns = CMP_CHUNK // CMP_STRIDE
    parts = _compress_partial((d0_ref, d1_ref, d2_ref, d3_ref), wcat_ref, xcat_s)
    for v in range(2):
        pab_s[v, pl.ds(pl.multiple_of(j * ns, ns), ns), :] = parts[v]

    @pl.when(j == nj - 1)
    def _():
        kc, vc = _compress_finish(pab_s, bias_ref[...], w2bd_ref)
        kc_ref[0] = _bf(kc)
        vcT_ref[0] = _bf(vc.T)


def _compress_prompt(proj, bsz, t, wcat, w2bd, bias):
    nj = t // CMP_CHUNK
    n_sub = t // CMP_STRIDE
    ns = CMP_CHUNK // CMP_STRIDE
    c3 = lambda shape: pl.BlockSpec(shape, lambda b, j: (0, 0, 0))
    return pl.pallas_call(
        _compress_prompt_kernel,
        grid=(bsz, nj),
        in_specs=[pl.BlockSpec((CMP_CHUNK, LANES),
                               functools.partial(lambda b, j, c: (b * nj + j, COL_KVC // LANES + c), c=c))
                  for c in range(KV_COLS // LANES)]
        + [c3(wcat.shape), c3(w2bd.shape), pl.BlockSpec(bias.shape, lambda b, j: (0, 0))],
        out_specs=[pl.BlockSpec((1, n_sub, KV_HALF), lambda b, j: (b, 0, 0)),
                   pl.BlockSpec((1, KV_HALF, n_sub), lambda b, j: (b, 0, 0))],
        out_shape=[jax.ShapeDtypeStruct((bsz, n_sub, KV_HALF), BF16),
                   jax.ShapeDtypeStruct((bsz, KV_HALF, n_sub), BF16)],
        scratch_shapes=[pltpu.VMEM((2, ns, CMP_STRIDE * KV_HALF), BF16),
                        pltpu.VMEM((2, n_sub, KV_COLS), F32)],
        compiler_params=_cparams(("parallel", "arbitrary")),
        name="compress_prompt",
    )(proj, proj, proj, proj, wcat, w2bd, bias)


def _topk_mask(score, jrow, n_pick):
    sel = jnp.zeros_like(score)
    for _ in range(n_pick):
        mx = jnp.max(score, axis=0, keepdims=True)
        idx = jnp.min(jnp.where(score == mx, jrow, 1e9), axis=0, keepdims=True)
        chosen = jrow == idx
        sel = jnp.where(chosen, 1.0, sel)
        score = jnp.where(chosen, -jnp.inf, score)
    return sel


def _online_step(st, vt_aug, m, acc):
    mn = jnp.maximum(m, jnp.max(st, axis=0, keepdims=True))
    p = jnp.exp2(st - mn)
    acc = acc * jnp.exp2(m - mn) + _dot(vt_aug, _bf(p))
    return mn, acc


def _nsa_prompt_kernel(qT_ref, qrT_ref, gT_ref, kc_ref, vcT_ref, ks_ref, vsT_ref, kw_ref, vwT_ref, smapT_ref,
                       econst_ref, wn_ref, o_ref, qr_s, neg_s, oc_s, m_s, acc_s):
    qb = pl.program_id(1)
    nq = NSA_GQA * Q_BLOCK
    n_cmp_rows = kc_ref.shape[1]
    n_selblk = smapT_ref.shape[0]
    n_slab = n_selblk // BLOCKS_PER_SLAB
    lane_q = lax.broadcasted_iota(jnp.int32, (1, nq), 1) % Q_BLOCK
    qpos = qb * Q_BLOCK + lane_q
    zero_q = jnp.zeros((NSA_HEAD_DIM, nq), BF16)
    acc_rows = NSA_HEAD_DIM + 16

    jrow = lax.broadcasted_iota(jnp.int32, (n_selblk, Q_BLOCK), 0).astype(F32)
    cur = (qb * (Q_BLOCK // SEL_BLOCK)
           + lax.broadcasted_iota(jnp.int32, (1, Q_BLOCK), 1) // SEL_BLOCK).astype(F32)
    future = jrow > cur
    forced = jnp.where(jrow == 0.0, 1.0, 0.0) + jnp.where(jrow == cur, 1.0, 0.0) + jnp.where(jrow == cur - 1.0, 1.0, 0.0)
    crow = lax.broadcasted_iota(jnp.int32, (n_cmp_rows, 1), 0)
    cmp_valid = (crow * CMP_STRIDE + (CMP_LEN - 1)) <= qpos

    def q_cat(ref, h):
        return jnp.concatenate([ref[0, (h * NSA_GQA + g) * NSA_HEAD_DIM:(h * NSA_GQA + g + 1) * NSA_HEAD_DIM, :]
                                for g in range(NSA_GQA)], axis=1)

    for h in range(NSA_KV_HEADS):
        qp = jnp.concatenate([q_cat(qT_ref, h) if hh == h else zero_q for hh in range(NSA_KV_HEADS)], axis=0)
        sc = jnp.where(cmp_valid, _dot(kc_ref[0], qp), NEG)
        m_c = jnp.max(sc, axis=0, keepdims=True)
        e_c = jnp.where(cmp_valid, jnp.exp2(sc - m_c), 0.0)
        inv_l = 1.0 / jnp.maximum(jnp.sum(e_c, axis=0, keepdims=True), 1e-30)
        p_c = e_c * inv_l
        oc_s[h] = _dot(vcT_ref[0, h * NSA_HEAD_DIM:(h + 1) * NSA_HEAD_DIM, :], _bf(p_c))
        psum = (p_c[:, 0:Q_BLOCK] + p_c[:, Q_BLOCK:2 * Q_BLOCK]
                + p_c[:, 2 * Q_BLOCK:3 * Q_BLOCK] + p_c[:, 3 * Q_BLOCK:4 * Q_BLOCK])
        p_hi = _bf(psum)
        p_lo = _bf(psum - p_hi.astype(F32))
        imp_t = _dot(smapT_ref[...], p_hi) + _dot(smapT_ref[...], p_lo)
        score = jnp.where(future, -BIG, jnp.where(forced > 0.5, BIG, imp_t))
        sel = _topk_mask(score, jrow, min(SEL_TOP_N, n_selblk))
        neg = jnp.where(sel > 0.5, 0.0, NEG)
        neg = jnp.concatenate([neg] * NSA_GQA, axis=1).reshape(n_slab, BLOCKS_PER_SLAB, nq)
        neg_s[h] = _bf(jnp.concatenate([neg, jnp.zeros_like(neg)], axis=1))
        qr_h = q_cat(qrT_ref, h)
        rhs_top = jnp.concatenate([qr_h, zero_q] if h % 2 == 0 else [zero_q, qr_h], axis=0)
        for qc in range(N_QCHUNK):
            cols = slice(qc * QCHUNK, (qc + 1) * QCHUNK)
            qr_s[h, qc, 0:2 * NSA_HEAD_DIM, :] = rhs_top[:, cols]
            qr_s[h, qc, 2 * NSA_HEAD_DIM:, :] = jnp.zeros((2 * NSA_HEAD_DIM, QCHUNK), BF16)
            for br in range(2):
                m_s[br, h, qc] = jnp.full((8, QCHUNK), NEG, F32)
                acc_s[br, h, qc] = jnp.zeros((acc_rows, QCHUNK), F32)

    ones16 = jnp.ones((16, Q_BLOCK), BF16)
    qpos_c = qpos[:, 0:QCHUNK]
    blk_row = lax.broadcasted_iota(jnp.int32, (Q_BLOCK, 1), 0)

    def tile_update(br, h, qc, sts, vt_ref, kcs):
        vt = jnp.concatenate([vt_ref[0, kc, h * NSA_HEAD_DIM:(h + 1) * NSA_HEAD_DIM, :] for kc in kcs], axis=1)
        vt = jnp.concatenate([vt, jnp.ones((16, len(kcs) * Q_BLOCK), BF16)], axis=0)
        st = sts[0] if len(sts) == 1 else jnp.concatenate(sts, axis=0)
        mn, acc = _online_step(st, vt, m_s[br, h, qc, 0:1, :], acc_s[br, h, qc])
        m_s[br, h, qc] = jnp.broadcast_to(mn, (8, QCHUNK))
        acc_s[br, h, qc] = acc

    def sel_scores(kc, h, qc):
        slab = lax.shift_right_logical(kc, CHUNK_SHIFT)
        qr_s[h, qc, 2 * NSA_HEAD_DIM:2 * NSA_HEAD_DIM + 16, :] = neg_s[h, slab, :, qc * QCHUNK:(qc + 1) * QCHUNK]
        pair = h // 2
        lhs = jnp.concatenate([ks_ref[0, kc, :, pair * LANES:(pair + 1) * LANES],
                               econst_ref[kc & (CHUNKS_PER_SLAB - 1)]], axis=1)
        return _dot(lhs, qr_s[h, qc])

    tiles = [(h, qc) for h in range(NSA_KV_HEADS) for qc in range(N_QCHUNK)]

    def run_pipelined(items):
        pending = []
        for score_fn, update_fn in items:
            pending.append((update_fn, score_fn()))
            if len(pending) > MXU_LOOKAHEAD:
                fn, st = pending.pop(0)
                fn(st)
        for fn, st in pending:
            fn(st)

    def sel_items(kcs, masked):
        def score(h, qc):
            sts = [sel_scores(kc, h, qc) for kc in kcs]
            if masked:
                sts = [jnp.where((kc * Q_BLOCK + blk_row) <= qpos_c, st, NEG) for kc, st in zip(kcs, sts)]
            return sts
        return [(functools.partial(score, h, qc),
                 functools.partial(lambda sts, h, qc: tile_update(0, h, qc, sts, vsT_ref, kcs), h=h, qc=qc))
                for h, qc in tiles]

    n_full = lax.shift_right_logical(qb, CHUNK_SHIFT)

    def slabs_body(n_slabs, s0):
        items = []
        for c in range(0, n_slabs * CHUNKS_PER_SLAB, CHUNKS_PER_ITEM):
            items += sel_items([s0 * CHUNKS_PER_SLAB + c + d for d in range(CHUNKS_PER_ITEM)], False)
        run_pipelined(items)

    def slab_group_body(sg, carry):
        slabs_body(SLABS_PER_TRIP, sg * SLABS_PER_TRIP)
        return carry

    def slab_body(s, carry):
        slabs_body(1, s)
        return carry

    n_groups = lax.shift_right_logical(n_full, SLABS_PER_TRIP.bit_length() - 1)
    lax.fori_loop(0, n_groups, slab_group_body, 0)
    lax.fori_loop(n_groups * SLABS_PER_TRIP, n_full, slab_body, 0)

    causal = (qb * Q_BLOCK + blk_row) <= qpos_c
    n_wblk = WINDOW // Q_BLOCK + 1

    def win_scores(i, h, qc):
        kb = qb - (n_wblk - 1) + i
        st = _dot(kw_ref[0, jnp.maximum(kb, 0), :, (h // 2) * LANES:(h // 2 + 1) * LANES],
                  qr_s[h, qc, 0:2 * NSA_HEAD_DIM, :])
        if i == 0:
            st = jnp.where((qpos_c - (kb * Q_BLOCK + blk_row)) <= WINDOW, st, NEG)
        if i == n_wblk - 1:
            return jnp.where(causal, st, NEG)
        return jnp.where(kb >= 0, st, NEG)

    items = []
    for c in range(0, CHUNKS_PER_SLAB, CHUNKS_PER_ITEM):
        items += sel_items([n_full * CHUNKS_PER_SLAB + c + d for d in range(CHUNKS_PER_ITEM)], True)
    for i0 in range(0, n_wblk, CHUNKS_PER_ITEM):
        blks = list(range(i0, min(i0 + CHUNKS_PER_ITEM, n_wblk)))
        kbs_ld = [jnp.maximum(qb - (n_wblk - 1) + i, 0) for i in blks]
        items += [(functools.partial(lambda h, qc, blks: [win_scores(i, h, qc) for i in blks], h, qc, blks),
                   functools.partial(lambda sts, h, qc, kbs_ld: tile_update(1, h, qc, sts, vwT_ref, kbs_ld),
                                     h=h, qc=qc, kbs_ld=kbs_ld))
                  for h, qc in tiles]
    run_pipelined(items)

    ot_parts = []
    for h in range(NSA_KV_HEADS):
        o_parts = []
        for qc in range(N_QCHUNK):
            outs = []
            for br in range(2):
                acc = acc_s[br, h, qc]
                outs.append(acc[0:NSA_HEAD_DIM, :] * (1.0 / acc[NSA_HEAD_DIM:NSA_HEAD_DIM + 1, :]))
            o_parts.append(outs)

        def gate_row(br):
            return jnp.concatenate([gT_ref[0, (h * NSA_GQA + g) * 3 + br:(h * NSA_GQA + g) * 3 + br + 1, :]
                                    for g in range(NSA_GQA)], axis=1)

        os_t = jnp.concatenate([o_parts[qc][0] for qc in range(N_QCHUNK)], axis=1)
        ow_t = jnp.concatenate([o_parts[qc][1] for qc in range(N_QCHUNK)], axis=1)
        o_h = gate_row(0) * oc_s[h] + gate_row(1) * os_t + gate_row(2) * ow_t
        for g in range(NSA_GQA):
            ot_parts.append(o_h[:, g * Q_BLOCK:(g + 1) * Q_BLOCK])
    o_t = _bf(jnp.concatenate(ot_parts, axis=0))
    o_ref[...] = _dot_tn(o_t, wn_ref[...])


def _selection_map_t(n_sel, n_cmp_rows):
    i = np.arange(n_cmp_rows)[None, :]
    j = np.arange(n_sel)[:, None]
    ov = (i * CMP_STRIDE < (j + 1) * SEL_BLOCK) & (i * CMP_STRIDE + CMP_LEN > j * SEL_BLOCK)
    return ov


def _nsa_prompt(qT, qrT, gT, kc, vcT, ks, vsT, kw, vwT, wn, bsz, t):
    nqb = t // Q_BLOCK
    n_sub = t // CMP_STRIDE
    n_cmp = n_sub - CMP_LEN // CMP_STRIDE + 1
    n_sel = t // SEL_BLOCK
    smap = _selection_map_t(n_sel, n_sub) & (np.arange(n_sub)[None, :] < n_cmp)
    smap_t = jnp.asarray(smap, dtype=BF16)
    blk_of_key = (np.arange(KEY_SLAB) // SEL_BLOCK).reshape(CHUNKS_PER_SLAB, Q_BLOCK, 1)
    econst = jnp.asarray(blk_of_key == np.arange(LANES)[None, None, :], dtype=BF16)
    nsl = t // KEY_SLAB
    nq = NSA_GQA * Q_BLOCK
    per_b3 = lambda shape: pl.BlockSpec(shape, lambda b, j: (b, 0, 0))
    per_b4 = lambda shape: pl.BlockSpec(shape, lambda b, j: (b, 0, 0, 0))
    return pl.pallas_call(
        _nsa_prompt_kernel,
        grid=(bsz, nqb),
        in_specs=[
            pl.BlockSpec((1, D_MODEL, Q_BLOCK), lambda b, j: (b, 0, j)),
            pl.BlockSpec((1, D_MODEL, Q_BLOCK), lambda b, j: (b, 0, j)),
            pl.BlockSpec((1, 3 * NSA_HEADS, Q_BLOCK), lambda b, j: (b, 0, j)),
            per_b3((1, n_sub, KV_HALF)), per_b3((1, KV_HALF, n_sub)),
            per_b4((1, nqb, Q_BLOCK, KV_HALF)), per_b4((1, nqb, KV_HALF, Q_BLOCK)),
            per_b4((1, nqb, Q_BLOCK, KV_HALF)), per_b4((1, nqb, KV_HALF, Q_BLOCK)),
            pl.BlockSpec((n_sel, n_sub), lambda b, j: (0, 0)),
            pl.BlockSpec((CHUNKS_PER_SLAB, Q_BLOCK, LANES), lambda b, j: (0, 0, 0)),
            pl.BlockSpec((D_MODEL, D_MODEL), lambda b, j: (0, 0)),
        ],
        out_specs=pl.BlockSpec((Q_BLOCK, D_MODEL), lambda b, j: (b * nqb + j, 0)),
        out_shape=jax.ShapeDtypeStruct((bsz * t, D_MODEL), F32),
        scratch_shapes=[pltpu.VMEM((NSA_KV_HEADS, N_QCHUNK, KV_HALF, QCHUNK), BF16),
                        pltpu.VMEM((NSA_KV_HEADS, nsl, 16, nq), BF16),
                        pltpu.VMEM((NSA_KV_HEADS, NSA_HEAD_DIM, nq), F32),
                        pltpu.VMEM((2, NSA_KV_HEADS, N_QCHUNK, 8, QCHUNK), F32),
                        pltpu.VMEM((2, NSA_KV_HEADS, N_QCHUNK, NSA_HEAD_DIM + 16, QCHUNK), F32)],
        compiler_params=_cparams(("parallel", "arbitrary")),
        name="nsa_prompt",
    )(qT, qrT, gT, kc, vcT, ks, vsT, kw, vwT, smap_t, econst, wn)


def _merge_kernel(x_ref, mg_ref, ys_ref, bn_ref, ws_ref, wo_ref, g_ref, b_ref, h_ref):
    mg = mg_ref[...]
    b_ssm = _dot(ys_ref[...], ws_ref[...])
    mix = jax.nn.sigmoid(mg[:, :D_MODEL]) * b_ssm + jax.nn.sigmoid(mg[:, D_MODEL:]) * bn_ref[...]
    pre = DEEPNORM_ALPHA * x_ref[...] + _dot(_bf(mix), wo_ref[...])
    h_ref[...] = _layer_norm(pre, g_ref[...], b_ref[...])


def _merge(x, proj, y_ssm, b_nsa, ws, wo, g, b, tm):
    m = x.shape[0]
    rowblk = lambda shape, c=0: pl.BlockSpec(shape, lambda i, c=c: (i, c))
    const = lambda shape: pl.BlockSpec(shape, lambda i: (0, 0))
    return pl.pallas_call(
        _merge_kernel,
        grid=(m // tm,),
        in_specs=[rowblk((tm, D_MODEL)), rowblk((tm, 2 * D_MODEL), COL_MG // (2 * D_MODEL)),
                  rowblk((tm, SSM_D_INNER)), rowblk((tm, D_MODEL)),
                  const((SSM_D_INNER, D_MODEL)), const((D_MODEL, D_MODEL)), const((1, D_MODEL)), const((1, D_MODEL))],
        out_specs=rowblk((tm, D_MODEL)),
        out_shape=jax.ShapeDtypeStruct((m, D_MODEL), F32),
        compiler_params=_cparams(("parallel",)),
        name="merge_ln1",
    )(x, proj, y_ssm, b_nsa, ws, wo, g, b)


def _ffn_kernel(h_ref, wg_ref, wu_ref, wd_ref, g_ref, b_ref, y_ref, acc_s):
    h = h_ref[...]
    hb = _bf(h)
    acc_s[...] = jnp.zeros_like(acc_s)

    def body(c, carry):
        gate = _dot(hb, wg_ref[c])
        up = _dot(hb, wu_ref[c])
        acc_s[...] += _dot(_bf(_silu(gate) * up), wd_ref[c])
        return carry

    lax.fori_loop(0, wg_ref.shape[0], body, 0)
    y_ref[...] = _layer_norm(DEEPNORM_ALPHA * h + acc_s[...], g_ref[...], b_ref[...])


def _ffn(h, wg3, wu3, wd3, g, b, tm):
    m = h.shape[0]
    nc = wg3.shape[0]
    const2 = lambda shape: pl.BlockSpec(shape, lambda i: (0, 0))
    const3 = lambda shape: pl.BlockSpec(shape, lambda i: (0, 0, 0))
    return pl.pallas_call(
        _ffn_kernel,
        grid=(m // tm,),
        in_specs=[pl.BlockSpec((tm, D_MODEL), lambda i: (i, 0)),
                  const3((nc, D_MODEL, FFN_CHUNK)), const3((nc, D_MODEL, FFN_CHUNK)), const3((nc, FFN_CHUNK, D_MODEL)),
                  const2((1, D_MODEL)), const2((1, D_MODEL))],
        out_specs=pl.BlockSpec((tm, D_MODEL), lambda i: (i, 0)),
        out_shape=jax.ShapeDtypeStruct((m, D_MODEL), F32),
        scratch_shapes=[pltpu.VMEM((tm, D_MODEL), F32)],
        compiler_params=_cparams(("parallel",)),
        name="ffn_ln2",
    )(h, wg3, wu3, wd3, g, b)


def _ssd_s_pre_kernel(xbc_ref, cst_ref, sm_ref, cw_ref, cb_ref, dtb_ref, alog_ref, e_ref,
                      xs_ref, xdt_ref, bm_ref, cm_ref, dec_ref, ncv_ref):
    xbc = xbc_ref[...]
    acc = cb_ref[...] + cw_ref[SSM_CONV_W - 1:SSM_CONV_W, :] * xbc
    for k in range(SSM_CONV_W - 1):
        acc = acc + cw_ref[k:k + 1, :] * cst_ref[k]
    act = _silu(acc)
    for k in range(SSM_CONV_W - 2):
        ncv_ref[k] = cst_ref[k + 1]
    ncv_ref[SSM_CONV_W - 2] = xbc
    lane = lax.broadcasted_iota(jnp.int32, (1, LANES), 1)
    a_full = jnp.where(lane < SSM_HEADS, -jnp.exp(alog_ref[...]), 0.0)
    dt = _softplus(sm_ref[...] + dtb_ref[...])
    dec_ref[...] = jnp.exp(dt * a_full)
    xs = act[:, :SSM_D_INNER]
    xs_ref[...] = xs
    xdt_ref[...] = xs * _dot_exact_lhs(dt, e_ref[...])
    bm_ref[...] = act[:, SSM_D_INNER:SSM_D_INNER + SSM_GROUPS * SSM_D_STATE]
    cm_ref[...] = act[:, SSM_D_INNER + SSM_GROUPS * SSM_D_STATE:]


def _ssd_s_pre(proj_s, cst, cw, cb, dtb_pad, alog_pad):
    n = proj_s.shape[0]
    e = np.zeros((LANES, SSM_D_INNER), np.float32)
    for h in range(SSM_HEADS):
        e[h, h * SSM_HEAD_DIM:(h + 1) * SSM_HEAD_DIM] = 1.0
    gw = SSM_GROUPS * SSM_D_STATE
    full = lambda shape: pl.BlockSpec(shape, lambda i: (0,) * len(shape))
    return pl.pallas_call(
        _ssd_s_pre_kernel,
        grid=(1,),
        in_specs=[pl.BlockSpec((n, SSM_CONV_DIM), lambda i: (0, COL_XBC // SSM_CONV_DIM)),
                  full(cst.shape),
                  pl.BlockSpec((n, LANES), lambda i: (0, COL_SMALL // LANES)),
                  full(cw.shape), full(cb.shape), full((1, LANES)), full((1, LANES)), full(e.shape)],
        out_specs=[full((n, SSM_D_INNER)), full((n, SSM_D_INNER)), full((n, gw)), full((n, gw)),
                   full((n, LANES)), full(cst.shape)],
        out_shape=[jax.ShapeDtypeStruct((n, SSM_D_INNER), F32), jax.ShapeDtypeStruct((n, SSM_D_INNER), F32),
                   jax.ShapeDtypeStruct((n, gw), F32), jax.ShapeDtypeStruct((n, gw), F32),
                   jax.ShapeDtypeStruct((n, LANES), F32), jax.ShapeDtypeStruct(cst.shape, F32)],
        compiler_params=_cparams(("arbitrary",)),
        name="ssd_sample_pre",
    )(proj_s, cst, proj_s, cw, cb, dtb_pad, alog_pad, jnp.asarray(e, dtype=BF16))


def _dyn_row(ref, b, cols=slice(None)):
    tile = ref[pl.ds(pl.multiple_of((b >> 3) << 3, 8), 8), cols]
    r = lax.broadcasted_iota(jnp.int32, (8, 1), 0)
    return jnp.sum(jnp.where(r == (b & 7), tile, 0.0), axis=0, keepdims=True)


def _onehot_cols(b, n):
    return jnp.where(lax.broadcasted_iota(jnp.int32, (n, n), 0) == b, 1.0, 0.0).astype(BF16)


def _ssd_s_state_kernel(st_ref, xdt_ref, bm_ref, cm_ref, dec_ref, nst_ref, yT_ref, xT_s, dT_s, cT_s):
    b = pl.program_id(0)
    n = xdt_ref.shape[0]

    @pl.when(b == 0)
    def _():
        for i, part in enumerate(_split3(xdt_ref[...].T)):
            xT_s[i] = part
        for i, part in enumerate(_split3(dec_ref[...].T)):
            dT_s[i] = part
        cT_s[...] = _bf(cm_ref[...].T)
        yT_ref[...] = jnp.zeros_like(yT_ref)

    hb = _onehot_cols(b, n)
    r = _dot(xT_s[0], hb) + _dot(xT_s[1], hb) + _dot(xT_s[2], hb)
    dec_r = _dot(dT_s[0], hb) + _dot(dT_s[1], hb) + _dot(dT_s[2], hb)
    c_r = _bf(_dot(cT_s[...], hb))
    lane_is_b = lax.broadcasted_iota(jnp.int32, (1, n), 1) == b
    hpg = SSM_HEADS // SSM_GROUPS
    brow_all = _dyn_row(bm_ref, b)
    for h in range(SSM_HEADS):
        g = h // hpg
        brow = brow_all[:, g * SSM_D_STATE:(g + 1) * SSM_D_STATE]
        rows = slice(h * SSM_HEAD_DIM, (h + 1) * SSM_HEAD_DIM)
        new = st_ref[0, h] * dec_r[h:h + 1, :] + r[rows, :] * brow
        nst_ref[0, h] = new
        y_h = _dot(_bf(new), c_r[g * SSM_D_STATE:(g + 1) * SSM_D_STATE, :])
        yT_ref[rows, :] = jnp.where(lane_is_b, y_h, yT_ref[rows, :])


def _ssd_s_state(state, xdt, bm, cm, dec):
    n = xdt.shape[0]
    assert n == LANES and SSM_D_STATE == LANES
    full = lambda a: pl.BlockSpec(a.shape, lambda b: (0, 0))
    blk = pl.BlockSpec((1, SSM_HEADS, SSM_HEAD_DIM, SSM_D_STATE), lambda b: (b, 0, 0, 0))
    return pl.pallas_call(
        _ssd_s_state_kernel,
        grid=(n,),
        in_specs=[blk, full(xdt), full(bm), full(cm), full(dec)],
        out_specs=[blk, pl.BlockSpec((SSM_D_INNER, n), lambda b: (0, 0))],
        out_shape=[jax.ShapeDtypeStruct(state.shape, F32), jax.ShapeDtypeStruct((SSM_D_INNER, n), F32)],
        scratch_shapes=[pltpu.VMEM((3, SSM_D_INNER, n), BF16), pltpu.VMEM((3, LANES, n), BF16),
                        pltpu.VMEM((SSM_GROUPS * SSM_D_STATE, n), BF16)],
        compiler_params=_cparams(("arbitrary",)),
        name="ssd_sample_state",
    )(state, xdt, bm, cm, dec)


def _ssd_s_post_kernel(yT_ref, xs_ref, z_ref, drow_ref, nw_ref, o_ref):
    y = yT_ref[...].T + drow_ref[...] * xs_ref[...]
    v = y * _silu(z_ref[...])
    gw = SSM_D_INNER // SSM_GROUPS
    outs = []
    for g in range(SSM_GROUPS):
        vg = v[:, g * gw:(g + 1) * gw]
        ms = jnp.sum(vg * vg, axis=-1, keepdims=True) * (1.0 / gw)
        outs.append(vg * lax.rsqrt(ms + NORM_EPS) * nw_ref[:, g * gw:(g + 1) * gw])
    o_ref[...] = _bf(jnp.concatenate(outs, axis=1))


def _ssd_s_post(yT, xs, proj_s, drow, nw):
    n = xs.shape[0]
    full = lambda shape: pl.BlockSpec(shape, lambda i: (0, 0))
    return pl.pallas_call(
        _ssd_s_post_kernel,
        grid=(1,),
        in_specs=[full(yT.shape), full(xs.shape),
                  pl.BlockSpec((n, SSM_D_INNER), lambda i: (0, COL_Z // SSM_D_INNER)),
                  full(drow.shape), full(nw.shape)],
        out_specs=full((n, SSM_D_INNER)),
        out_shape=jax.ShapeDtypeStruct((n, SSM_D_INNER), BF16),
        compiler_params=_cparams(("arbitrary",)),
        name="ssd_sample_post",
    )(yT, xs, proj_s, drow, nw)


PAGES_PER_STEP = 16


def _q_block(q_ref, b, n):
    r_q = _dot(q_ref[0], _onehot_cols(b, n))
    lane = lax.broadcasted_iota(jnp.int32, (1, n), 1)
    blocks = []
    for h in range(NSA_KV_HEADS):
        blk = None
        for g in range(NSA_GQA):
            hd = h * NSA_GQA + g
            piece = jnp.where(lane == hd, r_q[hd * NSA_HEAD_DIM:(hd + 1) * NSA_HEAD_DIM, :], 0.0)
            blk = piece if blk is None else blk + piece
        blocks.append(blk)
    return _bf(jnp.concatenate(blocks, axis=0))


def _scatter_heads(o_t, out_ref, b, n):
    lane = lax.broadcasted_iota(jnp.int32, (1, n), 1)
    lane_is_b = lane == b
    for hd in range(NSA_HEADS):
        h = hd // NSA_GQA
        piece = o_t[h * NSA_HEAD_DIM:(h + 1) * NSA_HEAD_DIM, :]
        col = jnp.sum(jnp.where(lane == hd, piece, 0.0), axis=1, keepdims=True)
        rows = slice(hd * NSA_HEAD_DIM, (hd + 1) * NSA_HEAD_DIM)
        out_ref[rows, :] = jnp.where(lane_is_b, col, out_ref[rows, :])


def _nsa_s_cmp_kernel(pt_ref, *refs):
    pages = refs[:PAGES_PER_STEP]
    (qT_ref, wcat_ref, w2bd_ref, bias_ref, perm_ref, smapT_ref, gsum_ref,
     otc_ref, sel_ref, xcat_s, pab_s) = refs[PAGES_PER_STEP:]
    b = pl.program_id(0)
    j = pl.program_id(1)
    nj = pl.num_programs(1)
    n = qT_ref.shape[2]
    ns = CMP_CHUNK // CMP_STRIDE
    sub_per_pair = 2 * PAGE_SIZE // CMP_STRIDE
    for i in range(PAGES_PER_STEP // 2):
        pair = _bf(jnp.concatenate([pages[2 * i][0], pages[2 * i + 1][0]], axis=1))
        t_perm = _bf(_dot_nt(perm_ref[...], pair))
        for lp in range(CMP_STRIDE):
            for v in range(2):
                xcat_s[v, i * sub_per_pair:(i + 1) * sub_per_pair, lp * KV_HALF:(lp + 1) * KV_HALF] = (
                    t_perm[lp * sub_per_pair:(lp + 1) * sub_per_pair, v * KV_HALF:(v + 1) * KV_HALF])
    for v in range(2):
        pab_s[v, pl.ds(pl.multiple_of(j * ns, ns), ns), :] = _dot(xcat_s[v], wcat_ref[v])

    @pl.when((b == 0) & (j == 0))
    def _():
        otc_ref[...] = jnp.zeros_like(otc_ref)

    @pl.when(j == nj - 1)
    def _():
        kc, vc = _compress_finish(pab_s, bias_ref[...], w2bd_ref)
        n_sub = kc.shape[0]
        past_len = n_sub * CMP_STRIDE
        qblk = _q_block(qT_ref, b, n)
        crow = lax.broadcasted_iota(jnp.int32, (n_sub, 1), 0)
        valid = (crow * CMP_STRIDE + (CMP_LEN - 1)) <= past_len
        sc = jnp.where(valid, _dot(_bf(kc), qblk), NEG)
        m = jnp.max(sc, axis=0, keepdims=True)
        e = jnp.where(valid, jnp.exp2(sc - m), 0.0)
        p = e * (1.0 / jnp.maximum(jnp.sum(e, axis=0, keepdims=True), 1e-30))
        _scatter_heads(_dot(_bf(vc.T), _bf(p)), otc_ref, b, n)
        p_hi = _bf(p)
        p_lo = _bf(p - p_hi.astype(F32))
        psum = _dot(p_hi, gsum_ref[...]) + _dot(p_lo, gsum_ref[...])
        q_hi = _bf(psum)
        q_lo = _bf(psum - q_hi.astype(F32))
        imp_t = _dot(smapT_ref[...], q_hi) + _dot(smapT_ref[...], q_lo)
        nj_pad = imp_t.shape[0]
        n_sel = past_len // SEL_BLOCK + 1
        cur = float(past_len // SEL_BLOCK)
        jrow = lax.broadcasted_iota(jnp.int32, (nj_pad, n), 0).astype(F32)
        forced = (jnp.where(jrow == 0.0, 1.0, 0.0) + jnp.where(jrow == cur, 1.0, 0.0)
                  + jnp.where(jrow == cur - 1.0, 1.0, 0.0))
        score = jnp.where(jrow > cur, -BIG, jnp.where(forced > 0.5, BIG, imp_t))
        score = jnp.where(jrow >= float(n_sel), -jnp.inf, score)
        sel_ref[0] = _topk_mask(score, jrow, min(SEL_TOP_N, n_sel))


def _page_specs(n_pages_step):
    return [pl.BlockSpec((1, KV_COLS, PAGE_SIZE),
                         functools.partial(lambda b, j, pt, i: (pt[b, j * n_pages_step + i], 0, 0), i=i))
            for i in range(n_pages_step)]


def _nsa_s_cmp(page_table, cache_cmp, qT, wcat, w2bd, bias, past_len):
    nseq = page_table.shape[0]
    pair_tok = 2 * PAGE_SIZE
    r = np.arange(pair_tok)
    perm = jnp.asarray(r[None, :] == ((r % (pair_tok // CMP_STRIDE)) * CMP_STRIDE + r // (pair_tok // CMP_STRIDE))[:, None],
                       dtype=BF16)
    n_sub = past_len // CMP_STRIDE
    n_cmp = n_sub - CMP_LEN // CMP_STRIDE + 1
    n_sel = past_len // SEL_BLOCK + 1
    nj_pad = -(-n_sel // 8) * 8
    smap = np.zeros((nj_pad, n_sub), bool)
    smap[:n_sel] = _selection_map_t(n_sel, n_sub) & (np.arange(n_sub)[None, :] < n_cmp)
    lanes = np.arange(nseq)
    gsum = ((lanes[:, None] // NSA_GQA) == (lanes[None, :] // NSA_GQA)) & (lanes[:, None] < NSA_HEADS) & (lanes[None, :] < NSA_HEADS)
    nj = past_len // CMP_CHUNK
    ns = CMP_CHUNK // CMP_STRIDE
    c2 = lambda a: pl.BlockSpec(a.shape, lambda b, j, pt: (0, 0))
    c3 = lambda a: pl.BlockSpec(a.shape, lambda b, j, pt: (0, 0, 0))
    smap_j = jnp.asarray(smap, dtype=BF16)
    gsum_j = jnp.asarray(gsum, dtype=BF16)
    grid_spec = pltpu.PrefetchScalarGridSpec(
        num_scalar_prefetch=1,
        grid=(nseq, nj),
        in_specs=_page_specs(PAGES_PER_STEP) + [c3(qT), c3(wcat), c3(w2bd), c2(bias), c2(perm), c2(smap_j), c2(gsum_j)],
        out_specs=[pl.BlockSpec((D_MODEL, nseq), lambda b, j, pt: (0, 0)),
                   pl.BlockSpec((1, nj_pad, nseq), lambda b, j, pt: (b, 0, 0))],
        scratch_shapes=[pltpu.VMEM((2, ns, CMP_STRIDE * KV_HALF), BF16),
                        pltpu.VMEM((2, n_sub, KV_COLS), F32)],
    )
    return pl.pallas_call(
        _nsa_s_cmp_kernel,
        grid_spec=grid_spec,
        out_shape=[jax.ShapeDtypeStruct((D_MODEL, nseq), F32),
                   jax.ShapeDtypeStruct((nseq, nj_pad, nseq), F32)],
        compiler_params=_cparams(("arbitrary", "arbitrary")),
        name="nsa_sample_cmp",
    )(page_table, *([cache_cmp] * PAGES_PER_STEP), qT, wcat, w2bd, bias, perm, smap_j, gsum_j)


def _nsa_s_att_kernel(n_past_blk, pt_ref, *refs):
    pages = refs[:PAGES_PER_STEP]
    (qrT_ref, sel_ref, nselT_ref, win_ref, nwinT_ref,
     ots_ref, otw_ref, nwo_ref, qblk_s, m_s, acc_s) = refs[PAGES_PER_STEP:]
    b = pl.program_id(0)
    j = pl.program_id(1)
    nj = pl.num_programs(1)
    n = qrT_ref.shape[2]
    blk_per_step = PAGES_PER_STEP * PAGE_SIZE // SEL_BLOCK

    @pl.when((b == 0) & (j == 0))
    def _():
        ots_ref[...] = jnp.zeros_like(ots_ref)
        otw_ref[...] = jnp.zeros_like(otw_ref)

    @pl.when(j == 0)
    def _():
        qblk_s[...] = _q_block(qrT_ref, b, n)
        m_s[...] = jnp.full(m_s.shape, NEG, F32)
        acc_s[...] = jnp.zeros_like(acc_s)

    qblk = qblk_s[...]

    def keys_update(kv_t, mask, m, acc):
        st = jnp.where(mask, _dot_tn(_bf(kv_t[:KV_HALF, :]), qblk), NEG)
        vt = jnp.concatenate([_bf(kv_t[KV_HALF:, :]), jnp.ones((16, kv_t.shape[1]), BF16)], axis=0)
        return _online_step(st, vt, m, acc)

    lane_n = lax.broadcasted_iota(jnp.int32, (1, n), 1)
    key_is0 = lax.broadcasted_iota(jnp.int32, (PAGE_SIZE, 1), 0) == 0

    def new_token_tile(ref):
        col = jnp.sum(jnp.where(lane_n == b, ref[...], 0.0), axis=1, keepdims=True)
        return jnp.where(lax.broadcasted_iota(jnp.int32, (1, PAGE_SIZE), 1) == 0, col, 0.0), col

    kv_step = jnp.concatenate([pages[i][0] for i in range(PAGES_PER_STEP)], axis=1)
    sel_rows = sel_ref[0, pl.ds(pl.multiple_of(j * blk_per_step, blk_per_step), blk_per_step), :]
    mask = jnp.concatenate([jnp.broadcast_to(sel_rows[r:r + 1, :], (SEL_BLOCK, n))
                            for r in range(blk_per_step)], axis=0) > 0.5
    m, acc = keys_update(kv_step, mask, m_s[...], acc_s[...])
    m_s[...] = m
    acc_s[...] = acc

    @pl.when(j == nj - 1)
    def _():
        kv_new, _ = new_token_tile(nselT_ref)
        sel_new = sel_ref[0, n_past_blk:n_past_blk + 1, :] > 0.5
        m2, acc2 = keys_update(kv_new, key_is0 & sel_new, m_s[...], acc_s[...])
        _scatter_heads(acc2[0:KV_HALF, :] / acc2[KV_HALF:KV_HALF + 1, :], ots_ref, b, n)
        win_t = win_ref[0]
        w_keys = win_t.shape[1]
        kv_new, new_col = new_token_tile(nwinT_ref)
        valid = jnp.concatenate([jnp.full((w_keys, 1), True), key_is0], axis=0)
        _, aw = keys_update(jnp.concatenate([win_t, kv_new], axis=1), valid,
                            jnp.full(m_s.shape, NEG, F32), jnp.zeros_like(acc2))
        _scatter_heads(aw[0:KV_HALF, :] / aw[KV_HALF:KV_HALF + 1, :], otw_ref, b, n)
        lane_w = lax.broadcasted_iota(jnp.int32, (1, w_keys), 1)
        nwo_ref[0] = jnp.where(lane_w == w_keys - 1, new_col, pltpu.roll(win_t, w_keys - 1, 1))


def _nsa_s_att(page_table, cache_sel, qrT, sel, nsel_rows, cache_win, nwin_rows, past_len):
    nseq = page_table.shape[0]
    nj = past_len // (PAGES_PER_STEP * PAGE_SIZE)
    wbuf = cache_win.shape[2]
    c2 = lambda a: pl.BlockSpec(a.shape, lambda b, j, pt: (0, 0))
    c3 = lambda a: pl.BlockSpec(a.shape, lambda b, j, pt: (0, 0, 0))
    acc_rows = KV_HALF + 16
    grid_spec = pltpu.PrefetchScalarGridSpec(
        num_scalar_prefetch=1,
        grid=(nseq, nj),
        in_specs=_page_specs(PAGES_PER_STEP) + [
            c3(qrT),
            pl.BlockSpec((1,) + sel.shape[1:], lambda b, j, pt: (b, 0, 0)),
            c2(nsel_rows),
            pl.BlockSpec((1, KV_COLS, wbuf), lambda b, j, pt: (b, 0, 0)),
            c2(nwin_rows)],
        out_specs=[pl.BlockSpec((D_MODEL, nseq), lambda b, j, pt: (0, 0)),
                   pl.BlockSpec((D_MODEL, nseq), lambda b, j, pt: (0, 0)),
                   pl.BlockSpec((1, KV_COLS, wbuf), lambda b, j, pt: (b, 0, 0))],
        scratch_shapes=[pltpu.VMEM((KV_HALF, nseq), BF16), pltpu.VMEM((1, nseq), F32),
                        pltpu.VMEM((acc_rows, nseq), F32)],
    )
    return pl.pallas_call(
        functools.partial(_nsa_s_att_kernel, past_len // SEL_BLOCK),
        grid_spec=grid_spec,
        out_shape=[jax.ShapeDtypeStruct((D_MODEL, nseq), F32), jax.ShapeDtypeStruct((D_MODEL, nseq), F32),
                   jax.ShapeDtypeStruct((nseq, KV_COLS, wbuf), F32)],
        compiler_params=_cparams(("arbitrary", "arbitrary")),
        name="nsa_sample_att",
    )(page_table, *([cache_sel] * PAGES_PER_STEP), qrT, sel, nsel_rows, cache_win, nwin_rows)


def _nsa_s_out_kernel(otc_ref, ots_ref, otw_ref, gT_ref, wn_ref, o_ref):
    n = otc_ref.shape[1]
    parts = []
    for hd in range(NSA_HEADS):
        rows = slice(hd * NSA_HEAD_DIM, (hd + 1) * NSA_HEAD_DIM)
        parts.append(gT_ref[0, 3 * hd:3 * hd + 1, :] * otc_ref[rows, :]
                     + gT_ref[0, 3 * hd + 1:3 * hd + 2, :] * ots_ref[rows, :]
                     + gT_ref[0, 3 * hd + 2:3 * hd + 3, :] * otw_ref[rows, :])
    o_t = _bf(jnp.concatenate(parts, axis=0))
    o_ref[...] = _dot_tn(o_t, wn_ref[...])


def _nsa_s_out(otc, ots, otw, gT, wn):
    n = otc.shape[1]
    f2 = lambda a: pl.BlockSpec(a.shape, lambda i: (0, 0))
    f3 = lambda a: pl.BlockSpec(a.shape, lambda i: (0, 0, 0))
    return pl.pallas_call(
        _nsa_s_out_kernel,
        grid=(1,),
        in_specs=[f2(otc), f2(ots), f2(otw), f3(gT), f2(wn)],
        out_specs=pl.BlockSpec((n, D_MODEL), lambda i: (0, 0)),
        out_shape=jax.ShapeDtypeStruct((n, D_MODEL), F32),
        compiler_params=_cparams(("arbitrary",)),
        name="nsa_sample_out",
    )(otc, ots, otw, gT, wn)


def _rope_tables(pos):
    half = ROPE_DIM // 2
    inv_freq = jnp.power(ROPE_THETA, -jnp.arange(half, dtype=F32) * 2.0 / ROPE_DIM)
    ang = pos.astype(F32)[None, :] * inv_freq[:, None]
    return jnp.cos(ang), jnp.sin(ang)


def _permute_w_in(w_in):
    sizes = (SSM_D_INNER, SSM_CONV_DIM, SSM_HEADS, NSA_HEADS * NSA_HEAD_DIM, KV_COLS, KV_COLS, KV_COLS,
             3 * NSA_HEADS, 2 * D_MODEL)
    offs = np.concatenate([[0], np.cumsum(sizes)])
    z, xbc, dt, q, kvc, kvs, kvw, ng, mg = (w_in[:, offs[i]:offs[i + 1]] for i in range(len(sizes)))
    pad = jnp.zeros((D_MODEL, PROJ_COLS - COL_SMALL - SSM_HEADS - 3 * NSA_HEADS), w_in.dtype)
    return _bf(jnp.concatenate([xbc, q, z, mg, kvc, kvs, kvw, dt, ng, pad], axis=1))


def _compress_weights(cmp_pe, cmp_w1, cmp_w2):
    eye = jnp.eye(NSA_KV_HEADS, dtype=F32)
    w1a = cmp_w1[:, :CMP_STRIDE]
    w1b = cmp_w1[:, CMP_STRIDE:]
    bd = lambda w: jnp.einsum('vlde,hk->vlhdke', w, eye).reshape(2, CMP_STRIDE * KV_HALF, KV_HALF)
    wcat = _bf(jnp.concatenate([bd(w1a), bd(w1b)], axis=2))
    w2bd = _bf(jnp.einsum('vef,hk->vhekf', cmp_w2, eye).reshape(2, KV_HALF, KV_HALF))
    pe8 = jnp.zeros((2, 8, CMP_LEN * NSA_HEAD_DIM), F32).at[:, 0, :].set(cmp_pe.reshape(2, -1))
    w1f4 = jnp.tile(cmp_w1.reshape(2, CMP_LEN * NSA_HEAD_DIM, NSA_HEAD_DIM), (1, 1, NSA_KV_HEADS))
    return wcat, w2bd, pe8, w1f4


def kernel(x_prompt, x_sample, cache_cmp_kv, cache_sel_kv, cache_win_kv, state_ssm, state_conv, page_table,
           w_in, conv_w, conv_b, dt_bias, a_log, d_skip, ssm_norm_w, w_ssm_out, cmp_pe, cmp_w1, cmp_w2,
           w_nsa_out, w_o, ln1_g, ln1_b, w_gate, w_up, w_down, ln2_g, ln2_b):
    bsz, t, _ = x_prompt.shape
    nseq, dec_seq, _ = x_sample.shape
    n_pool = cache_cmp_kv.shape[1]
    past_len = page_table.shape[1] * PAGE_SIZE
    assert w_in.shape[0] == 1 and dec_seq == 1 and nseq == LANES
    assert t % CMP_CHUNK == 0 and past_len % CMP_CHUNK == 0 and cache_win_kv.shape[2] == WINDOW

    w_in_p = _permute_w_in(w_in[0])
    pad_row = lambda v: jnp.zeros((1, LANES), F32).at[0, :SSM_HEADS].set(v)
    dtb_pad, alog_pad = pad_row(dt_bias[0]), pad_row(a_log[0])
    drow = jnp.repeat(d_skip[0], SSM_HEAD_DIM)[None, :]
    nw = ssm_norm_w[0][None, :]
    cw, cb = conv_w[0], conv_b[0][None, :]
    wcat, w2bd, pe8, w1f4 = _compress_weights(cmp_pe[0], cmp_w1[0], cmp_w2[0])
    w_ssm_b, w_nsa_b, w_o_b = _bf(w_ssm_out[0]), _bf(w_nsa_out[0]), _bf(w_o[0])
    nc = FFN_HIDDEN // FFN_CHUNK
    wg3 = _bf(w_gate[0]).reshape(D_MODEL, nc, FFN_CHUNK).transpose(1, 0, 2)
    wu3 = _bf(w_up[0]).reshape(D_MODEL, nc, FFN_CHUNK).transpose(1, 0, 2)
    wd3 = _bf(w_down[0]).reshape(nc, FFN_CHUNK, D_MODEL)
    g1, b1, g2, b2 = ln1_g[0][None, :], ln1_b[0][None, :], ln2_g[0][None, :], ln2_b[0][None, :]

    xp = x_prompt.reshape(bsz * t, D_MODEL)
    proj = _matmul(xp, w_in_p, 2048, PROJ_TN, "in_proj")
    y_ssm, new_ssm_p, new_conv_p = _ssd_prompt(proj, bsz, t, cw, cb, dtb_pad, alog_pad, drow, nw)
    cos_p, sin_p = _rope_tables(jnp.arange(t, dtype=jnp.int32))
    qT, qrT, ks, vsT, kw, vwT, gT, ncmp, nsel, nwin = _attn_prep(proj, bsz, t, KEY_SLAB, cos_p, sin_p)
    cmp_bias = _pe_bias(pe8, w1f4)
    kc, vcT = _compress_prompt(proj, bsz, t, wcat, w2bd, cmp_bias)
    b_nsa = _nsa_prompt(qT, qrT, gT, kc, vcT, ks, vsT, kw, vwT, w_nsa_b, bsz, t)
    h = _merge(xp, proj, y_ssm, b_nsa, w_ssm_b, w_o_b, g1, b1, 512)
    y_p = _ffn(h, wg3, wu3, wd3, g2, b2, 512).reshape(bsz, t, D_MODEL)

    def kv6(a_t):
        n_b, _, n_t = a_t.shape
        return jnp.moveaxis(a_t.reshape(n_b, 2, NSA_KV_HEADS, NSA_HEAD_DIM, n_t), -1, 1)[None]

    w_keep = min(WINDOW, t)
    new_win_p = kv6(nwin[:, :, t - w_keep:])

    xs_in = x_sample.reshape(nseq, D_MODEL)
    proj_s = _matmul(xs_in, w_in_p, nseq, PROJ_TN, "in_proj_s")
    cst = jnp.moveaxis(state_conv[0], 1, 0)
    xs_s, xdt_s, bm_s, cm_s, dec_s, ncv_s = _ssd_s_pre(proj_s, cst, cw, cb, dtb_pad, alog_pad)
    new_ssm_s, yT_s = _ssd_s_state(state_ssm[0], xdt_s, bm_s, cm_s, dec_s)
    y_ssm_s = _ssd_s_post(yT_s, xs_s, proj_s, drow, nw)
    cos_s, sin_s = _rope_tables(jnp.full((nseq,), past_len, dtype=jnp.int32))
    qT_s, qrT_s, _, _, _, _, gT_s, ncmp_s, nsel_s, nwin_s = _attn_prep(proj_s, 1, nseq, nseq, cos_s, sin_s)
    fmaj = lambda c, n_lead: jnp.moveaxis(c, 1, -1).reshape(n_lead, KV_COLS, c.shape[1])
    cache_cmp = fmaj(cache_cmp_kv[0], n_pool)
    cache_sel = fmaj(cache_sel_kv[0], n_pool)
    cache_win = fmaj(cache_win_kv[0], nseq)
    otc, sel = _nsa_s_cmp(page_table, cache_cmp, qT_s, wcat, w2bd, cmp_bias, past_len)
    ots, otw, new_win_t = _nsa_s_att(page_table, cache_sel, qrT_s, sel, nsel_s[0], cache_win, nwin_s[0], past_len)
    kv6_s = lambda a_t: kv6(jnp.transpose(a_t, (2, 1, 0)))
    b_nsa_s = _nsa_s_out(otc, ots, otw, gT_s, w_nsa_b)
    h_s = _merge(xs_in, proj_s, y_ssm_s, b_nsa_s, w_ssm_b, w_o_b, g1, b1, nseq)
    y_s = _ffn(h_s, wg3, wu3, wd3, g2, b2, nseq).reshape(nseq, 1, D_MODEL)

    return (y_p, y_s,
            kv6(ncmp), kv6(nsel), new_win_p,
            new_ssm_p[None], new_conv_p[None],
            kv6_s(ncmp_s), kv6_s(nsel_s),
            kv6(new_win_t),
            new_ssm_s[None],
            jnp.moveaxis(ncv_s, 0, 1)[None])
```

```python
import functools
import math

import numpy as np
import jax
import jax.numpy as jnp
from jax import lax
from jax.experimental import pallas as pl
from jax.experimental.pallas import tpu as pltpu

F32 = jnp.float32
BF16 = jnp.bfloat16

D_MODEL = 1024
SSM_D_INNER = 2048
SSM_HEAD_DIM = 64
SSM_HEADS = 32
SSM_GROUPS = 4
SSM_D_STATE = 128
SSM_CONV_W = 4
SSM_CONV_DIM = 3072
SSM_CHUNK = 128
NSA_HEADS = 16
NSA_KV_HEADS = 4
NSA_HEAD_DIM = 64
NSA_GQA = 4
KV_COLS = 512
KV_HALF = 256
CMP_LEN = 32
CMP_STRIDE = 16
SEL_BLOCK = 64
SEL_TOP_N = 16
WINDOW = 512
Q_BLOCK = 128
ROPE_DIM = 16
ROPE_THETA = 500000.0
FFN_HIDDEN = 2816
NORM_EPS = 1e-5
BIG = 1e30
NEG = -1e30
DEPTH = 1
DEEPNORM_ALPHA = (2 * DEPTH) ** 0.25
PAGE_SIZE = 128

LANES = 128
VMEM_LIMIT_BYTES = 56 * 1024 * 1024

COL_XBC = 0
COL_Q = 3072
COL_Z = 4096
COL_MG = 6144
COL_KVC = 8192
COL_KVS = 8704
COL_KVW = 9216
COL_SMALL = 9728
PROJ_COLS = 9984
PROJ_TN = 768

Q_SCALE = NSA_HEAD_DIM ** -0.5 * math.log2(math.e)
KEY_SLAB = 512
BLOCKS_PER_SLAB = KEY_SLAB // SEL_BLOCK
CHUNKS_PER_SLAB = KEY_SLAB // Q_BLOCK
CHUNK_SHIFT = 2
assert 1 << CHUNK_SHIFT == CHUNKS_PER_SLAB
QCHUNK = 256
N_QCHUNK = NSA_GQA * Q_BLOCK // QCHUNK
CMP_ROW_STEP = 128
CMP_QB_SHIFT = 4
assert (Q_BLOCK // CMP_STRIDE) << CMP_QB_SHIFT == CMP_ROW_STEP
SLABS_PER_TRIP = 2
assert SLABS_PER_TRIP & (SLABS_PER_TRIP - 1) == 0
CHUNKS_PER_ITEM = 1
MXU_LOOKAHEAD = 12
CMP_CHUNK = 2048
FFN_CHUNK = 256


def _dot_dims(a, b, dims):
    return lax.dot_general(a, b, (dims, ((), ())), preferred_element_type=F32)


def _dot(a, b):
    return _dot_dims(a, b, ((1,), (0,)))


def _dot_nt(a, b):
    return _dot_dims(a, b, ((1,), (1,)))


def _dot_tn(a, b):
    return _dot_dims(a, b, ((0,), (0,)))


def _bf(x):
    return x.astype(BF16)


def _split3(x):
    hi = _bf(x)
    r1 = x - hi.astype(F32)
    mid = _bf(r1)
    lo = _bf(r1 - mid.astype(F32))
    return hi, mid, lo


def _dot_exact_lhs(x, w_bf16):
    hi, mid, lo = _split3(x)
    return _dot(hi, w_bf16) + _dot(mid, w_bf16) + _dot(lo, w_bf16)


def _dot_exact_rhs(w_bf16, x):
    hi, mid, lo = _split3(x)
    return _dot(w_bf16, hi) + _dot(w_bf16, mid) + _dot(w_bf16, lo)


def _silu(x):
    h = 0.5 * x
    return h + h * jnp.tanh(h)


def _softplus(x):
    e = jnp.exp(-jnp.abs(x))
    u = 1.0 + e
    log1p_e = jnp.where(u == 1.0, e, jnp.log(u) * e / jnp.where(u == 1.0, 1.0, u - 1.0))
    return jnp.maximum(x, 0.0) + log1p_e


def _gelu_tanh(x):
    return 0.5 * x * (1.0 + jnp.tanh(0.7978845608028654 * (x + 0.044715 * (x * x * x))))


def _layer_norm(x, g, b):
    mu = jnp.mean(x, axis=-1, keepdims=True)
    xc = x - mu
    var = jnp.mean(xc * xc, axis=-1, keepdims=True)
    return xc * lax.rsqrt(var + NORM_EPS) * g + b


def _cparams(sem, flags=None):
    return pltpu.CompilerParams(dimension_semantics=sem, vmem_limit_bytes=VMEM_LIMIT_BYTES, flags=flags)


def _mm_kernel(x_ref, w_ref, o_ref):
    o_ref[...] = _dot(_bf(x_ref[...]), w_ref[...])


def _matmul(x, w, tm, tn, name):
    m, k = x.shape
    n = w.shape[1]
    return pl.pallas_call(
        _mm_kernel,
        grid=(m // tm, n // tn),
        in_specs=[pl.BlockSpec((tm, k), lambda i, j: (i, 0)),
                  pl.BlockSpec((k, tn), lambda i, j: (0, j))],
        out_specs=pl.BlockSpec((tm, tn), lambda i, j: (i, j)),
        out_shape=jax.ShapeDtypeStruct((m, n), F32),
        compiler_params=_cparams(("parallel", "arbitrary")),
        name=name,
    )(x, w)


def _ssd_kernel(xbc_ref, z_ref, sm_ref, cw_ref, cb_ref, dtb_ref, alog_ref, drow_ref, nw_ref, r3_ref,
                y_ref, st_ref, cv_ref, xp_s, stT_s):
    j = pl.program_id(1)
    nj = pl.num_programs(1)
    q = SSM_CHUNK

    @pl.when(j == 0)
    def _():
        xp_s[0:8, :] = jnp.zeros((8, SSM_CONV_DIM), F32)
        stT_s[...] = jnp.zeros_like(stT_s)

    xp_s[8:8 + q, :] = xbc_ref[...]
    acc = cb_ref[...]
    for k in range(SSM_CONV_W):
        acc = acc + cw_ref[k:k + 1, :] * xp_s[5 + k:5 + k + q, :]
    act = _silu(acc)
    tail = xp_s[q + 5:q + 8, :]
    xp_s[5:8, :] = tail

    @pl.when(j == nj - 1)
    def _():
        cv_ref[0] = tail

    lane = lax.broadcasted_iota(jnp.int32, (1, LANES), 1)
    a_full = jnp.where(lane < SSM_HEADS, -jnp.exp(alog_ref[...]), 0.0)
    dt = _softplus(sm_ref[...] + dtb_ref[...])
    da = dt * a_full
    row_i = lax.broadcasted_iota(jnp.int32, (q, q), 0)
    col_i = lax.broadcasted_iota(jnp.int32, (q, q), 1)
    causal = col_i <= row_i
    tril = jnp.where(causal, 1.0, 0.0).astype(BF16)
    cum = _dot_exact_rhs(tril, da)
    cumT = cum.T
    dtT = dt.T
    hi, mid, lo = _split3(cum)
    packed = _bf(hi.astype(F32) + pltpu.roll(mid.astype(F32), 32, 1) + pltpu.roll(lo.astype(F32), 64, 1))
    lane_lo = lax.broadcasted_iota(jnp.int32, (1, LANES), 1) < SSM_HEAD_DIM

    heads_per_group = SSM_HEADS // SSM_GROUPS
    pairs_per_group = heads_per_group // 2
    n_pairs = SSM_HEADS // 2
    cmbs, cbs, bTs = [], [], []
    for g in range(SSM_GROUPS):
        bm_g = act[:, SSM_D_INNER + g * SSM_D_STATE:SSM_D_INNER + (g + 1) * SSM_D_STATE]
        cm_g = act[:, SSM_D_INNER + SSM_GROUPS * SSM_D_STATE + g * SSM_D_STATE:
                   SSM_D_INNER + SSM_GROUPS * SSM_D_STATE + (g + 1) * SSM_D_STATE]
        cmbs.append(_bf(cm_g))
        cbs.append(_dot_nt(cmbs[g], _bf(bm_g)))
        bTs.append(bm_g.T)
    cols_all = [_dot(packed, r3_ref[:, pair * 2 * LANES:(pair + 1) * 2 * LANES]) for pair in range(n_pairs)]
    y_inter = [_dot(cmbs[pair // pairs_per_group], _bf(stT_s[:, pair * LANES:(pair + 1) * LANES]))
               for pair in range(n_pairs)]

    y_parts = []
    for g in range(SSM_GROUPS):
        cmb, cb, bT = cmbs[g], cbs[g], bTs[g]
        for pp in range(pairs_per_group):
            pair = g * pairs_per_group + pp
            h0 = 2 * pair
            xs_pair = act[:, pair * LANES:(pair + 1) * LANES]
            xs_a = _bf(jnp.where(lane_lo, xs_pair, 0.0))
            xs_b = _bf(jnp.where(lane_lo, 0.0, xs_pair))
            cols2 = cols_all[pair]
            y_pair = None
            ds_pair = None
            lasts = []
            cols = []
            for hh, xs_m in ((0, xs_a), (1, xs_b)):
                h = h0 + hh
                col = cols2[:, hh * LANES:(hh + 1) * LANES]
                row = cumT[h:h + 1, :]
                dtrow = dtT[h:h + 1, :]
                dec = jnp.exp(jnp.where(causal, col - row, NEG))
                m_h = _bf(cb * dec * dtrow)
                y_h = _dot(m_h, xs_m)
                last = col[q - 1:q, :]
                wrow = jnp.exp(last - row) * dtrow
                ds_h = _dot(_bf(bT * wrow), xs_m)
                y_pair = y_h if y_pair is None else y_pair + y_h
                ds_pair = ds_h if ds_pair is None else ds_pair + ds_h
                lasts.append(last)
                cols.append(col)
            st_pair = stT_s[:, pair * LANES:(pair + 1) * LANES]
            scale_t = jnp.exp(jnp.where(lane_lo, cols[0], cols[1]))
            y_pair = y_pair + y_inter[pair] * scale_t
            stT_s[:, pair * LANES:(pair + 1) * LANES] = (
                st_pair * jnp.exp(jnp.where(lane_lo, lasts[0], lasts[1])) + ds_pair)
            y_pair = y_pair + drow_ref[:, pair * LANES:(pair + 1) * LANES] * xs_pair
            y_parts.append(y_pair)
    y = jnp.concatenate(y_parts, axis=1)
    v = y * _silu(z_ref[...])
    gw = SSM_D_INNER // SSM_GROUPS
    outs = []
    for g in range(SSM_GROUPS):
        vg = v[:, g * gw:(g + 1) * gw]
        ms = jnp.sum(vg * vg, axis=-1, keepdims=True) * (1.0 / gw)
        outs.append(vg * lax.rsqrt(ms + NORM_EPS) * nw_ref[:, g * gw:(g + 1) * gw])
    y_ref[...] = _bf(jnp.concatenate(outs, axis=1))

    @pl.when(j == nj - 1)
    def _():
        st_ref[0] = stT_s[...].T.reshape(SSM_HEADS, SSM_HEAD_DIM, SSM_D_STATE)


def _r3_table():
    k = np.arange(LANES)[:, None]
    c = np.arange(SSM_HEADS * LANES)[None, :]
    return jnp.asarray(((k % SSM_HEADS) == (c // LANES)) & (k < 3 * SSM_HEADS), dtype=BF16)


def _ssd_prompt(proj, bsz, t, cw, cb, dtb_pad, alog_pad, drow, nw):
    nch = t // SSM_CHUNK
    q = SSM_CHUNK
    row = lambda b, j: b * nch + j
    const = lambda shape: pl.BlockSpec(shape, lambda b, j: (0, 0))
    return pl.pallas_call(
        _ssd_kernel,
        grid=(bsz, nch),
        in_specs=[
            pl.BlockSpec((q, SSM_CONV_DIM), lambda b, j: (row(b, j), COL_XBC // SSM_CONV_DIM)),
            pl.BlockSpec((q, SSM_D_INNER), lambda b, j: (row(b, j), COL_Z // SSM_D_INNER)),
            pl.BlockSpec((q, LANES), lambda b, j: (row(b, j), COL_SMALL // LANES)),
            const((SSM_CONV_W, SSM_CONV_DIM)), const((1, SSM_CONV_DIM)),
            const((1, LANES)), const((1, LANES)), const((1, SSM_D_INNER)), const((1, SSM_D_INNER)),
            const((LANES, SSM_HEADS * LANES)),
        ],
        out_specs=[
            pl.BlockSpec((q, SSM_D_INNER), lambda b, j: (row(b, j), 0)),
            pl.BlockSpec((1, SSM_HEADS, SSM_HEAD_DIM, SSM_D_STATE), lambda b, j: (b, 0, 0, 0)),
            pl.BlockSpec((1, SSM_CONV_W - 1, SSM_CONV_DIM), lambda b, j: (b, 0, 0)),
        ],
        out_shape=[
            jax.ShapeDtypeStruct((bsz * t, SSM_D_INNER), BF16),
            jax.ShapeDtypeStruct((bsz, SSM_HEADS, SSM_HEAD_DIM, SSM_D_STATE), F32),
            jax.ShapeDtypeStruct((bsz, SSM_CONV_W - 1, SSM_CONV_DIM), F32),
        ],
        scratch_shapes=[pltpu.VMEM((q + 8, SSM_CONV_DIM), F32),
                        pltpu.VMEM((SSM_D_STATE, SSM_D_INNER), F32)],
        compiler_params=_cparams(("parallel", "arbitrary")),
        name="ssd_prompt",
    )(proj, proj, proj, cw, cb, dtb_pad, alog_pad, drow, nw, _r3_table())


def _rope_t(x_t, nh, c, s):
    n = x_t.shape[1]
    x3 = x_t.reshape(nh, NSA_HEAD_DIM, n)
    half = ROPE_DIM // 2
    x1 = x3[:, 0:half, :]
    x2 = x3[:, half:ROPE_DIM, :]
    r1 = x1 * c - x2 * s
    r2 = x2 * c + x1 * s
    return jnp.concatenate([r1, r2, x3[:, ROPE_DIM:, :]], axis=1).reshape(nh * NSA_HEAD_DIM, n)


def _prep_kernel(q_ref, kvc_ref, kvs_ref, kvw_ref, sm_ref, cos_ref, sin_ref,
                 qT_ref, qrT_ref, ks_ref, vsT_ref, kw_ref, vwT_ref, gT_ref, ncmp_ref, nsel_ref, nwin_ref):
    c = cos_ref[...]
    s = sin_ref[...]
    tt = q_ref.shape[0]
    q_t = q_ref[...].T
    qT_ref[0] = _bf(q_t * Q_SCALE)
    qrT_ref[0] = _bf(_rope_t(q_t, NSA_HEADS, c, s) * Q_SCALE)
    ncmp_ref[0] = kvc_ref[...].T
    for src, full_out, k_out, vt_out in ((kvs_ref, nsel_ref, ks_ref, vsT_ref), (kvw_ref, nwin_ref, kw_ref, vwT_ref)):
        kv = src[...]
        k_rot_t = _rope_t(kv[:, :KV_HALF].T, NSA_KV_HEADS, c, s)
        k_rot = k_rot_t.T
        v_t = kv[:, KV_HALF:].T
        full_out[0, :KV_HALF, :] = k_rot_t
        full_out[0, KV_HALF:, :] = v_t
        for i in range(tt // Q_BLOCK):
            k_out[0, i] = _bf(k_rot[i * Q_BLOCK:(i + 1) * Q_BLOCK, :])
            vt_out[0, i] = _bf(v_t[:, i * Q_BLOCK:(i + 1) * Q_BLOCK])
    g_t = jax.nn.sigmoid(sm_ref[...]).T
    gT_ref[0] = g_t[SSM_HEADS:SSM_HEADS + 3 * NSA_HEADS, :]


def _attn_prep(proj, bsz, t, tt, cos_t, sin_t):
    nt = t // tt
    nb = tt // Q_BLOCK
    row = lambda b, j: b * nt + j
    n_gate = 3 * NSA_HEADS
    return pl.pallas_call(
        _prep_kernel,
        grid=(bsz, nt),
        in_specs=[
            pl.BlockSpec((tt, D_MODEL), lambda b, j: (row(b, j), COL_Q // D_MODEL)),
            pl.BlockSpec((tt, KV_COLS), lambda b, j: (row(b, j), COL_KVC // KV_COLS)),
            pl.BlockSpec((tt, KV_COLS), lambda b, j: (row(b, j), COL_KVS // KV_COLS)),
            pl.BlockSpec((tt, KV_COLS), lambda b, j: (row(b, j), COL_KVW // KV_COLS)),
            pl.BlockSpec((tt, LANES), lambda b, j: (row(b, j), COL_SMALL // LANES)),
            pl.BlockSpec((ROPE_DIM // 2, tt), lambda b, j: (0, j)),
            pl.BlockSpec((ROPE_DIM // 2, tt), lambda b, j: (0, j)),
        ],
        out_specs=[
            pl.BlockSpec((1, D_MODEL, tt), lambda b, j: (b, 0, j)),
            pl.BlockSpec((1, D_MODEL, tt), lambda b, j: (b, 0, j)),
            pl.BlockSpec((1, nb, Q_BLOCK, KV_HALF), lambda b, j: (b, j, 0, 0)),
            pl.BlockSpec((1, nb, KV_HALF, Q_BLOCK), lambda b, j: (b, j, 0, 0)),
            pl.BlockSpec((1, nb, Q_BLOCK, KV_HALF), lambda b, j: (b, j, 0, 0)),
            pl.BlockSpec((1, nb, KV_HALF, Q_BLOCK), lambda b, j: (b, j, 0, 0)),
            pl.BlockSpec((1, n_gate, tt), lambda b, j: (b, 0, j)),
            pl.BlockSpec((1, KV_COLS, tt), lambda b, j: (b, 0, j)),
            pl.BlockSpec((1, KV_COLS, tt), lambda b, j: (b, 0, j)),
            pl.BlockSpec((1, KV_COLS, tt), lambda b, j: (b, 0, j)),
        ],
        out_shape=[
            jax.ShapeDtypeStruct((bsz, D_MODEL, t), BF16),
            jax.ShapeDtypeStruct((bsz, D_MODEL, t), BF16),
            jax.ShapeDtypeStruct((bsz, t // Q_BLOCK, Q_BLOCK, KV_HALF), BF16),
            jax.ShapeDtypeStruct((bsz, t // Q_BLOCK, KV_HALF, Q_BLOCK), BF16),
            jax.ShapeDtypeStruct((bsz, t // Q_BLOCK, Q_BLOCK, KV_HALF), BF16),
            jax.ShapeDtypeStruct((bsz, t // Q_BLOCK, KV_HALF, Q_BLOCK), BF16),
            jax.ShapeDtypeStruct((bsz, n_gate, t), F32),
            jax.ShapeDtypeStruct((bsz, KV_COLS, t), F32),
            jax.ShapeDtypeStruct((bsz, KV_COLS, t), F32),
            jax.ShapeDtypeStruct((bsz, KV_COLS, t), F32),
        ],
        compiler_params=_cparams(("parallel", "parallel")),
        name="attn_prep",
    )(proj, proj, proj, proj, proj, cos_t, sin_t)


def _compress_partial(data_refs, wcat_ref, xcat_s):
    ns = data_refs[0].shape[0] // CMP_STRIDE
    tiles_per_half = KV_HALF // LANES
    outs = []
    for lp in range(CMP_STRIDE):
        for c, ref in enumerate(data_refs):
            rows = ref[pl.ds(lp, ns, stride=CMP_STRIDE), :]
            v, cc = divmod(c, tiles_per_half)
            xcat_s[v, :, lp * KV_HALF + cc * LANES:lp * KV_HALF + (cc + 1) * LANES] = _bf(rows)
    for v in range(2):
        outs.append(_dot(xcat_s[v], wcat_ref[v]))
    return outs


def _compress_finish(pab_s, bias, w2bd_ref):
    n_sub = pab_s.shape[1]
    res = []
    for v in range(2):
        pab = pab_s[v]
        pre = pab[:, :KV_HALF] + pltpu.roll(pab[:, KV_HALF:], n_sub - 1, 0) + bias[v:v + 1, :]
        res.append(_dot(_bf(_gelu_tanh(pre)), w2bd_ref[v]))
    return res


def _pe_bias_kernel(pe_ref, w1f_ref, o_ref):
    rows = []
    for v in range(2):
        rows.append(jnp.dot(pe_ref[v], w1f_ref[v], preferred_element_type=F32,
                            precision=lax.Precision.HIGHEST)[0:1, :])
    o_ref[...] = jnp.concatenate(rows + [jnp.zeros((6, KV_HALF), F32)], axis=0)


def _pe_bias(pe8, w1f4):
    c3 = lambda a: pl.BlockSpec(a.shape, lambda i: (0, 0, 0))
    return pl.pallas_call(
        _pe_bias_kernel, grid=(1,), in_specs=[c3(pe8), c3(w1f4)],
        out_specs=pl.BlockSpec((8, KV_HALF), lambda i: (0, 0)),
        out_shape=jax.ShapeDtypeStruct((8, KV_HALF), F32),
        compiler_params=_cparams(("arbitrary",)), name="cmp_pe_bias",
    )(pe8, w1f4)


def _compress_prompt_kernel(d0_ref, d1_ref, d2_ref, d3_ref, wcat_ref, w2bd_ref, bias_ref,
                            kc_ref, vcT_ref, xcat_s, pab_s):
    j = pl.program_id(1)
    nj = pl.num_programs(1)
    ns = CMP_CHUNK // CMP_STRIDE
    parts = _compress_partial((d0_ref, d1_ref, d2_ref, d3_ref), wcat_ref, xcat_s)
    for v in range(2):
        pab_s[v, pl.ds(pl.multiple_of(j * ns, ns), ns), :] = parts[v]

    @pl.when(j == nj - 1)
    def _():
        kc, vc = _compress_finish(pab_s, bias_ref[...], w2bd_ref)
        kc_ref[0] = _bf(kc)
        vcT_ref[0] = _bf(vc.T)


def _compress_prompt(proj, bsz, t, wcat, w2bd, bias):
    nj = t // CMP_CHUNK
    n_sub = t // CMP_STRIDE
    ns = CMP_CHUNK // CMP_STRIDE
    c3 = lambda shape: pl.BlockSpec(shape, lambda b, j: (0, 0, 0))
    return pl.pallas_call(
        _compress_prompt_kernel,
        grid=(bsz, nj),
        in_specs=[pl.BlockSpec((CMP_CHUNK, LANES),
                               functools.partial(lambda b, j, c: (b * nj + j, COL_KVC // LANES + c), c=c))
                  for c in range(KV_COLS // LANES)]
        + [c3(wcat.shape), c3(w2bd.shape), pl.BlockSpec(bias.shape, lambda b, j: (0, 0))],
        out_specs=[pl.BlockSpec((1, n_sub, KV_HALF), lambda b, j: (b, 0, 0)),
                   pl.BlockSpec((1, KV_HALF, n_sub), lambda b, j: (b, 0, 0))],
        out_shape=[jax.ShapeDtypeStruct((bsz, n_sub, KV_HALF), BF16),
                   jax.ShapeDtypeStruct((bsz, KV_HALF, n_sub), BF16)],
        scratch_shapes=[pltpu.VMEM((2, ns, CMP_STRIDE * KV_HALF), BF16),
                        pltpu.VMEM((2, n_sub, KV_COLS), F32)],
        compiler_params=_cparams(("parallel", "arbitrary")),
        name="compress_prompt",
    )(proj, proj, proj, proj, wcat, w2bd, bias)


def _topk_mask(score, jrow, n_pick):
    sel = jnp.zeros_like(score)
    for _ in range(n_pick):
        mx = jnp.max(score, axis=0, keepdims=True)
        idx = jnp.min(jnp.where(score == mx, jrow, 1e9), axis=0, keepdims=True)
        chosen = jrow == idx
        sel = jnp.where(chosen, 1.0, sel)
        score = jnp.where(chosen, -jnp.inf, score)
    return sel


def _online_step(st, vt_aug, m, acc):
    mn = jnp.maximum(m, jnp.max(st, axis=0, keepdims=True))
    p = jnp.exp2(st - mn)
    acc = acc * jnp.exp2(m - mn) + _dot(vt_aug, _bf(p))
    return mn, acc


def _nsa_prompt_kernel(qT_ref, qrT_ref, gT_ref, kc_ref, vcT_ref, ks_ref, vsT_ref, kw_ref, vwT_ref, smapT_ref,
                       econst_ref, wn_ref, o_ref, qr_s, neg_s, oc_s, m_s, acc_s, imp_s):
    qb = pl.program_id(1)
    nq = NSA_GQA * Q_BLOCK
    n_cmp_rows = kc_ref.shape[1]
    n_selblk = smapT_ref.shape[0]
    n_slab = n_selblk // BLOCKS_PER_SLAB
    lane_q = lax.broadcasted_iota(jnp.int32, (1, nq), 1) % Q_BLOCK
    qpos = qb * Q_BLOCK + lane_q
    zero_q = jnp.zeros((NSA_HEAD_DIM, nq), BF16)
    acc_rows = NSA_HEAD_DIM + 16

    jrow = lax.broadcasted_iota(jnp.int32, (n_selblk, Q_BLOCK), 0).astype(F32)
    cur = (qb * (Q_BLOCK // SEL_BLOCK)
           + lax.broadcasted_iota(jnp.int32, (1, Q_BLOCK), 1) // SEL_BLOCK).astype(F32)
    future = jrow > cur
    forced = jnp.where(jrow == 0.0, 1.0, 0.0) + jnp.where(jrow == cur, 1.0, 0.0) + jnp.where(jrow == cur - 1.0, 1.0, 0.0)
    crow = lax.broadcasted_iota(jnp.int32, (n_cmp_rows, 1), 0)
    cmp_valid = (crow * CMP_STRIDE + (CMP_LEN - 1)) <= qpos

    def q_cat(ref, h):
        return jnp.concatenate([ref[0, (h * NSA_GQA + g) * NSA_HEAD_DIM:(h * NSA_GQA + g + 1) * NSA_HEAD_DIM, :]
                                for g in range(NSA_GQA)], axis=1)

    def cmp_branch(nr):
        valid = cmp_valid[0:nr, :]
        for h in range(NSA_KV_HEADS):
            qp = jnp.concatenate([q_cat(qT_ref, h) if hh == h else zero_q for hh in range(NSA_KV_HEADS)], axis=0)
            sc = jnp.where(valid, _dot(kc_ref[0, 0:nr, :], qp), NEG)
            m_c = jnp.max(sc, axis=0, keepdims=True)
            e_c = jnp.where(valid, jnp.exp2(sc - m_c), 0.0)
            inv_l = 1.0 / jnp.maximum(jnp.sum(e_c, axis=0, keepdims=True), 1e-30)
            p_c = e_c * inv_l
            oc_s[h] = _dot(vcT_ref[0, h * NSA_HEAD_DIM:(h + 1) * NSA_HEAD_DIM, 0:nr], _bf(p_c))
            psum = (p_c[:, 0:Q_BLOCK] + p_c[:, Q_BLOCK:2 * Q_BLOCK]
                    + p_c[:, 2 * Q_BLOCK:3 * Q_BLOCK] + p_c[:, 3 * Q_BLOCK:4 * Q_BLOCK])
            p_hi = _bf(psum)
            p_lo = _bf(psum - p_hi.astype(F32))
            imp_s[h] = _dot(smapT_ref[:, 0:nr], p_hi) + _dot(smapT_ref[:, 0:nr], p_lo)

    n_variants = n_cmp_rows // CMP_ROW_STEP
    variant = jnp.minimum(lax.shift_right_logical(qb, CMP_QB_SHIFT), n_variants - 1)
    for k in range(n_variants):
        pl.when(variant == k)(functools.partial(cmp_branch, (k + 1) * CMP_ROW_STEP))

    for h in range(NSA_KV_HEADS):
        imp_t = imp_s[h]
        score = jnp.where(future, -BIG, jnp.where(forced > 0.5, BIG, imp_t))
        sel = _topk_mask(score, jrow, min(SEL_TOP_N, n_selblk))
        neg = jnp.where(sel > 0.5, 0.0, NEG)
        neg = jnp.concatenate([neg] * NSA_GQA, axis=1).reshape(n_slab, BLOCKS_PER_SLAB, nq)
        neg_s[h] = _bf(jnp.concatenate([neg, jnp.zeros_like(neg)], axis=1))
        qr_h = q_cat(qrT_ref, h)
        rhs_top = jnp.concatenate([qr_h, zero_q] if h % 2 == 0 else [zero_q, qr_h], axis=0)
        for qc in range(N_QCHUNK):
            cols = slice(qc * QCHUNK, (qc + 1) * QCHUNK)
            qr_s[h, qc, 0:2 * NSA_HEAD_DIM, :] = rhs_top[:, cols]
            qr_s[h, qc, 2 * NSA_HEAD_DIM:, :] = jnp.zeros((2 * NSA_HEAD_DIM, QCHUNK), BF16)
            for br in range(2):
                m_s[br, h, qc] = jnp.full((8, QCHUNK), NEG, F32)
                acc_s[br, h, qc] = jnp.zeros((acc_rows, QCHUNK), F32)

    ones16 = jnp.ones((16, Q_BLOCK), BF16)
    qpos_c = qpos[:, 0:QCHUNK]
    blk_row = lax.broadcasted_iota(jnp.int32, (Q_BLOCK, 1), 0)

    def tile_update(br, h, qc, sts, vt_ref, kcs):
        vt = jnp.concatenate([vt_ref[0, kc, h * NSA_HEAD_DIM:(h + 1) * NSA_HEAD_DIM, :] for kc in kcs], axis=1)
        vt = jnp.concatenate([vt, jnp.ones((16, len(kcs) * Q_BLOCK), BF16)], axis=0)
        st = sts[0] if len(sts) == 1 else jnp.concatenate(sts, axis=0)
        mn, acc = _online_step(st, vt, m_s[br, h, qc, 0:1, :], acc_s[br, h, qc])
        m_s[br, h, qc] = jnp.broadcast_to(mn, (8, QCHUNK))
        acc_s[br, h, qc] = acc

    def sel_scores(kc, h, qc):
        slab = lax.shift_right_logical(kc, CHUNK_SHIFT)
        qr_s[h, qc, 2 * NSA_HEAD_DIM:2 * NSA_HEAD_DIM + 16, :] = neg_s[h, slab, :, qc * QCHUNK:(qc + 1) * QCHUNK]
        pair = h // 2
        lhs = jnp.concatenate([ks_ref[0, kc, :, pair * LANES:(pair + 1) * LANES],
                               econst_ref[kc & (CHUNKS_PER_SLAB - 1)]], axis=1)
        return _dot(lhs, qr_s[h, qc])

    tiles = [(h, qc) for h in range(NSA_KV_HEADS) for qc in range(N_QCHUNK)]

    def run_pipelined(items):
        pending = []
        for score_fn, update_fn in items:
            pending.append((update_fn, score_fn()))
            if len(pending) > MXU_LOOKAHEAD:
                fn, st = pending.pop(0)
                fn(st)
        for fn, st in pending:
            fn(st)

    def sel_items(kcs, masked):
        def score(h, qc):
            sts = [sel_scores(kc, h, qc) for kc in kcs]
            if masked:
                sts = [jnp.where((kc * Q_BLOCK + blk_row) <= qpos_c, st, NEG) for kc, st in zip(kcs, sts)]
            return sts
        return [(functools.partial(score, h, qc),
                 functools.partial(lambda sts, h, qc: tile_update(0, h, qc, sts, vsT_ref, kcs), h=h, qc=qc))
                for h, qc in tiles]

    n_full = lax.shift_right_logical(qb, CHUNK_SHIFT)

    def slabs_body(n_slabs, s0):
        items = []
        for c in range(0, n_slabs * CHUNKS_PER_SLAB, CHUNKS_PER_ITEM):
            items += sel_items([s0 * CHUNKS_PER_SLAB + c + d for d in range(CHUNKS_PER_ITEM)], False)
        run_pipelined(items)

    def slab_group_body(sg, carry):
        slabs_body(SLABS_PER_TRIP, sg * SLABS_PER_TRIP)
        return carry

    def slab_body(s, carry):
        slabs_body(1, s)
        return carry

    n_groups = lax.shift_right_logical(n_full, SLABS_PER_TRIP.bit_length() - 1)
    lax.fori_loop(0, n_groups, slab_group_body, 0)
    lax.fori_loop(n_groups * SLABS_PER_TRIP, n_full, slab_body, 0)

    causal = (qb * Q_BLOCK + blk_row) <= qpos_c
    n_wblk = WINDOW // Q_BLOCK + 1

    def win_scores(i, h, qc):
        kb = qb - (n_wblk - 1) + i
        st = _dot(kw_ref[0, jnp.maximum(kb, 0), :, (h // 2) * LANES:(h // 2 + 1) * LANES],
                  qr_s[h, qc, 0:2 * NSA_HEAD_DIM, :])
        if i == 0:
            st = jnp.where((qpos_c - (kb * Q_BLOCK + blk_row)) <= WINDOW, st, NEG)
        if i == n_wblk - 1:
            return jnp.where(causal, st, NEG)
        return jnp.where(kb >= 0, st, NEG)

    items = []
    for c in range(0, CHUNKS_PER_SLAB, CHUNKS_PER_ITEM):
        items += sel_items([n_full * CHUNKS_PER_SLAB + c + d for d in range(CHUNKS_PER_ITEM)], True)
    for i0 in range(0, n_wblk, CHUNKS_PER_ITEM):
        blks = list(range(i0, min(i0 + CHUNKS_PER_ITEM, n_wblk)))
        kbs_ld = [jnp.maximum(qb - (n_wblk - 1) + i, 0) for i in blks]
        items += [(functools.partial(lambda h, qc, blks: [win_scores(i, h, qc) for i in blks], h, qc, blks),
                   functools.partial(lambda sts, h, qc, kbs_ld: tile_update(1, h, qc, sts, vwT_ref, kbs_ld),
                                     h=h, qc=qc, kbs_ld=kbs_ld))
                  for h, qc in tiles]
    run_pipelined(items)

    ot_parts = []
    for h in range(NSA_KV_HEADS):
        o_parts = []
        for qc in range(N_QCHUNK):
            outs = []
            for br in range(2):
                acc = acc_s[br, h, qc]
                outs.append(acc[0:NSA_HEAD_DIM, :] * (1.0 / acc[NSA_HEAD_DIM:NSA_HEAD_DIM + 1, :]))
            o_parts.append(outs)

        def gate_row(br):
            return jnp.concatenate([gT_ref[0, (h * NSA_GQA + g) * 3 + br:(h * NSA_GQA + g) * 3 + br + 1, :]
                                    for g in range(NSA_GQA)], axis=1)

        os_t = jnp.concatenate([o_parts[qc][0] for qc in range(N_QCHUNK)], axis=1)
        ow_t = jnp.concatenate([o_parts[qc][1] for qc in range(N_QCHUNK)], axis=1)
        o_h = gate_row(0) * oc_s[h] + gate_row(1) * os_t + gate_row(2) * ow_t
        for g in range(NSA_GQA):
            ot_parts.append(o_h[:, g * Q_BLOCK:(g + 1) * Q_BLOCK])
    o_t = _bf(jnp.concatenate(ot_parts, axis=0))
    o_ref[...] = _dot_tn(o_t, wn_ref[...])


def _selection_map_t(n_sel, n_cmp_rows):
    i = np.arange(n_cmp_rows)[None, :]
    j = np.arange(n_sel)[:, None]
    ov = (i * CMP_STRIDE < (j + 1) * SEL_BLOCK) & (i * CMP_STRIDE + CMP_LEN > j * SEL_BLOCK)
    return ov


def _nsa_prompt(qT, qrT, gT, kc, vcT, ks, vsT, kw, vwT, wn, bsz, t):
    nqb = t // Q_BLOCK
    n_sub = t // CMP_STRIDE
    n_cmp = n_sub - CMP_LEN // CMP_STRIDE + 1
    n_sel = t // SEL_BLOCK
    smap = _selection_map_t(n_sel, n_sub) & (np.arange(n_sub)[None, :] < n_cmp)
    smap_t = jnp.asarray(smap, dtype=BF16)
    blk_of_key = (np.arange(KEY_SLAB) // SEL_BLOCK).reshape(CHUNKS_PER_SLAB, Q_BLOCK, 1)
    econst = jnp.asarray(blk_of_key == np.arange(LANES)[None, None, :], dtype=BF16)
    nsl = t // KEY_SLAB
    nq = NSA_GQA * Q_BLOCK
    per_b3 = lambda shape: pl.BlockSpec(shape, lambda b, j: (b, 0, 0))
    per_b4 = lambda shape: pl.BlockSpec(shape, lambda b, j: (b, 0, 0, 0))
    return pl.pallas_call(
        _nsa_prompt_kernel,
        grid=(bsz, nqb),
        in_specs=[
            pl.BlockSpec((1, D_MODEL, Q_BLOCK), lambda b, j: (b, 0, j)),
            pl.BlockSpec((1, D_MODEL, Q_BLOCK), lambda b, j: (b, 0, j)),
            pl.BlockSpec((1, 3 * NSA_HEADS, Q_BLOCK), lambda b, j: (b, 0, j)),
            per_b3((1, n_sub, KV_HALF)), per_b3((1, KV_HALF, n_sub)),
            per_b4((1, nqb, Q_BLOCK, KV_HALF)), per_b4((1, nqb, KV_HALF, Q_BLOCK)),
            per_b4((1, nqb, Q_BLOCK, KV_HALF)), per_b4((1, nqb, KV_HALF, Q_BLOCK)),
            pl.BlockSpec((n_sel, n_sub), lambda b, j: (0, 0)),
            pl.BlockSpec((CHUNKS_PER_SLAB, Q_BLOCK, LANES), lambda b, j: (0, 0, 0)),
            pl.BlockSpec((D_MODEL, D_MODEL), lambda b, j: (0, 0)),
        ],
        out_specs=pl.BlockSpec((Q_BLOCK, D_MODEL), lambda b, j: (b * nqb + j, 0)),
        out_shape=jax.ShapeDtypeStruct((bsz * t, D_MODEL), F32),
        scratch_shapes=[pltpu.VMEM((NSA_KV_HEADS, N_QCHUNK, KV_HALF, QCHUNK), BF16),
                        pltpu.VMEM((NSA_KV_HEADS, nsl, 16, nq), BF16),
                        pltpu.VMEM((NSA_KV_HEADS, NSA_HEAD_DIM, nq), F32),
                        pltpu.VMEM((2, NSA_KV_HEADS, N_QCHUNK, 8, QCHUNK), F32),
                        pltpu.VMEM((2, NSA_KV_HEADS, N_QCHUNK, NSA_HEAD_DIM + 16, QCHUNK), F32),
                        pltpu.VMEM((NSA_KV_HEADS, n_sel, Q_BLOCK), F32)],
        compiler_params=_cparams(("parallel", "arbitrary")),
        name="nsa_prompt",
    )(qT, qrT, gT, kc, vcT, ks, vsT, kw, vwT, smap_t, econst, wn)


def _merge_kernel(x_ref, mg_ref, ys_ref, bn_ref, ws_ref, wo_ref, g_ref, b_ref, h_ref):
    mg = mg_ref[...]
    b_ssm = _dot(ys_ref[...], ws_ref[...])
    mix = jax.nn.sigmoid(mg[:, :D_MODEL]) * b_ssm + jax.nn.sigmoid(mg[:, D_MODEL:]) * bn_ref[...]
    pre = DEEPNORM_ALPHA * x_ref[...] + _dot(_bf(mix), wo_ref[...])
    h_ref[...] = _layer_norm(pre, g_ref[...], b_ref[...])


def _merge(x, proj, y_ssm, b_nsa, ws, wo, g, b, tm):
    m = x.shape[0]
    rowblk = lambda shape, c=0: pl.BlockSpec(shape, lambda i, c=c: (i, c))
    const = lambda shape: pl.BlockSpec(shape, lambda i: (0, 0))
    return pl.pallas_call(
        _merge_kernel,
        grid=(m // tm,),
        in_specs=[rowblk((tm, D_MODEL)), rowblk((tm, 2 * D_MODEL), COL_MG // (2 * D_MODEL)),
                  rowblk((tm, SSM_D_INNER)), rowblk((tm, D_MODEL)),
                  const((SSM_D_INNER, D_MODEL)), const((D_MODEL, D_MODEL)), const((1, D_MODEL)), const((1, D_MODEL))],
        out_specs=rowblk((tm, D_MODEL)),
        out_shape=jax.ShapeDtypeStruct((m, D_MODEL), F32),
        compiler_params=_cparams(("parallel",)),
        name="merge_ln1",
    )(x, proj, y_ssm, b_nsa, ws, wo, g, b)


def _ffn_kernel(h_ref, wg_ref, wu_ref, wd_ref, g_ref, b_ref, y_ref, acc_s):
    h = h_ref[...]
    hb = _bf(h)
    acc_s[...] = jnp.zeros_like(acc_s)

    def body(c, carry):
        gate = _dot(hb, wg_ref[c])
        up = _dot(hb, wu_ref[c])
        acc_s[...] += _dot(_bf(_silu(gate) * up), wd_ref[c])
        return carry

    lax.fori_loop(0, wg_ref.shape[0], body, 0)
    y_ref[...] = _layer_norm(DEEPNORM_ALPHA * h + acc_s[...], g_ref[...], b_ref[...])


def _ffn(h, wg3, wu3, wd3, g, b, tm):
    m = h.shape[0]
    nc = wg3.shape[0]
    const2 = lambda shape: pl.BlockSpec(shape, lambda i: (0, 0))
    const3 = lambda shape: pl.BlockSpec(shape, lambda i: (0, 0, 0))
    return pl.pallas_call(
        _ffn_kernel,
        grid=(m // tm,),
        in_specs=[pl.BlockSpec((tm, D_MODEL), lambda i: (i, 0)),
                  const3((nc, D_MODEL, FFN_CHUNK)), const3((nc, D_MODEL, FFN_CHUNK)), const3((nc, FFN_CHUNK, D_MODEL)),
                  const2((1, D_MODEL)), const2((1, D_MODEL))],
        out_specs=pl.BlockSpec((tm, D_MODEL), lambda i: (i, 0)),
        out_shape=jax.ShapeDtypeStruct((m, D_MODEL), F32),
        scratch_shapes=[pltpu.VMEM((tm, D_MODEL), F32)],
        compiler_params=_cparams(("parallel",)),
        name="ffn_ln2",
    )(h, wg3, wu3, wd3, g, b)


def _ssd_s_pre_kernel(xbc_ref, cst_ref, sm_ref, cw_ref, cb_ref, dtb_ref, alog_ref, e_ref,
                      xs_ref, xdt_ref, bm_ref, cm_ref, dec_ref, ncv_ref):
    xbc = xbc_ref[...]
    acc = cb_ref[...] + cw_ref[SSM_CONV_W - 1:SSM_CONV_W, :] * xbc
    for k in range(SSM_CONV_W - 1):
        acc = acc + cw_ref[k:k + 1, :] * cst_ref[k]
    act = _silu(acc)
    for k in range(SSM_CONV_W - 2):
        ncv_ref[k] = cst_ref[k + 1]
    ncv_ref[SSM_CONV_W - 2] = xbc
    lane = lax.broadcasted_iota(jnp.int32, (1, LANES), 1)
    a_full = jnp.where(lane < SSM_HEADS, -jnp.exp(alog_ref[...]), 0.0)
    dt = _softplus(sm_ref[...] + dtb_ref[...])
    dec_ref[...] = jnp.exp(dt * a_full)
    xs = act[:, :SSM_D_INNER]
    xs_ref[...] = xs
    xdt_ref[...] = xs * _dot_exact_lhs(dt, e_ref[...])
    bm_ref[...] = act[:, SSM_D_INNER:SSM_D_INNER + SSM_GROUPS * SSM_D_STATE]
    cm_ref[...] = act[:, SSM_D_INNER + SSM_GROUPS * SSM_D_STATE:]


def _ssd_s_pre(proj_s, cst, cw, cb, dtb_pad, alog_pad):
    n = proj_s.shape[0]
    e = np.zeros((LANES, SSM_D_INNER), np.float32)
    for h in range(SSM_HEADS):
        e[h, h * SSM_HEAD_DIM:(h + 1) * SSM_HEAD_DIM] = 1.0
    gw = SSM_GROUPS * SSM_D_STATE
    full = lambda shape: pl.BlockSpec(shape, lambda i: (0,) * len(shape))
    return pl.pallas_call(
        _ssd_s_pre_kernel,
        grid=(1,),
        in_specs=[pl.BlockSpec((n, SSM_CONV_DIM), lambda i: (0, COL_XBC // SSM_CONV_DIM)),
                  full(cst.shape),
                  pl.BlockSpec((n, LANES), lambda i: (0, COL_SMALL // LANES)),
                  full(cw.shape), full(cb.shape), full((1, LANES)), full((1, LANES)), full(e.shape)],
        out_specs=[full((n, SSM_D_INNER)), full((n, SSM_D_INNER)), full((n, gw)), full((n, gw)),
                   full((n, LANES)), full(cst.shape)],
        out_shape=[jax.ShapeDtypeStruct((n, SSM_D_INNER), F32), jax.ShapeDtypeStruct((n, SSM_D_INNER), F32),
                   jax.ShapeDtypeStruct((n, gw), F32), jax.ShapeDtypeStruct((n, gw), F32),
                   jax.ShapeDtypeStruct((n, LANES), F32), jax.ShapeDtypeStruct(cst.shape, F32)],
        compiler_params=_cparams(("arbitrary",)),
        name="ssd_sample_pre",
    )(proj_s, cst, proj_s, cw, cb, dtb_pad, alog_pad, jnp.asarray(e, dtype=BF16))


def _dyn_row(ref, b, cols=slice(None)):
    tile = ref[pl.ds(pl.multiple_of((b >> 3) << 3, 8), 8), cols]
    r = lax.broadcasted_iota(jnp.int32, (8, 1), 0)
    return jnp.sum(jnp.where(r == (b & 7), tile, 0.0), axis=0, keepdims=True)


def _onehot_cols(b, n):
    return jnp.where(lax.broadcasted_iota(jnp.int32, (n, n), 0) == b, 1.0, 0.0).astype(BF16)


def _ssd_s_state_kernel(st_ref, xdt_ref, bm_ref, cm_ref, dec_ref, nst_ref, yT_ref, xT_s, dT_s, cT_s):
    b = pl.program_id(0)
    n = xdt_ref.shape[0]

    @pl.when(b == 0)
    def _():
        for i, part in enumerate(_split3(xdt_ref[...].T)):
            xT_s[i] = part
        for i, part in enumerate(_split3(dec_ref[...].T)):
            dT_s[i] = part
        cT_s[...] = _bf(cm_ref[...].T)
        yT_ref[...] = jnp.zeros_like(yT_ref)

    hb = _onehot_cols(b, n)
    r = _dot(xT_s[0], hb) + _dot(xT_s[1], hb) + _dot(xT_s[2], hb)
    dec_r = _dot(dT_s[0], hb) + _dot(dT_s[1], hb) + _dot(dT_s[2], hb)
    c_r = _bf(_dot(cT_s[...], hb))
    lane_is_b = lax.broadcasted_iota(jnp.int32, (1, n), 1) == b
    hpg = SSM_HEADS // SSM_GROUPS
    brow_all = _dyn_row(bm_ref, b)
    for h in range(SSM_HEADS):
        g = h // hpg
        brow = brow_all[:, g * SSM_D_STATE:(g + 1) * SSM_D_STATE]
        rows = slice(h * SSM_HEAD_DIM, (h + 1) * SSM_HEAD_DIM)
        new = st_ref[0, h] * dec_r[h:h + 1, :] + r[rows, :] * brow
        nst_ref[0, h] = new
        y_h = _dot(_bf(new), c_r[g * SSM_D_STATE:(g + 1) * SSM_D_STATE, :])
        yT_ref[rows, :] = jnp.where(lane_is_b, y_h, yT_ref[rows, :])


def _ssd_s_state(state, xdt, bm, cm, dec):
    n = xdt.shape[0]
    assert n == LANES and SSM_D_STATE == LANES
    full = lambda a: pl.BlockSpec(a.shape, lambda b: (0, 0))
    blk = pl.BlockSpec((1, SSM_HEADS, SSM_HEAD_DIM, SSM_D_STATE), lambda b: (b, 0, 0, 0))
    return pl.pallas_call(
        _ssd_s_state_kernel,
        grid=(n,),
        in_specs=[blk, full(xdt), full(bm), full(cm), full(dec)],
        out_specs=[blk, pl.BlockSpec((SSM_D_INNER, n), lambda b: (0, 0))],
        out_shape=[jax.ShapeDtypeStruct(state.shape, F32), jax.ShapeDtypeStruct((SSM_D_INNER, n), F32)],
        scratch_shapes=[pltpu.VMEM((3, SSM_D_INNER, n), BF16), pltpu.VMEM((3, LANES, n), BF16),
                        pltpu.VMEM((SSM_GROUPS * SSM_D_STATE, n), BF16)],
        compiler_params=_cparams(("arbitrary",)),
        name="ssd_sample_state",
    )(state, xdt, bm, cm, dec)


def _ssd_s_post_kernel(yT_ref, xs_ref, z_ref, drow_ref, nw_ref, o_ref):
    y = yT_ref[...].T + drow_ref[...] * xs_ref[...]
    v = y * _silu(z_ref[...])
    gw = SSM_D_INNER // SSM_GROUPS
    outs = []
    for g in range(SSM_GROUPS):
        vg = v[:, g * gw:(g + 1) * gw]
        ms = jnp.sum(vg * vg, axis=-1, keepdims=True) * (1.0 / gw)
        outs.append(vg * lax.rsqrt(ms + NORM_EPS) * nw_ref[:, g * gw:(g + 1) * gw])
    o_ref[...] = _bf(jnp.concatenate(outs, axis=1))


def _ssd_s_post(yT, xs, proj_s, drow, nw):
    n = xs.shape[0]
    full = lambda shape: pl.BlockSpec(shape, lambda i: (0, 0))
    return pl.pallas_call(
        _ssd_s_post_kernel,
        grid=(1,),
        in_specs=[full(yT.shape), full(xs.shape),
                  pl.BlockSpec((n, SSM_D_INNER), lambda i: (0, COL_Z // SSM_D_INNER)),
                  full(drow.shape), full(nw.shape)],
        out_specs=full((n, SSM_D_INNER)),
        out_shape=jax.ShapeDtypeStruct((n, SSM_D_INNER), BF16),
        compiler_params=_cparams(("arbitrary",)),
        name="ssd_sample_post",
    )(yT, xs, proj_s, drow, nw)


PAGES_PER_STEP = 16


def _q_block(q_ref, b, n):
    r_q = _dot(q_ref[0], _onehot_cols(b, n))
    lane = lax.broadcasted_iota(jnp.int32, (1, n), 1)
    blocks = []
    for h in range(NSA_KV_HEADS):
        blk = None
        for g in range(NSA_GQA):
            hd = h * NSA_GQA + g
            piece = jnp.where(lane == hd, r_q[hd * NSA_HEAD_DIM:(hd + 1) * NSA_HEAD_DIM, :], 0.0)
            blk = piece if blk is None else blk + piece
        blocks.append(blk)
    return _bf(jnp.concatenate(blocks, axis=0))


def _scatter_heads(o_t, out_ref, b, n):
    lane = lax.broadcasted_iota(jnp.int32, (1, n), 1)
    lane_is_b = lane == b
    for hd in range(NSA_HEADS):
        h = hd // NSA_GQA
        piece = o_t[h * NSA_HEAD_DIM:(h + 1) * NSA_HEAD_DIM, :]
        col = jnp.sum(jnp.where(lane == hd, piece, 0.0), axis=1, keepdims=True)
        rows = slice(hd * NSA_HEAD_DIM, (hd + 1) * NSA_HEAD_DIM)
        out_ref[rows, :] = jnp.where(lane_is_b, col, out_ref[rows, :])


def _nsa_s_cmp_kernel(pt_ref, *refs):
    pages = refs[:PAGES_PER_STEP]
    (qT_ref, wcat_ref, w2bd_ref, bias_ref, perm_ref, smapT_ref, gsum_ref,
     otc_ref, sel_ref, xcat_s, pab_s) = refs[PAGES_PER_STEP:]
    b = pl.program_id(0)
    j = pl.program_id(1)
    nj = pl.num_programs(1)
    n = qT_ref.shape[2]
    ns = CMP_CHUNK // CMP_STRIDE
    sub_per_pair = 2 * PAGE_SIZE // CMP_STRIDE
    for i in range(PAGES_PER_STEP // 2):
        pair = _bf(jnp.concatenate([pages[2 * i][0], pages[2 * i + 1][0]], axis=1))
        t_perm = _bf(_dot_nt(perm_ref[...], pair))
        for lp in range(CMP_STRIDE):
            for v in range(2):
                xcat_s[v, i * sub_per_pair:(i + 1) * sub_per_pair, lp * KV_HALF:(lp + 1) * KV_HALF] = (
                    t_perm[lp * sub_per_pair:(lp + 1) * sub_per_pair, v * KV_HALF:(v + 1) * KV_HALF])
    for v in range(2):
        pab_s[v, pl.ds(pl.multiple_of(j * ns, ns), ns), :] = _dot(xcat_s[v], wcat_ref[v])

    @pl.when((b == 0) & (j == 0))
    def _():
        otc_ref[...] = jnp.zeros_like(otc_ref)

    @pl.when(j == nj - 1)
    def _():
        kc, vc = _compress_finish(pab_s, bias_ref[...], w2bd_ref)
        n_sub = kc.shape[0]
        past_len = n_sub * CMP_STRIDE
        qblk = _q_block(qT_ref, b, n)
        crow = lax.broadcasted_iota(jnp.int32, (n_sub, 1), 0)
        valid = (crow * CMP_STRIDE + (CMP_LEN - 1)) <= past_len
        sc = jnp.where(valid, _dot(_bf(kc), qblk), NEG)
        m = jnp.max(sc, axis=0, keepdims=True)
        e = jnp.where(valid, jnp.exp2(sc - m), 0.0)
        p = e * (1.0 / jnp.maximum(jnp.sum(e, axis=0, keepdims=True), 1e-30))
        _scatter_heads(_dot(_bf(vc.T), _bf(p)), otc_ref, b, n)
        p_hi = _bf(p)
        p_lo = _bf(p - p_hi.astype(F32))
        psum = _dot(p_hi, gsum_ref[...]) + _dot(p_lo, gsum_ref[...])
        q_hi = _bf(psum)
        q_lo = _bf(psum - q_hi.astype(F32))
        imp_t = _dot(smapT_ref[...], q_hi) + _dot(smapT_ref[...], q_lo)
        nj_pad = imp_t.shape[0]
        n_sel = past_len // SEL_BLOCK + 1
        cur = float(past_len // SEL_BLOCK)
        jrow = lax.broadcasted_iota(jnp.int32, (nj_pad, n), 0).astype(F32)
        forced = (jnp.where(jrow == 0.0, 1.0, 0.0) + jnp.where(jrow == cur, 1.0, 0.0)
                  + jnp.where(jrow == cur - 1.0, 1.0, 0.0))
        score = jnp.where(jrow > cur, -BIG, jnp.where(forced > 0.5, BIG, imp_t))
        score = jnp.where(jrow >= float(n_sel), -jnp.inf, score)
        sel_ref[0] = _topk_mask(score, jrow, min(SEL_TOP_N, n_sel))


def _page_specs(n_pages_step):
    return [pl.BlockSpec((1, KV_COLS, PAGE_SIZE),
                         functools.partial(lambda b, j, pt, i: (pt[b, j * n_pages_step + i], 0, 0), i=i))
            for i in range(n_pages_step)]


def _nsa_s_cmp(page_table, cache_cmp, qT, wcat, w2bd, bias, past_len):
    nseq = page_table.shape[0]
    pair_tok = 2 * PAGE_SIZE
    r = np.arange(pair_tok)
    perm = jnp.asarray(r[None, :] == ((r % (pair_tok // CMP_STRIDE)) * CMP_STRIDE + r // (pair_tok // CMP_STRIDE))[:, None],
                       dtype=BF16)
    n_sub = past_len // CMP_STRIDE
    n_cmp = n_sub - CMP_LEN // CMP_STRIDE + 1
    n_sel = past_len // SEL_BLOCK + 1
    nj_pad = -(-n_sel // 8) * 8
    smap = np.zeros((nj_pad, n_sub), bool)
    smap[:n_sel] = _selection_map_t(n_sel, n_sub) & (np.arange(n_sub)[None, :] < n_cmp)
    lanes = np.arange(nseq)
    gsum = ((lanes[:, None] // NSA_GQA) == (lanes[None, :] // NSA_GQA)) & (lanes[:, None] < NSA_HEADS) & (lanes[None, :] < NSA_HEADS)
    nj = past_len // CMP_CHUNK
    ns = CMP_CHUNK // CMP_STRIDE
    c2 = lambda a: pl.BlockSpec(a.shape, lambda b, j, pt: (0, 0))
    c3 = lambda a: pl.BlockSpec(a.shape, lambda b, j, pt: (0, 0, 0))
    smap_j = jnp.asarray(smap, dtype=BF16)
    gsum_j = jnp.asarray(gsum, dtype=BF16)
    grid_spec = pltpu.PrefetchScalarGridSpec(
        num_scalar_prefetch=1,
        grid=(nseq, nj),
        in_specs=_page_specs(PAGES_PER_STEP) + [c3(qT), c3(wcat), c3(w2bd), c2(bias), c2(perm), c2(smap_j), c2(gsum_j)],
        out_specs=[pl.BlockSpec((D_MODEL, nseq), lambda b, j, pt: (0, 0)),
                   pl.BlockSpec((1, nj_pad, nseq), lambda b, j, pt: (b, 0, 0))],
        scratch_shapes=[pltpu.VMEM((2, ns, CMP_STRIDE * KV_HALF), BF16),
                        pltpu.VMEM((2, n_sub, KV_COLS), F32)],
    )
    return pl.pallas_call(
        _nsa_s_cmp_kernel,
        grid_spec=grid_spec,
        out_shape=[jax.ShapeDtypeStruct((D_MODEL, nseq), F32),
                   jax.ShapeDtypeStruct((nseq, nj_pad, nseq), F32)],
        compiler_params=_cparams(("arbitrary", "arbitrary")),
        name="nsa_sample_cmp",
    )(page_table, *([cache_cmp] * PAGES_PER_STEP), qT, wcat, w2bd, bias, perm, smap_j, gsum_j)


def _nsa_s_att_kernel(n_past_blk, pt_ref, *refs):
    pages = refs[:PAGES_PER_STEP]
    (qrT_ref, sel_ref, nselT_ref, win_ref, nwinT_ref,
     ots_ref, otw_ref, nwo_ref, qblk_s, m_s, acc_s) = refs[PAGES_PER_STEP:]
    b = pl.program_id(0)
    j = pl.program_id(1)
    nj = pl.num_programs(1)
    n = qrT_ref.shape[2]
    blk_per_step = PAGES_PER_STEP * PAGE_SIZE // SEL_BLOCK

    @pl.when((b == 0) & (j == 0))
    def _():
        ots_ref[...] = jnp.zeros_like(ots_ref)
        otw_ref[...] = jnp.zeros_like(otw_ref)

    @pl.when(j == 0)
    def _():
        qblk_s[...] = _q_block(qrT_ref, b, n)
        m_s[...] = jnp.full(m_s.shape, NEG, F32)
        acc_s[...] = jnp.zeros_like(acc_s)

    qblk = qblk_s[...]

    def keys_update(kv_t, mask, m, acc):
        st = jnp.where(mask, _dot_tn(_bf(kv_t[:KV_HALF, :]), qblk), NEG)
        vt = jnp.concatenate([_bf(kv_t[KV_HALF:, :]), jnp.ones((16, kv_t.shape[1]), BF16)], axis=0)
        return _online_step(st, vt, m, acc)

    lane_n = lax.broadcasted_iota(jnp.int32, (1, n), 1)
    key_is0 = lax.broadcasted_iota(jnp.int32, (PAGE_SIZE, 1), 0) == 0

    def new_token_tile(ref):
        col = jnp.sum(jnp.where(lane_n == b, ref[...], 0.0), axis=1, keepdims=True)
        return jnp.where(lax.broadcasted_iota(jnp.int32, (1, PAGE_SIZE), 1) == 0, col, 0.0), col

    kv_step = jnp.concatenate([pages[i][0] for i in range(PAGES_PER_STEP)], axis=1)
    sel_rows = sel_ref[0, pl.ds(pl.multiple_of(j * blk_per_step, blk_per_step), blk_per_step), :]
    mask = jnp.concatenate([jnp.broadcast_to(sel_rows[r:r + 1, :], (SEL_BLOCK, n))
                            for r in range(blk_per_step)], axis=0) > 0.5
    m, acc = keys_update(kv_step, mask, m_s[...], acc_s[...])
    m_s[...] = m
    acc_s[...] = acc

    @pl.when(j == nj - 1)
    def _():
        kv_new, _ = new_token_tile(nselT_ref)
        sel_new = sel_ref[0, n_past_blk:n_past_blk + 1, :] > 0.5
        m2, acc2 = keys_update(kv_new, key_is0 & sel_new, m_s[...], acc_s[...])
        _scatter_heads(acc2[0:KV_HALF, :] / acc2[KV_HALF:KV_HALF + 1, :], ots_ref, b, n)
        win_t = win_ref[0]
        w_keys = win_t.shape[1]
        kv_new, new_col = new_token_tile(nwinT_ref)
        valid = jnp.concatenate([jnp.full((w_keys, 1), True), key_is0], axis=0)
        _, aw = keys_update(jnp.concatenate([win_t, kv_new], axis=1), valid,
                            jnp.full(m_s.shape, NEG, F32), jnp.zeros_like(acc2))
        _scatter_heads(aw[0:KV_HALF, :] / aw[KV_HALF:KV_HALF + 1, :], otw_ref, b, n)
        lane_w = lax.broadcasted_iota(jnp.int32, (1, w_keys), 1)
        nwo_ref[0] = jnp.where(lane_w == w_keys - 1, new_col, pltpu.roll(win_t, w_keys - 1, 1))


def _nsa_s_att(page_table, cache_sel, qrT, sel, nsel_rows, cache_win, nwin_rows, past_len):
    nseq = page_table.shape[0]
    nj = past_len // (PAGES_PER_STEP * PAGE_SIZE)
    wbuf = cache_win.shape[2]
    c2 = lambda a: pl.BlockSpec(a.shape, lambda b, j, pt: (0, 0))
    c3 = lambda a: pl.BlockSpec(a.shape, lambda b, j, pt: (0, 0, 0))
    acc_rows = KV_HALF + 16
    grid_spec = pltpu.PrefetchScalarGridSpec(
        num_scalar_prefetch=1,
        grid=(nseq, nj),
        in_specs=_page_specs(PAGES_PER_STEP) + [
            c3(qrT),
            pl.BlockSpec((1,) + sel.shape[1:], lambda b, j, pt: (b, 0, 0)),
            c2(nsel_rows),
            pl.BlockSpec((1, KV_COLS, wbuf), lambda b, j, pt: (b, 0, 0)),
            c2(nwin_rows)],
        out_specs=[pl.BlockSpec((D_MODEL, nseq), lambda b, j, pt: (0, 0)),
                   pl.BlockSpec((D_MODEL, nseq), lambda b, j, pt: (0, 0)),
                   pl.BlockSpec((1, KV_COLS, wbuf), lambda b, j, pt: (b, 0, 0))],
        scratch_shapes=[pltpu.VMEM((KV_HALF, nseq), BF16), pltpu.VMEM((1, nseq), F32),
                        pltpu.VMEM((acc_rows, nseq), F32)],
    )
    return pl.pallas_call(
        functools.partial(_nsa_s_att_kernel, past_len // SEL_BLOCK),
        grid_spec=grid_spec,
        out_shape=[jax.ShapeDtypeStruct((D_MODEL, nseq), F32), jax.ShapeDtypeStruct((D_MODEL, nseq), F32),
                   jax.ShapeDtypeStruct((nseq, KV_COLS, wbuf), F32)],
        compiler_params=_cparams(("arbitrary", "arbitrary")),
        name="nsa_sample_att",
    )(page_table, *([cache_sel] * PAGES_PER_STEP), qrT, sel, nsel_rows, cache_win, nwin_rows)


def _nsa_s_out_kernel(otc_ref, ots_ref, otw_ref, gT_ref, wn_ref, o_ref):
    n = otc_ref.shape[1]
    parts = []
    for hd in range(NSA_HEADS):
        rows = slice(hd * NSA_HEAD_DIM, (hd + 1) * NSA_HEAD_DIM)
        parts.append(gT_ref[0, 3 * hd:3 * hd + 1, :] * otc_ref[rows, :]
                     + gT_ref[0, 3 * hd + 1:3 * hd + 2, :] * ots_ref[rows, :]
                     + gT_ref[0, 3 * hd + 2:3 * hd + 3, :] * otw_ref[rows, :])
    o_t = _bf(jnp.concatenate(parts, axis=0))
    o_ref[...] = _dot_tn(o_t, wn_ref[...])


def _nsa_s_out(otc, ots, otw, gT, wn):
    n = otc.shape[1]
    f2 = lambda a: pl.BlockSpec(a.shape, lambda i: (0, 0))
    f3 = lambda a: pl.BlockSpec(a.shape, lambda i: (0, 0, 0))
    return pl.pallas_call(
        _nsa_s_out_kernel,
        grid=(1,),
        in_specs=[f2(otc), f2(ots), f2(otw), f3(gT), f2(wn)],
        out_specs=pl.BlockSpec((n, D_MODEL), lambda i: (0, 0)),
        out_shape=jax.ShapeDtypeStruct((n, D_MODEL), F32),
        compiler_params=_cparams(("arbitrary",)),
        name="nsa_sample_out",
    )(otc, ots, otw, gT, wn)


def _rope_tables(pos):
    half = ROPE_DIM // 2
    inv_freq = jnp.power(ROPE_THETA, -jnp.arange(half, dtype=F32) * 2.0 / ROPE_DIM)
    ang = pos.astype(F32)[None, :] * inv_freq[:, None]
    return jnp.cos(ang), jnp.sin(ang)


def _permute_w_in(w_in):
    sizes = (SSM_D_INNER, SSM_CONV_DIM, SSM_HEADS, NSA_HEADS * NSA_HEAD_DIM, KV_COLS, KV_COLS, KV_COLS,
             3 * NSA_HEADS, 2 * D_MODEL)
    offs = np.concatenate([[0], np.cumsum(sizes)])
    z, xbc, dt, q, kvc, kvs, kvw, ng, mg = (w_in[:, offs[i]:offs[i + 1]] for i in range(len(sizes)))
    pad = jnp.zeros((D_MODEL, PROJ_COLS - COL_SMALL - SSM_HEADS - 3 * NSA_HEADS), w_in.dtype)
    return _bf(jnp.concatenate([xbc, q, z, mg, kvc, kvs, kvw, dt, ng, pad], axis=1))


def _compress_weights(cmp_pe, cmp_w1, cmp_w2):
    eye = jnp.eye(NSA_KV_HEADS, dtype=F32)
    w1a = cmp_w1[:, :CMP_STRIDE]
    w1b = cmp_w1[:, CMP_STRIDE:]
    bd = lambda w: jnp.einsum('vlde,hk->vlhdke', w, eye).reshape(2, CMP_STRIDE * KV_HALF, KV_HALF)
    wcat = _bf(jnp.concatenate([bd(w1a), bd(w1b)], axis=2))
    w2bd = _bf(jnp.einsum('vef,hk->vhekf', cmp_w2, eye).reshape(2, KV_HALF, KV_HALF))
    pe8 = jnp.zeros((2, 8, CMP_LEN * NSA_HEAD_DIM), F32).at[:, 0, :].set(cmp_pe.reshape(2, -1))
    w1f4 = jnp.tile(cmp_w1.reshape(2, CMP_LEN * NSA_HEAD_DIM, NSA_HEAD_DIM), (1, 1, NSA_KV_HEADS))
    return wcat, w2bd, pe8, w1f4


def kernel(x_prompt, x_sample, cache_cmp_kv, cache_sel_kv, cache_win_kv, state_ssm, state_conv, page_table,
           w_in, conv_w, conv_b, dt_bias, a_log, d_skip, ssm_norm_w, w_ssm_out, cmp_pe, cmp_w1, cmp_w2,
           w_nsa_out, w_o, ln1_g, ln1_b, w_gate, w_up, w_down, ln2_g, ln2_b):
    bsz, t, _ = x_prompt.shape
    nseq, dec_seq, _ = x_sample.shape
    n_pool = cache_cmp_kv.shape[1]
    past_len = page_table.shape[1] * PAGE_SIZE
    assert w_in.shape[0] == 1 and dec_seq == 1 and nseq == LANES
    assert t % CMP_CHUNK == 0 and past_len % CMP_CHUNK == 0 and cache_win_kv.shape[2] == WINDOW

    w_in_p = _permute_w_in(w_in[0])
    pad_row = lambda v: jnp.zeros((1, LANES), F32).at[0, :SSM_HEADS].set(v)
    dtb_pad, alog_pad = pad_row(dt_bias[0]), pad_row(a_log[0])
    drow = jnp.repeat(d_skip[0], SSM_HEAD_DIM)[None, :]
    nw = ssm_norm_w[0][None, :]
    cw, cb = conv_w[0], conv_b[0][None, :]
    wcat, w2bd, pe8, w1f4 = _compress_weights(cmp_pe[0], cmp_w1[0], cmp_w2[0])
    w_ssm_b, w_nsa_b, w_o_b = _bf(w_ssm_out[0]), _bf(w_nsa_out[0]), _bf(w_o[0])
    nc = FFN_HIDDEN // FFN_CHUNK
    wg3 = _bf(w_gate[0]).reshape(D_MODEL, nc, FFN_CHUNK).transpose(1, 0, 2)
    wu3 = _bf(w_up[0]).reshape(D_MODEL, nc, FFN_CHUNK).transpose(1, 0, 2)
    wd3 = _bf(w_down[0]).reshape(nc, FFN_CHUNK, D_MODEL)
    g1, b1, g2, b2 = ln1_g[0][None, :], ln1_b[0][None, :], ln2_g[0][None, :], ln2_b[0][None, :]

    xp = x_prompt.reshape(bsz * t, D_MODEL)
    proj = _matmul(xp, w_in_p, 2048, PROJ_TN, "in_proj")
    y_ssm, new_ssm_p, new_conv_p = _ssd_prompt(proj, bsz, t, cw, cb, dtb_pad, alog_pad, drow, nw)
    cos_p, sin_p = _rope_tables(jnp.arange(t, dtype=jnp.int32))
    qT, qrT, ks, vsT, kw, vwT, gT, ncmp, nsel, nwin = _attn_prep(proj, bsz, t, KEY_SLAB, cos_p, sin_p)
    cmp_bias = _pe_bias(pe8, w1f4)
    kc, vcT = _compress_prompt(proj, bsz, t, wcat, w2bd, cmp_bias)
    b_nsa = _nsa_prompt(qT, qrT, gT, kc, vcT, ks, vsT, kw, vwT, w_nsa_b, bsz, t)
    h = _merge(xp, proj, y_ssm, b_nsa, w_ssm_b, w_o_b, g1, b1, 512)
    y_p = _ffn(h, wg3, wu3, wd3, g2, b2, 512).reshape(bsz, t, D_MODEL)

    def kv6(a_t):
        n_b, _, n_t = a_t.shape
        return jnp.moveaxis(a_t.reshape(n_b, 2, NSA_KV_HEADS, NSA_HEAD_DIM, n_t), -1, 1)[None]

    w_keep = min(WINDOW, t)
    new_win_p = kv6(nwin[:, :, t - w_keep:])

    xs_in = x_sample.reshape(nseq, D_MODEL)
    proj_s = _matmul(xs_in, w_in_p, nseq, PROJ_TN, "in_proj_s")
    cst = jnp.moveaxis(state_conv[0], 1, 0)
    xs_s, xdt_s, bm_s, cm_s, dec_s, ncv_s = _ssd_s_pre(proj_s, cst, cw, cb, dtb_pad, alog_pad)
    new_ssm_s, yT_s = _ssd_s_state(state_ssm[0], xdt_s, bm_s, cm_s, dec_s)
    y_ssm_s = _ssd_s_post(yT_s, xs_s, proj_s, drow, nw)
    cos_s, sin_s = _rope_tables(jnp.full((nseq,), past_len, dtype=jnp.int32))
    qT_s, qrT_s, _, _, _, _, gT_s, ncmp_s, nsel_s, nwin_s = _attn_prep(proj_s, 1, nseq, nseq, cos_s, sin_s)
    fmaj = lambda c, n_lead: jnp.moveaxis(c, 1, -1).reshape(n_lead, KV_COLS, c.shape[1])
    cache_cmp = fmaj(cache_cmp_kv[0], n_pool)
    cache_sel = fmaj(cache_sel_kv[0], n_pool)
    cache_win = fmaj(cache_win_kv[0], nseq)
    otc, sel = _nsa_s_cmp(page_table, cache_cmp, qT_s, wcat, w2bd, cmp_bias, past_len)
    ots, otw, new_win_t = _nsa_s_att(page_table, cache_sel, qrT_s, sel, nsel_s[0], cache_win, nwin_s[0], past_len)
    kv6_s = lambda a_t: kv6(jnp.transpose(a_t, (2, 1, 0)))
    b_nsa_s = _nsa_s_out(otc, ots, otw, gT_s, w_nsa_b)
    h_s = _merge(xs_in, proj_s, y_ssm_s, b_nsa_s, w_ssm_b, w_o_b, g1, b1, nseq)
    y_s = _ffn(h_s, wg3, wu3, wd3, g2, b2, nseq).reshape(nseq, 1, D_MODEL)

    return (y_p, y_s,
            kv6(ncmp), kv6(nsel), new_win_p,
            new_ssm_p[None], new_conv_p[None],
            kv6_s(ncmp_s), kv6_s(nsel_s),
            kv6(new_win_t),
            new_ssm_s[None],
            jnp.moveaxis(ncv_s, 0, 1)[None])
```
